```python
import math
import jax, jax.numpy as jnp
from jax import lax
import numpy as np

D_MODEL = 1024
BATCH = 8
SEQ = 4096
DEPTH = 2

N_A_LAYERS = DEPTH // 2
N_B_LAYERS = DEPTH - N_A_LAYERS

PLE_DIM = 256
D_FF = 4 * D_MODEL
NORM_EPS = 1e-6

RWKV_HEAD = 64
RWKV_HEADS = D_MODEL // RWKV_HEAD
DECAY_LORA = max(32, int(round(D_MODEL ** 0.5 * 1.8 / 32)) * 32)
AAA_LORA = max(32, int(round(D_MODEL ** 0.5 * 1.8 / 32)) * 32)
GATE_LORA = max(32, int(round(0.6 * D_MODEL ** 0.8 / 32)) * 32)
GN_EPS = 64e-5

DIFF_HEAD = 64
DIFF_HEADS = D_MODEL // (2 * DIFF_HEAD)
DIFF_QK = DIFF_HEADS * 2 * DIFF_HEAD
DIFF_V = DIFF_HEADS * 2 * DIFF_HEAD
Q_BLOCK = 128

N_BUCKETS = 32
MAX_DIST = 128

kernel_name = "yoco_rwkv7_diffattn_hybrid"


def rmsnorm(x, g, eps=NORM_EPS):
    xf = x.astype(jnp.float32)
    y = xf * lax.rsqrt(jnp.mean(xf * xf, axis=-1, keepdims=True) + eps)
    return (y * g.astype(jnp.float32)).astype(x.dtype)


def squared_relu_mlp(h, w1, w2):
    return jnp.square(jax.nn.relu(h @ w1)) @ w2


def ple_inject(x, p_i, g, w_up, w_gate):
    gate = jax.nn.sigmoid(rmsnorm(x, g) @ w_gate)
    return x + (p_i @ w_up) * gate


def rwkv7_recurrence(r, w, k, v, a, b):
    Bn, _, H, N = r.shape
    tm = lambda t: jnp.moveaxis(t.astype(jnp.float32), 1, 0)

    def step(S, inp):
        r_t, w_t, k_t, v_t, a_t, b_t = inp
        Sa = jnp.einsum('bhvk,bhk->bhv', S, a_t)
        S = S * w_t[:, :, None, :] + Sa[..., None] * b_t[:, :, None, :] + v_t[..., None] * k_t[:, :, None, :]
        y = jnp.einsum('bhvk,bhk->bhv', S, r_t)
        return S, y

    S0 = jnp.zeros((Bn, H, N, N), jnp.float32)
    _, y = lax.scan(step, S0, (tm(r), tm(w), tm(k), tm(v), tm(a), tm(b)))
    return jnp.moveaxis(y, 0, 1)


def rwkv7_time_mix(h, mix, w_rkvo, w0, w1, w2, a0, a1, a2, g1, g2, k_k, k_a, r_k, ln_w, ln_b):
    Bn, T, C = h.shape
    H, N = RWKV_HEADS, RWKV_HEAD
    heads = lambda t: t.reshape(Bn, T, H, N)
    h_prev = jnp.pad(h[:, :-1], ((0, 0), (1, 0), (0, 0)))
    dx = h_prev - h
    xr = h + dx * mix[0]
    xw = h + dx * mix[1]
    xk = h + dx * mix[2]
    xv = h + dx * mix[3]
    xa = h + dx * mix[4]
    xg = h + dx * mix[5]
    r = xr @ w_rkvo[0]
    k = xk @ w_rkvo[1]
    v = xv @ w_rkvo[2]
    w_log = -jax.nn.softplus(-(w0 + jnp.tanh(xw @ w1) @ w2)) - 0.5
    decay = jnp.exp(-jnp.exp(w_log.astype(jnp.float32)))
    a = jax.nn.sigmoid(a0 + (xa @ a1) @ a2)
    g = jax.nn.sigmoid(xg @ g1) @ g2
    kk = heads((k * k_k).astype(jnp.float32))
    kk = kk / jnp.maximum(jnp.sqrt(jnp.sum(kk * kk, axis=-1, keepdims=True)), 1e-12)
    k = k * (1 + (a - 1) * k_a)
    y = rwkv7_recurrence(heads(r), heads(decay), heads(k), heads(v), -kk, kk * heads(a).astype(jnp.float32))
    mu = jnp.mean(y, axis=-1, keepdims=True)
    var = jnp.mean(jnp.square(y - mu), axis=-1, keepdims=True)
    y = ((y - mu) * lax.rsqrt(var + GN_EPS)).reshape(Bn, T, C)
    y = (y * ln_w.astype(jnp.float32) + ln_b.astype(jnp.float32)).astype(h.dtype)
    bonus = jnp.sum(heads(r) * heads(k) * r_k, axis=-1, keepdims=True) * heads(v)
    y = (y + bonus.reshape(Bn, T, C)) * g
    return y @ w_rkvo[3]


def t5_bucket(rel):
    n = jnp.maximum(rel, 0)
    max_exact = N_BUCKETS // 2
    nf = jnp.maximum(n, 1).astype(jnp.float32)
    large = max_exact + (jnp.log(nf / max_exact) / math.log(MAX_DIST / max_exact)
                         * (N_BUCKETS - max_exact)).astype(jnp.int32)
    large = jnp.minimum(large, N_BUCKETS - 1)
    return jnp.where(n < max_exact, n, large)


def shared_kv(x, kv_norm_g, w_k, w_v, k_norm_g):
    Bn, S, _ = x.shape
    h = rmsnorm(x, kv_norm_g)
    k = rmsnorm((h @ w_k).reshape(Bn, S, DIFF_HEADS, 2, DIFF_HEAD), k_norm_g)
    v = (h @ w_v).reshape(Bn, S, DIFF_HEADS, 2 * DIFF_HEAD)
    return jnp.transpose(k, (0, 2, 3, 1, 4)), jnp.transpose(v, (0, 2, 1, 3))


def diff_attention(h, k, v, rel_bias, w_q, q_norm_g, lam, subln_g, w_o, layer_idx):
    Bn, S, _ = h.shape
    H, d = DIFF_HEADS, DIFF_HEAD
    lambda_init = 0.8 - 0.6 * math.exp(-0.3 * layer_idx)
    lamf = lam.astype(jnp.float32)
    lam_full = (jnp.exp(jnp.sum(lamf[0] * lamf[1])) - jnp.exp(jnp.sum(lamf[2] * lamf[3]))
                + lambda_init)
    q = rmsnorm((h @ w_q).reshape(Bn, S, H, 2, d), q_norm_g)
    nb = S // Q_BLOCK
    qb = jnp.transpose(q.reshape(Bn, nb, Q_BLOCK, H, 2, d), (1, 0, 3, 4, 2, 5))
    k_pos = jnp.arange(S, dtype=jnp.int32)
    scale = d ** -0.5

    def block(args):
        i, q_blk = args
        q_pos = i * Q_BLOCK + jnp.arange(Q_BLOCK, dtype=jnp.int32)
        rel = q_pos[:, None] - k_pos[None, :]
        bias = jnp.transpose(rel_bias[t5_bucket(rel)].astype(jnp.float32), (2, 0, 1))
        logits = jnp.einsum('bhcqd,bhckd->bhcqk', q_blk, k).astype(jnp.float32) * scale
        logits = logits + bias[None, :, None]
        logits = jnp.where(rel >= 0, logits, -jnp.inf)
        probs = jax.nn.softmax(logits, axis=-1)
        attn = probs[:, :, 0] - lam_full * probs[:, :, 1]
        return jnp.einsum('bhqk,bhkv->bhqv', attn.astype(v.dtype), v)

    o = lax.map(block, (jnp.arange(nb, dtype=jnp.int32), qb))
    o = jnp.transpose(o, (1, 0, 3, 2, 4)).reshape(Bn, S, H, 2 * d)
    o = rmsnorm(o, subln_g) * (1 - lambda_init)
    return o.reshape(Bn, S, H * 2 * d) @ w_o


def setup_inputs(seed: int = 0) -> dict:
    key = jax.random.key(seed)
    ks = iter(jax.random.split(key, 40))
    nrm = lambda shape, s: jax.random.normal(next(ks), shape, jnp.float32) * s
    uni = lambda shape, lo, hi: jax.random.uniform(next(ks), shape, jnp.float32, lo, hi)
    D, NA, NB = D_MODEL, N_A_LAYERS, N_B_LAYERS
    return {
        "x": nrm((BATCH, SEQ, D), 1.0),
        "p": nrm((DEPTH, BATCH, SEQ, PLE_DIM), 1.0),
        "norm_g": 1.0 + nrm((DEPTH, 3, D), 0.05),
        "mlp_w1": nrm((DEPTH, D, D_FF), D ** -0.5),
        "mlp_w2": nrm((DEPTH, D_FF, D), D_FF ** -0.5),
        "ple_w_up": nrm((DEPTH, PLE_DIM, D), PLE_DIM ** -0.5),
        "ple_w_gate": nrm((DEPTH, D, D), D ** -0.5),
        "rwkv_mix": uni((NA, 6, D), 0.0, 1.0),
        "rwkv_w_rkvo": nrm((NA, 4, D, D), D ** -0.5),
        "rwkv_w0": uni((NA, D), -4.0, 1.0),
        "rwkv_w1": nrm((NA, D, DECAY_LORA), D ** -0.5),
        "rwkv_w2": nrm((NA, DECAY_LORA, D), 0.5 * DECAY_LORA ** -0.5),
        "rwkv_a0": nrm((NA, D), 0.1),
        "rwkv_a1": nrm((NA, D, AAA_LORA), D ** -0.5),
        "rwkv_a2": nrm((NA, AAA_LORA, D), AAA_LORA ** -0.5),
        "rwkv_g1": nrm((NA, D, GATE_LORA), D ** -0.5),
        "rwkv_g2": nrm((NA, GATE_LORA, D), GATE_LORA ** -0.5),
        "rwkv_k_k": 0.85 + nrm((NA, D), 0.05),
        "rwkv_k_a": 1.0 + nrm((NA, D), 0.05),
        "rwkv_r_k": nrm((NA, RWKV_HEADS, RWKV_HEAD), 0.1),
        "rwkv_ln_w": 1.0 + nrm((NA, D), 0.05),
        "rwkv_ln_b": nrm((NA, D), 0.02),
        "kv_norm_g": 1.0 + nrm((D,), 0.05),
        "w_k_shared": nrm((D, DIFF_QK), D ** -0.5),
        "w_v_shared": nrm((D, DIFF_V), D ** -0.5),
        "k_norm_g": 1.0 + nrm((DIFF_HEAD,), 0.05),
        "diff_w_q": nrm((NB, D, DIFF_QK), D ** -0.5),
        "diff_q_norm_g": 1.0 + nrm((NB, DIFF_HEAD), 0.05),
        "diff_lam": nrm((NB, 4, DIFF_HEAD), 0.1),
        "diff_subln_g": 1.0 + nrm((NB, 2 * DIFF_HEAD), 0.05),
        "diff_w_o": nrm((NB, DIFF_V, D), DIFF_V ** -0.5),
        "rel_bias": nrm((N_BUCKETS, DIFF_HEADS), 0.5),
    }


def reference(x, p, norm_g, mlp_w1, mlp_w2, ple_w_up, ple_w_gate,
              rwkv_mix, rwkv_w_rkvo, rwkv_w0, rwkv_w1, rwkv_w2, rwkv_a0, rwkv_a1, rwkv_a2,
              rwkv_g1, rwkv_g2, rwkv_k_k, rwkv_k_a, rwkv_r_k, rwkv_ln_w, rwkv_ln_b,
              kv_norm_g, w_k_shared, w_v_shared, k_norm_g,
              diff_w_q, diff_q_norm_g, diff_lam, diff_subln_g, diff_w_o, rel_bias):
    k_sh = None
    v_sh = None
    for layer in range(DEPTH):
        if layer < N_A_LAYERS:
            i = layer
            h = rmsnorm(x, norm_g[layer, 0])
            x = x + rwkv7_time_mix(h, rwkv_mix[i], rwkv_w_rkvo[i], rwkv_w0[i], rwkv_w1[i], rwkv_w2[i],
                                   rwkv_a0[i], rwkv_a1[i], rwkv_a2[i], rwkv_g1[i], rwkv_g2[i],
                                   rwkv_k_k[i], rwkv_k_a[i], rwkv_r_k[i], rwkv_ln_w[i], rwkv_ln_b[i])
        else:
            j = layer - N_A_LAYERS
            if j == 0:
                k_sh, v_sh = shared_kv(x, kv_norm_g, w_k_shared, w_v_shared, k_norm_g)
            h = rmsnorm(x, norm_g[layer, 0])
            x = x + diff_attention(h, k_sh, v_sh, rel_bias, diff_w_q[j], diff_q_norm_g[j],
                                   diff_lam[j], diff_subln_g[j], diff_w_o[j], layer)
        x = x + squared_relu_mlp(rmsnorm(x, norm_g[layer, 1]), mlp_w1[layer], mlp_w2[layer])
        x = ple_inject(x, p[layer], norm_g[layer, 2], ple_w_up[layer], ple_w_gate[layer])
    return x
```

```python
import functools
import math

import numpy as np
import jax
import jax.numpy as jnp
from jax import lax
from jax.experimental import pallas as pl
from jax.experimental.pallas import tpu as pltpu

F32 = jnp.float32
BF16 = jnp.bfloat16

D_MODEL = 1024
RWKV_HEAD = 64
DIFF_HEAD = 64
DIFF_HEADS = 8
LANES = 128
CHUNK = 64
SUB = 16
NORM_EPS = 1e-6
GN_EPS = 64e-5
N_BUCKETS = 32
MAX_DIST = 128
NEG_BIG = -1e30
VMEM_LIMIT = 56 * 1024 * 1024


def _dot(a, b):
    return jnp.dot(a.astype(BF16), b.astype(BF16), preferred_element_type=F32)


def _dot_nt(a, b):
    return lax.dot_general(a.astype(BF16), b.astype(BF16), (((1,), (1,)), ((), ())),
                           preferred_element_type=F32)


def _dot_tn(a, b):
    return lax.dot_general(a.astype(BF16), b.astype(BF16), (((0,), (0,)), ((), ())),
                           preferred_element_type=F32)


def _split(x):
    hi = x.astype(BF16)
    lo = (x - hi.astype(F32)).astype(BF16)
    return hi, lo


def _dot_x(a, b_exact):
    hi, lo = _split(a)
    return _dot(hi, b_exact) + _dot(lo, b_exact)


def _mm3(dotfn, a, b):
    ah, al = _split(a)
    bh, bl = _split(b)
    return dotfn(ah, bh) + (dotfn(ah, bl) + dotfn(al, bh))


def _rms(x, g):
    return x * lax.rsqrt(jnp.mean(x * x, axis=-1, keepdims=True) + NORM_EPS) * g


def _sigmoid(x):
    return 1.0 / (1.0 + jnp.exp(-x))


def _const_spec(shape):
    nd = len(shape)
    return pl.BlockSpec(shape, lambda *_: (0,) * nd, pipeline_mode=pl.Buffered(1))


def _row_spec(tm, width):
    return pl.BlockSpec((tm, width), lambda i: (i, 0))


def _rwkv_pre_kernel(x_ref, xp_ref, vec_ref, mix_ref, wr_ref, wk_ref, wv_ref,
                     w1_ref, w2_ref, a1_ref, a2_ref, g1_ref, g2_ref, bd_ref,
                     r_out, lw_out, k_out, v_out, kk_out, a_out, g_out, bonus_out,
                     *, blocks_per_seq):
    i = pl.program_id(0)
    g0 = vec_ref[0:1, :]
    h = _rms(x_ref[...], g0)
    prev = _rms(xp_ref[...], g0)[7:8, :]
    prev = jnp.where(i % blocks_per_seq == 0, 0.0, prev)
    row = lax.broadcasted_iota(jnp.int32, h.shape, 0)
    hprev = jnp.where(row == 0, prev, pltpu.roll(h, 1, 0))
    dx = hprev - h

    def mixed(j):
        return (h + dx * mix_ref[j:j + 1, :]).astype(BF16)

    r = _dot(mixed(0), wr_ref[...])
    k = _dot(mixed(2), wk_ref[...])
    v = _dot(mixed(3), wv_ref[...])
    w_lora = _dot(jnp.tanh(_dot(mixed(1), w1_ref[...])), w2_ref[...])
    a_lora = _dot(_dot(mixed(4), a1_ref[...]), a2_ref[...])
    g = _dot(_sigmoid(_dot(mixed(5), g1_ref[...])), g2_ref[...])

    z = -(vec_ref[1:2, :] + w_lora)
    softplus = jnp.maximum(z, 0.0) + jnp.log1p(jnp.exp(-jnp.abs(z)))
    lw = -jnp.exp(-softplus - 0.5)
    a = _sigmoid(vec_ref[2:3, :] + a_lora)

    bd = bd_ref[...]
    kk = k * vec_ref[3:4, :]
    ss = _dot_x(kk * kk, bd)
    kk = kk / jnp.maximum(jnp.sqrt(ss), 1e-12)
    k = k * (1.0 + (a - 1.0) * vec_ref[4:5, :])
    bonus = _dot_x(r * k * vec_ref[5:6, :], bd) * v

    r_out[...] = r
    lw_out[...] = lw
    k_out[...] = k
    v_out[...] = v
    kk_out[...] = kk
    a_out[...] = a
    g_out[...] = g
    bonus_out[...] = bonus


def _rwkv_pre(x2d, vecs, mix, wr, wk, wv, w1, w2, a1, a2, g1, g2, bd, seq, tm):
    n, d = x2d.shape
    blocks_per_seq = seq // tm
    consts = (vecs, mix, wr, wk, wv, w1, w2, a1, a2, g1, g2, bd)
    out_sd = jax.ShapeDtypeStruct((n, d), F32)
    return pl.pallas_call(
        functools.partial(_rwkv_pre_kernel, blocks_per_seq=blocks_per_seq),
        grid=(n // tm,),
        in_specs=[_row_spec(tm, d),
                  pl.BlockSpec((8, d), lambda i: (jnp.maximum(i * (tm // 8) - 1, 0), 0))]
                 + [_const_spec(c.shape) for c in consts],
        out_specs=[_row_spec(tm, d)] * 8,
        out_shape=[out_sd] * 8,
        compiler_params=pltpu.CompilerParams(dimension_semantics=("arbitrary",),
                                             vmem_limit_bytes=VMEM_LIMIT),
        name="rwkv_pre",
    )(x2d, x2d, *consts)


def _rec_kernel(r_ref, lw_ref, k_ref, v_ref, kk_ref, a_ref, y_ref, s_ref, *, nchunk):
    c2 = 2 * CHUNK

    @pl.when(pl.program_id(2) == 0)
    def _():
        s_ref[...] = jnp.zeros_like(s_ref)

    ri = lax.broadcasted_iota(jnp.int32, (c2, LANES), 0)
    ci = lax.broadcasted_iota(jnp.int32, (c2, LANES), 1)
    top = ri < CHUNK
    left = ci < CHUNK
    ti = ri % CHUNK
    tj = ci % CHUNK
    mask0 = tj < ti + jnp.where(top, 0, 1)
    mask1 = tj < ti + jnp.where(top, 1, 0)
    blockdiag = (ri // CHUNK) == (ci // CHUNK)
    sub_diag = (ri // SUB) == (ci // SUB)
    eye = (ri == ci).astype(F32)
    lo_lane = lax.broadcasted_iota(jnp.int32, (1, LANES), 1) < CHUNK
    tri = (lax.broadcasted_iota(jnp.int32, (CHUNK, CHUNK), 1)
           <= lax.broadcasted_iota(jnp.int32, (CHUNK, CHUNK), 0)).astype(BF16)
    zeros_c = jnp.zeros((CHUNK, LANES), F32)

    def chunk(c, carry):
        sl = pl.ds(pl.multiple_of(c * CHUNK, CHUNK), CHUNK)
        r = r_ref[sl, :]
        lw = lw_ref[sl, :]
        k = k_ref[sl, :]
        v = v_ref[sl, :]
        kk = kk_ref[sl, :]
        a = a_ref[sl, :]
        s = s_ref[...]

        l_hi = lw.astype(BF16)
        l_mid = (lw - l_hi.astype(F32)).astype(BF16)
        l_lo = (lw - l_hi.astype(F32) - l_mid.astype(F32)).astype(BF16)
        cum = _dot(tri, l_hi) + _dot(tri, l_mid) + _dot(tri, l_lo)
        tot = cum[CHUNK - 1:CHUNK, :]
        e_neg = jnp.exp(-cum)
        e_rest = jnp.exp(tot - cum)
        rt = r * jnp.exp(cum)
        at = -kk * jnp.exp(cum - lw)
        kb = kk * a
        bt = kb * e_neg
        kt = k * e_neg

        ar = jnp.concatenate([at, rt], axis=0)
        ra = jnp.concatenate([rt, at], axis=0)
        bk = jnp.concatenate([bt, kt], axis=0)
        kbm = jnp.concatenate([kt, bt], axis=0)
        g0 = jnp.where(mask0, _mm3(_dot_nt, jnp.where(lo_lane, ar, 0.0), bk), 0.0)
        g1 = jnp.where(mask1, _mm3(_dot_nt, jnp.where(lo_lane, 0.0, ra), kbm), 0.0)

        abd = jnp.where(top, jnp.where(left, g0, 0.0), jnp.where(left, 0.0, g1))
        dg = jnp.where(sub_diag, abd, 0.0)
        lo = abd - dg
        mm = functools.partial(_mm3, _dot)
        x = dg
        t = eye + dg
        for _ in range(3):
            x = mm(x, x)
            t = t + mm(t, x)
        nn = mm(t, lo)
        n2 = mm(nn, nn)
        n3 = mm(nn, n2)
        tinv = mm(eye + nn + n2 + n3, t)

        hs = _mm3(_dot_nt, ar, s)
        akv = jnp.where(lo_lane,
                        mm(g0[:CHUNK], jnp.concatenate([zeros_c, v], axis=0)),
                        mm(g1[CHUNK:], jnp.concatenate([v, zeros_c], axis=0)))
        w = hs[:CHUNK] + akv
        tw = mm(tinv, jnp.concatenate([w, w], axis=0))
        u = jnp.where(lo_lane, tw[:CHUNK], tw[CHUNK:])
        uv = jnp.concatenate([u, v], axis=0)
        vu = jnp.concatenate([v, u], axis=0)
        y = hs[CHUNK:] + jnp.where(lo_lane, mm(g0[CHUNK:], uv), mm(g1[:CHUNK], vu))
        y_ref[sl, :] = y

        bkw = jnp.concatenate([kb * e_rest, k * e_rest], axis=0)
        upd = _mm3(_dot_tn, uv, bkw)
        s_ref[...] = s * jnp.exp(tot) + jnp.where(blockdiag, upd, 0.0)
        return carry

    lax.fori_loop(0, nchunk, chunk, 0)


def _rwkv_rec(r, lw, k, v, kk, a, batch, seq, tb):
    n, d = r.shape
    npair = d // LANES
    nt = seq // tb
    spec = pl.BlockSpec((tb, LANES), lambda b, p, t: (b * nt + t, p))
    return pl.pallas_call(
        functools.partial(_rec_kernel, nchunk=tb // CHUNK),
        grid=(batch, npair, nt),
        in_specs=[spec] * 6,
        out_specs=spec,
        out_shape=jax.ShapeDtypeStruct((n, d), F32),
        scratch_shapes=[pltpu.VMEM((LANES, LANES), F32)],
        compiler_params=pltpu.CompilerParams(
            dimension_semantics=("arbitrary", "arbitrary", "arbitrary")),
        name="rwkv_rec",
    )(r, lw, k, v, kk, a)


def _rwkv_post_kernel(y_ref, bonus_ref, g_ref, vec_ref, bd_ref, out_ref):
    bd = bd_ref[...]
    y = y_ref[...]
    inv_n = 1.0 / RWKV_HEAD
    yc = y - _dot_x(y, bd) * inv_n
    var = _dot_x(yc * yc, bd) * inv_n
    yn = yc * lax.rsqrt(var + GN_EPS) * vec_ref[6:7, :] + vec_ref[7:8, :]
    out_ref[...] = ((yn + bonus_ref[...]) * g_ref[...]).astype(out_ref.dtype)


def _rwkv_post(y, bonus, g, vecs, bd, tm):
    n, d = y.shape
    return pl.pallas_call(
        _rwkv_post_kernel,
        grid=(n // tm,),
        in_specs=[_row_spec(tm, d)] * 3 + [_const_spec(vecs.shape), _const_spec(bd.shape)],
        out_specs=_row_spec(tm, d),
        out_shape=jax.ShapeDtypeStruct((n, d), BF16),
        compiler_params=pltpu.CompilerParams(dimension_semantics=("arbitrary",)),
        name="rwkv_post",
    )(y, bonus, g, vecs, bd)


def _tail_kernel(x_ref, pre_ref, p_ref, gn_ref, wo_ref, w1_ref, w2_ref, wup_ref, wgate_ref, out_ref,
                 *, ff_chunk):
    x = x_ref[...] + _dot(pre_ref[...], wo_ref[...])
    hn = _rms(x, gn_ref[0:1, :]).astype(BF16)
    d_ff = w1_ref.shape[1]
    acc = x
    for c in range(0, d_ff, ff_chunk):
        mid = jnp.maximum(_dot(hn, w1_ref[:, c:c + ff_chunk]), 0.0)
        acc = acc + _dot(mid * mid, w2_ref[c:c + ff_chunk, :])
    gate = _sigmoid(_dot(_rms(acc, gn_ref[1:2, :]), wgate_ref[...]))
    out_ref[...] = acc + _dot(p_ref[...], wup_ref[...]) * gate


def _tail(x2d, pre, p2d, gn, wo, w1, w2, wup, wgate, tm):
    n, d = x2d.shape
    consts = (gn, wo, w1, w2, wup, wgate)
    return pl.pallas_call(
        functools.partial(_tail_kernel, ff_chunk=1024),
        grid=(n // tm,),
        in_specs=[_row_spec(tm, d), _row_spec(tm, d), _row_spec(tm, p2d.shape[1])]
                 + [_const_spec(c.shape) for c in consts],
        out_specs=_row_spec(tm, d),
        out_shape=jax.ShapeDtypeStruct((n, d), F32),
        compiler_params=pltpu.CompilerParams(dimension_semantics=("arbitrary",),
                                             vmem_limit_bytes=VMEM_LIMIT),
        name="tail",
    )(x2d, pre, p2d, *consts)


def _qkv_kernel(x_ref, gn_ref, wq_ref, wk_ref, wv_ref, bd_ref, q_out, k_out, v_out):
    x = x_ref[...]
    bd = bd_ref[...]
    inv_n = 1.0 / DIFF_HEAD

    def head_rms(t, g):
        ms = _dot_x(t * t, bd) * inv_n
        return t * lax.rsqrt(ms + NORM_EPS) * g

    hq = _rms(x, gn_ref[0:1, :])
    hk = _rms(x, gn_ref[1:2, :]).astype(BF16)
    q = head_rms(_dot(hq, wq_ref[...]), gn_ref[2:3, :])
    q_out[...] = q.astype(q_out.dtype)
    k_out[...] = head_rms(_dot(hk, wk_ref[...]), gn_ref[3:4, :]).astype(k_out.dtype)
    v_out[...] = _dot(hk, wv_ref[...]).astype(v_out.dtype)


def _qkv(x2d, gn, wq, wk, wv, bd, tm):
    n, d = x2d.shape
    consts = (gn, wq, wk, wv, bd)
    out_sd = jax.ShapeDtypeStruct((n, d), BF16)
    return pl.pallas_call(
        _qkv_kernel,
        grid=(n // tm,),
        in_specs=[_row_spec(tm, d)] + [_const_spec(c.shape) for c in consts],
        out_specs=[_row_spec(tm, d)] * 3,
        out_shape=[out_sd] * 3,
        compiler_params=pltpu.CompilerParams(dimension_semantics=("arbitrary",),
                                             vmem_limit_bytes=VMEM_LIMIT),
        name="qkv",
    )(x2d, *consts)


def _bucket_tiles(tb):
    i = np.arange(tb, dtype=np.int64)[:, None]
    j = np.arange(tb, dtype=np.int64)[None, :]

    def bucket(rel):
        n = np.maximum(rel, 0)
        max_exact = N_BUCKETS // 2
        nf = np.maximum(n, 1).astype(np.float32)
        large = max_exact + (np.log(nf / np.float32(max_exact)) / np.float32(math.log(MAX_DIST / max_exact))
                             * np.float32(N_BUCKETS - max_exact)).astype(np.int32)
        large = np.minimum(large, N_BUCKETS - 1)
        return np.where(n < max_exact, n, large).astype(np.int32)

    diag = np.where(i - j >= 0, bucket(i - j), -1)
    near = bucket(tb + i - j)
    return np.stack([diag, near]).astype(np.int32)


def _bias_kernel(tab_ref, bucket_ref, out_ref):
    h = pl.program_id(0)
    b = bucket_ref[...]
    far = tab_ref[N_BUCKETS - 1, h]
    acc = jnp.where(b < 0, NEG_BIG, 0.0)
    for n in range(N_BUCKETS - 1):
        acc = jnp.where(b == n, tab_ref[n, h] - far, acc)
    out_ref[0] = acc


def _bias_tiles(rel_bias, tb):
    buckets = jnp.asarray(_bucket_tiles(tb))
    nh = rel_bias.shape[1]
    return pl.pallas_call(
        _bias_kernel,
        grid=(nh,),
        in_specs=[pl.BlockSpec(memory_space=pltpu.SMEM),
                  pl.BlockSpec((2, tb, tb), lambda h: (0, 0, 0))],
        out_specs=pl.BlockSpec((1, 2, tb, tb), lambda h: (h, 0, 0, 0)),
        out_shape=jax.ShapeDtypeStruct((nh, 2, tb, tb), F32),
        compiler_params=pltpu.CompilerParams(dimension_semantics=("arbitrary",)),
        name="bias",
    )(rel_bias, buckets)


def _attn_kernel(q_ref, k_ref, v_ref, bias_ref, lam_ref, g_ref, o_ref, m_ref, l_ref, acc_ref,
                 *, tb, out_scale, lambda_init):
    qi = pl.program_id(2)
    lane = lax.broadcasted_iota(jnp.int32, (1, LANES), 1)
    q = q_ref[...]
    zero = jnp.zeros_like(q)
    qs = jnp.concatenate([jnp.where(lane < DIFF_HEAD, q, zero),
                          jnp.where(lane < DIFF_HEAD, zero, q)], axis=0)

    m_ref[...] = jnp.full_like(m_ref, NEG_BIG)
    l_ref[...] = jnp.zeros_like(l_ref)
    acc_ref[...] = jnp.zeros_like(acc_ref)

    def block(j, bias):
        rows = pl.ds(pl.multiple_of(j * tb, tb), tb)
        s = _dot_nt(qs, k_ref[rows, :])
        if bias is not None:
            s = s + jnp.concatenate([bias, bias], axis=0)
        m_old = m_ref[...]
        m_new = jnp.maximum(m_old, jnp.max(s, axis=1, keepdims=True))
        alpha = jnp.exp(m_old - m_new)
        p = jnp.exp(s - m_new)
        l_ref[...] = alpha * l_ref[...] + jnp.sum(p, axis=1, keepdims=True)
        acc_ref[...] = alpha * acc_ref[...] + _dot(p, v_ref[rows, :])
        m_ref[...] = m_new

    def far(j, carry):
        block(j, None)
        return carry

    lax.fori_loop(0, jnp.maximum(qi - 1, 0), far, 0)

    @pl.when(qi >= 1)
    def _():
        block(qi - 1, bias_ref[0, 1])

    block(qi, bias_ref[0, 0])

    lam = lam_ref[...]
    lam_full = (jnp.exp(jnp.sum(lam[0:1] * lam[1:2], axis=1, keepdims=True))
                - jnp.exp(jnp.sum(lam[2:3] * lam[3:4], axis=1, keepdims=True)) + lambda_init)
    o = acc_ref[...] / l_ref[...]
    o = o[:tb] - lam_full * o[tb:]
    o = _rms(o, g_ref[...]) * out_scale
    o_ref[...] = o.astype(o_ref.dtype)


def _attn(q, k, v, bias, lam, subln_g, batch, seq, tb, layer_idx):
    n, d = q.shape
    nh = d // LANES
    nq = seq // tb
    lambda_init = 0.8 - 0.6 * math.exp(-0.3 * layer_idx)
    kv_spec = pl.BlockSpec((seq, LANES), lambda b, h, i: (b, h))
    q_spec = pl.BlockSpec((tb, LANES), lambda b, h, i: (b * nq + i, h))
    return pl.pallas_call(
        functools.partial(_attn_kernel, tb=tb, out_scale=1.0 - lambda_init, lambda_init=lambda_init),
        grid=(batch, nh, nq),
        in_specs=[q_spec, kv_spec, kv_spec,
                  pl.BlockSpec((1, 2, tb, tb), lambda b, h, i: (h, 0, 0, 0)),
                  pl.BlockSpec(lam.shape, lambda b, h, i: (0, 0)),
                  pl.BlockSpec(subln_g.shape, lambda b, h, i: (0, 0))],
        out_specs=q_spec,
        out_shape=jax.ShapeDtypeStruct((n, d), BF16),
        scratch_shapes=[pltpu.VMEM((2 * tb, 1), F32), pltpu.VMEM((2 * tb, 1), F32),
                        pltpu.VMEM((2 * tb, LANES), F32)],
        compiler_params=pltpu.CompilerParams(
            dimension_semantics=("arbitrary", "arbitrary", "arbitrary"),
            vmem_limit_bytes=VMEM_LIMIT),
        name="attn",
    )(q, k, v, bias, lam, subln_g)


def kernel(x, p, norm_g, mlp_w1, mlp_w2, ple_w_up, ple_w_gate, rwkv_mix, rwkv_w_rkvo, rwkv_w0, rwkv_w1, rwkv_w2, rwkv_a0, rwkv_a1, rwkv_a2, rwkv_g1, rwkv_g2, rwkv_k_k, rwkv_k_a, rwkv_r_k, rwkv_ln_w, rwkv_ln_b, kv_norm_g, w_k_shared, w_v_shared, k_norm_g, diff_w_q, diff_q_norm_g, diff_lam, diff_subln_g, diff_w_o, rel_bias):
    batch, seq, d = x.shape
    assert d == D_MODEL and norm_g.shape[0] == 2
    n = batch * seq
    tm = min(256, seq)
    tb_rec = min(256, seq)
    tb_attn = min(256, seq)
    assert seq % tm == 0 and seq % tb_rec == 0 and seq % tb_attn == 0 and tb_attn >= MAX_DIST

    bf = lambda w: w.astype(BF16)
    x2d = x.reshape(n, d)
    p2d = p.reshape(p.shape[0], n, p.shape[-1])
    head_id = jnp.arange(d, dtype=jnp.int32) // RWKV_HEAD
    bd = (head_id[:, None] == head_id[None, :]).astype(BF16)

    vecs = jnp.stack([norm_g[0, 0], rwkv_w0[0], rwkv_a0[0], rwkv_k_k[0], rwkv_k_a[0],
                      rwkv_r_k[0].reshape(d), rwkv_ln_w[0], rwkv_ln_b[0]])
    r, lw, k, v, kk, a, g, bonus = _rwkv_pre(
        x2d, vecs, rwkv_mix[0], bf(rwkv_w_rkvo[0, 0]), bf(rwkv_w_rkvo[0, 1]), bf(rwkv_w_rkvo[0, 2]),
        bf(rwkv_w1[0]), bf(rwkv_w2[0]), bf(rwkv_a1[0]), bf(rwkv_a2[0]), bf(rwkv_g1[0]), bf(rwkv_g2[0]),
        bd, seq, tm)
    y = _rwkv_rec(r, lw, k, v, kk, a, batch, seq, tb_rec)
    pre = _rwkv_post(y, bonus, g, vecs, bd, tm)
    x2d = _tail(x2d, pre, p2d[0], norm_g[0, 1:3], bf(rwkv_w_rkvo[0, 3]), bf(mlp_w1[0]), bf(mlp_w2[0]),
                bf(ple_w_up[0]), bf(ple_w_gate[0]), tm)

    reps = d // DIFF_HEAD
    gn = jnp.stack([norm_g[1, 0], kv_norm_g,
                    jnp.tile(diff_q_norm_g[0], reps) * (DIFF_HEAD ** -0.5), jnp.tile(k_norm_g, reps)])
    q, kq, vq = _qkv(x2d, gn, bf(diff_w_q[0]), bf(w_k_shared), bf(w_v_shared), bd, tm)
    bias = _bias_tiles(rel_bias, tb_attn)
    o = _attn(q, kq, vq, bias, diff_lam[0], diff_subln_g[0].reshape(1, LANES), batch, seq, tb_attn, 1)
    x2d = _tail(x2d, o, p2d[1], norm_g[1, 1:3], bf(diff_w_o[0]), bf(mlp_w1[1]), bf(mlp_w2[1]),
                bf(ple_w_up[1]), bf(ple_w_gate[1]), tm)
    return x2d.reshape(batch, seq, d)
```

```python
import functools
import math

import numpy as np
import jax
import jax.numpy as jnp
from jax import lax
from jax.experimental import pallas as pl
from jax.experimental.pallas import tpu as pltpu

F32 = jnp.float32
BF16 = jnp.bfloat16

D_MODEL = 1024
RWKV_HEAD = 64
DIFF_HEAD = 64
DIFF_HEADS = 8
LANES = 128
CHUNK = 64
SUB = 16
NORM_EPS = 1e-6
GN_EPS = 64e-5
N_BUCKETS = 32
MAX_DIST = 128
NEG_BIG = -1e30
VMEM_LIMIT = 56 * 1024 * 1024


def _dot(a, b):
    return jnp.dot(a.astype(BF16), b.astype(BF16), preferred_element_type=F32)


def _dot_nt(a, b):
    return lax.dot_general(a.astype(BF16), b.astype(BF16), (((1,), (1,)), ((), ())),
                           preferred_element_type=F32)


def _dot_tn(a, b):
    return lax.dot_general(a.astype(BF16), b.astype(BF16), (((0,), (0,)), ((), ())),
                           preferred_element_type=F32)


def _split(x):
    hi = x.astype(BF16)
    lo = (x - hi.astype(F32)).astype(BF16)
    return hi, lo


def _dot_x(a, b_exact):
    hi, lo = _split(a)
    return _dot(hi, b_exact) + _dot(lo, b_exact)


def _mm3(dotfn, a, b):
    ah, al = _split(a)
    bh, bl = _split(b)
    return dotfn(ah, bh) + (dotfn(ah, bl) + dotfn(al, bh))


def _rms(x, g):
    return x * lax.rsqrt(jnp.mean(x * x, axis=-1, keepdims=True) + NORM_EPS) * g


def _sigmoid(x):
    return 1.0 / (1.0 + jnp.exp(-x))


def _const_spec(shape):
    nd = len(shape)
    return pl.BlockSpec(shape, lambda *_: (0,) * nd, pipeline_mode=pl.Buffered(1))


def _row_spec(tm, width):
    return pl.BlockSpec((tm, width), lambda i: (i, 0))


def _rwkv_pre_kernel(x_ref, xp_ref, vec_ref, mix_ref, wr_ref, wk_ref, wv_ref,
                     w1_ref, w2_ref, a1_ref, a2_ref, g1_ref, g2_ref, bd_ref,
                     r_out, lw_out, k_out, v_out, kk_out, a_out, g_out, bonus_out,
                     *, blocks_per_seq):
    i = pl.program_id(0)
    g0 = vec_ref[0:1, :]
    h = _rms(x_ref[...], g0)
    prev = _rms(xp_ref[...], g0)[7:8, :]
    prev = jnp.where(i % blocks_per_seq == 0, 0.0, prev)
    row = lax.broadcasted_iota(jnp.int32, h.shape, 0)
    hprev = jnp.where(row == 0, prev, pltpu.roll(h, 1, 0))
    dx = hprev - h

    def mixed(j):
        return (h + dx * mix_ref[j:j + 1, :]).astype(BF16)

    r = _dot(mixed(0), wr_ref[...])
    k = _dot(mixed(2), wk_ref[...])
    v = _dot(mixed(3), wv_ref[...])
    w_lora = _dot(jnp.tanh(_dot(mixed(1), w1_ref[...])), w2_ref[...])
    a_lora = _dot(_dot(mixed(4), a1_ref[...]), a2_ref[...])
    g = _dot(_sigmoid(_dot(mixed(5), g1_ref[...])), g2_ref[...])

    z = -(vec_ref[1:2, :] + w_lora)
    softplus = jnp.maximum(z, 0.0) + jnp.log1p(jnp.exp(-jnp.abs(z)))
    lw = -jnp.exp(-softplus - 0.5)
    a = _sigmoid(vec_ref[2:3, :] + a_lora)

    bd = bd_ref[...]
    kk = k * vec_ref[3:4, :]
    ss = _dot_x(kk * kk, bd)
    kk = kk / jnp.maximum(jnp.sqrt(ss), 1e-12)
    k = k * (1.0 + (a - 1.0) * vec_ref[4:5, :])
    bonus = _dot_x(r * k * vec_ref[5:6, :], bd) * v

    r_out[...] = r
    lw_out[...] = lw
    k_out[...] = k
    v_out[...] = v
    kk_out[...] = kk
    a_out[...] = a
    g_out[...] = g
    bonus_out[...] = bonus


def _rwkv_pre(x2d, vecs, mix, wr, wk, wv, w1, w2, a1, a2, g1, g2, bd, seq, tm):
    n, d = x2d.shape
    blocks_per_seq = seq // tm
    consts = (vecs, mix, wr, wk, wv, w1, w2, a1, a2, g1, g2, bd)
    out_sd = jax.ShapeDtypeStruct((n, d), F32)
    return pl.pallas_call(
        functools.partial(_rwkv_pre_kernel, blocks_per_seq=blocks_per_seq),
        grid=(n // tm,),
        in_specs=[_row_spec(tm, d),
                  pl.BlockSpec((8, d), lambda i: (jnp.maximum(i * (tm // 8) - 1, 0), 0))]
                 + [_const_spec(c.shape) for c in consts],
        out_specs=[_row_spec(tm, d)] * 8,
        out_shape=[out_sd] * 8,
        compiler_params=pltpu.CompilerParams(dimension_semantics=("arbitrary",),
                                             vmem_limit_bytes=VMEM_LIMIT),
        name="rwkv_pre",
    )(x2d, x2d, *consts)


def _rec_kernel(r_ref, lw_ref, k_ref, v_ref, kk_ref, a_ref, y_ref, s_ref, *, nchunk, npair):
    c2 = 2 * CHUNK

    @pl.when(pl.program_id(2) == 0)
    def _():
        s_ref[...] = jnp.zeros_like(s_ref)

    ri = lax.broadcasted_iota(jnp.int32, (c2, LANES), 0)
    ci = lax.broadcasted_iota(jnp.int32, (c2, LANES), 1)
    top = ri < CHUNK
    left = ci < CHUNK
    ti = ri % CHUNK
    tj = ci % CHUNK
    mask0 = tj < ti + jnp.where(top, 0, 1)
    mask1 = tj < ti + jnp.where(top, 1, 0)
    blockdiag = (ri // CHUNK) == (ci // CHUNK)
    sub_diag = (ri // SUB) == (ci // SUB)
    eye = (ri == ci).astype(F32)
    lo_lane = lax.broadcasted_iota(jnp.int32, (1, LANES), 1) < CHUNK
    tri = (lax.broadcasted_iota(jnp.int32, (CHUNK, CHUNK), 1)
           <= lax.broadcasted_iota(jnp.int32, (CHUNK, CHUNK), 0)).astype(BF16)
    zeros_c = jnp.zeros((CHUNK, LANES), F32)

    def pair_chunk(sl, pr):
        ln = slice(pr * LANES, (pr + 1) * LANES)
        r = r_ref[sl, ln]
        lw = lw_ref[sl, ln]
        k = k_ref[sl, ln]
        v = v_ref[sl, ln]
        kk = kk_ref[sl, ln]
        a = a_ref[sl, ln]
        s = s_ref[pr]

        l_hi = lw.astype(BF16)
        l_mid = (lw - l_hi.astype(F32)).astype(BF16)
        l_lo = (lw - l_hi.astype(F32) - l_mid.astype(F32)).astype(BF16)
        cum = _dot(tri, l_hi) + _dot(tri, l_mid) + _dot(tri, l_lo)
        tot = cum[CHUNK - 1:CHUNK, :]
        e_neg = jnp.exp(-cum)
        e_rest = jnp.exp(tot - cum)
        rt = r * jnp.exp(cum)
        at = -kk * jnp.exp(cum - lw)
        kb = kk * a
        bt = kb * e_neg
        kt = k * e_neg

        ar = jnp.concatenate([at, rt], axis=0)
        ra = jnp.concatenate([rt, at], axis=0)
        bk = jnp.concatenate([bt, kt], axis=0)
        kbm = jnp.concatenate([kt, bt], axis=0)
        g0 = jnp.where(mask0, _dot_nt(jnp.where(lo_lane, ar, 0.0), bk), 0.0)
        g1 = jnp.where(mask1, _dot_nt(jnp.where(lo_lane, 0.0, ra), kbm), 0.0)

        abd = jnp.where(top, jnp.where(left, g0, 0.0), jnp.where(left, 0.0, g1))
        dg = jnp.where(sub_diag, abd, 0.0)
        lo = abd - dg
        x = dg
        t = eye + dg
        for _ in range(3):
            x = _dot(x, x)
            t = t + _dot(t, x)
        nn = _dot(t, lo)
        n2 = _dot(nn, nn)
        n3 = _dot(nn, n2)
        tinv = _dot(eye + nn + n2 + n3, t)

        hs = _dot_nt(ar, s)
        akv = jnp.where(lo_lane,
                        _dot(g0[:CHUNK], jnp.concatenate([zeros_c, v], axis=0)),
                        _dot(g1[CHUNK:], jnp.concatenate([v, zeros_c], axis=0)))
        w = hs[:CHUNK] + akv
        tw = _dot(tinv, jnp.concatenate([w, w], axis=0))
        u = jnp.where(lo_lane, tw[:CHUNK], tw[CHUNK:])
        uv = jnp.concatenate([u, v], axis=0)
        vu = jnp.concatenate([v, u], axis=0)
        y_ref[sl, ln] = hs[CHUNK:] + jnp.where(lo_lane, _dot(g0[CHUNK:], uv), _dot(g1[:CHUNK], vu))

        bkw = jnp.concatenate([kb * e_rest, k * e_rest], axis=0)
        s_ref[pr] = s * jnp.exp(tot) + jnp.where(blockdiag, _dot_tn(uv, bkw), 0.0)

    def chunk(c, carry):
        sl = pl.ds(pl.multiple_of(c * CHUNK, CHUNK), CHUNK)
        for pr in range(npair):
            pair_chunk(sl, pr)
        return carry

    lax.fori_loop(0, nchunk, chunk, 0)


def _rwkv_rec(r, lw, k, v, kk, a, batch, seq, tb, pairs_per_block):
    n, d = r.shape
    width = pairs_per_block * LANES
    nt = seq // tb
    spec = pl.BlockSpec((tb, width), lambda b, p, t: (b * nt + t, p))
    return pl.pallas_call(
        functools.partial(_rec_kernel, nchunk=tb // CHUNK, npair=pairs_per_block),
        grid=(batch, d // width, nt),
        in_specs=[spec] * 6,
        out_specs=spec,
        out_shape=jax.ShapeDtypeStruct((n, d), F32),
        scratch_shapes=[pltpu.VMEM((pairs_per_block, LANES, LANES), F32)],
        compiler_params=pltpu.CompilerParams(
            dimension_semantics=("arbitrary", "arbitrary", "arbitrary"),
            vmem_limit_bytes=VMEM_LIMIT),
        name="rwkv_rec",
    )(r, lw, k, v, kk, a)


def _rwkv_post_kernel(y_ref, bonus_ref, g_ref, vec_ref, bd_ref, out_ref):
    bd = bd_ref[...]
    y = y_ref[...]
    inv_n = 1.0 / RWKV_HEAD
    yc = y - _dot_x(y, bd) * inv_n
    var = _dot_x(yc * yc, bd) * inv_n
    yn = yc * lax.rsqrt(var + GN_EPS) * vec_ref[6:7, :] + vec_ref[7:8, :]
    out_ref[...] = ((yn + bonus_ref[...]) * g_ref[...]).astype(out_ref.dtype)


def _rwkv_post(y, bonus, g, vecs, bd, tm):
    n, d = y.shape
    return pl.pallas_call(
        _rwkv_post_kernel,
        grid=(n // tm,),
        in_specs=[_row_spec(tm, d)] * 3 + [_const_spec(vecs.shape), _const_spec(bd.shape)],
        out_specs=_row_spec(tm, d),
        out_shape=jax.ShapeDtypeStruct((n, d), BF16),
        compiler_params=pltpu.CompilerParams(dimension_semantics=("arbitrary",)),
        name="rwkv_post",
    )(y, bonus, g, vecs, bd)


def _tail_kernel(x_ref, pre_ref, p_ref, gn_ref, wo_ref, w1_ref, w2_ref, wup_ref, wgate_ref, out_ref,
                 *, ff_chunk):
    x = x_ref[...] + _dot(pre_ref[...], wo_ref[...])
    hn = _rms(x, gn_ref[0:1, :]).astype(BF16)
    d_ff = w1_ref.shape[1]
    acc = x
    for c in range(0, d_ff, ff_chunk):
        mid = jnp.maximum(_dot(hn, w1_ref[:, c:c + ff_chunk]), 0.0)
        acc = acc + _dot(mid * mid, w2_ref[c:c + ff_chunk, :])
    gate = _sigmoid(_dot(_rms(acc, gn_ref[1:2, :]), wgate_ref[...]))
    out_ref[...] = acc + _dot(p_ref[...], wup_ref[...]) * gate


def _tail(x2d, pre, p2d, gn, wo, w1, w2, wup, wgate, tm):
    n, d = x2d.shape
    consts = (gn, wo, w1, w2, wup, wgate)
    return pl.pallas_call(
        functools.partial(_tail_kernel, ff_chunk=1024),
        grid=(n // tm,),
        in_specs=[_row_spec(tm, d), _row_spec(tm, d), _row_spec(tm, p2d.shape[1])]
                 + [_const_spec(c.shape) for c in consts],
        out_specs=_row_spec(tm, d),
        out_shape=jax.ShapeDtypeStruct((n, d), F32),
        compiler_params=pltpu.CompilerParams(dimension_semantics=("arbitrary",),
                                             vmem_limit_bytes=VMEM_LIMIT),
        name="tail",
    )(x2d, pre, p2d, *consts)


def _qkv_kernel(x_ref, gn_ref, wq_ref, wk_ref, wv_ref, bd_ref, q_out, k_out, v_out):
    x = x_ref[...]
    bd = bd_ref[...]
    inv_n = 1.0 / DIFF_HEAD

    def head_rms(t, g):
        ms = _dot_x(t * t, bd) * inv_n
        return t * lax.rsqrt(ms + NORM_EPS) * g

    hq = _rms(x, gn_ref[0:1, :])
    hk = _rms(x, gn_ref[1:2, :]).astype(BF16)
    q = head_rms(_dot(hq, wq_ref[...]), gn_ref[2:3, :])
    q_out[...] = q.astype(q_out.dtype)
    k_out[...] = head_rms(_dot(hk, wk_ref[...]), gn_ref[3:4, :]).astype(k_out.dtype)
    v_out[0] = _dot(hk, wv_ref[...]).T.astype(v_out.dtype)


def _qkv(x2d, gn, wq, wk, wv, bd, tm):
    n, d = x2d.shape
    consts = (gn, wq, wk, wv, bd)
    out_sd = jax.ShapeDtypeStruct((n, d), BF16)
    return pl.pallas_call(
        _qkv_kernel,
        grid=(n // tm,),
        in_specs=[_row_spec(tm, d)] + [_const_spec(c.shape) for c in consts],
        out_specs=[_row_spec(tm, d)] * 2 + [pl.BlockSpec((1, d, tm), lambda i: (i, 0, 0))],
        out_shape=[out_sd] * 2 + [jax.ShapeDtypeStruct((n // tm, d, tm), BF16)],
        compiler_params=pltpu.CompilerParams(dimension_semantics=("arbitrary",),
                                             vmem_limit_bytes=VMEM_LIMIT),
        name="qkv",
    )(x2d, *consts)


def _bucket_tiles(tb):
    i = np.arange(tb, dtype=np.int64)[None, :]
    j = np.arange(tb, dtype=np.int64)[:, None]

    def bucket(rel):
        n = np.maximum(rel, 0)
        max_exact = N_BUCKETS // 2
        nf = np.maximum(n, 1).astype(np.float32)
        large = max_exact + (np.log(nf / np.float32(max_exact)) / np.float32(math.log(MAX_DIST / max_exact))
                             * np.float32(N_BUCKETS - max_exact)).astype(np.int32)
        large = np.minimum(large, N_BUCKETS - 1)
        return np.where(n < max_exact, n, large).astype(np.int32)

    diag = np.where(i - j >= 0, bucket(i - j), -1)
    near = bucket(tb + i - j)
    tiles = np.stack([diag, near]).astype(np.int32)
    return np.concatenate([tiles, tiles], axis=2)


def _bias_kernel(tab_ref, bucket_ref, out_ref):
    h = pl.program_id(0)
    b = bucket_ref[...]
    far = tab_ref[N_BUCKETS - 1, h]
    acc = jnp.where(b < 0, NEG_BIG, 0.0)
    for n in range(N_BUCKETS - 1):
        acc = jnp.where(b == n, tab_ref[n, h] - far, acc)
    out_ref[0] = acc


def _bias_tiles(rel_bias, tb):
    buckets = jnp.asarray(_bucket_tiles(tb))
    nh = rel_bias.shape[1]
    return pl.pallas_call(
        _bias_kernel,
        grid=(nh,),
        in_specs=[pl.BlockSpec(memory_space=pltpu.SMEM),
                  pl.BlockSpec((2, tb, 2 * tb), lambda h: (0, 0, 0))],
        out_specs=pl.BlockSpec((1, 2, tb, 2 * tb), lambda h: (h, 0, 0, 0)),
        out_shape=jax.ShapeDtypeStruct((nh, 2, tb, 2 * tb), F32),
        compiler_params=pltpu.CompilerParams(dimension_semantics=("arbitrary",)),
        name="bias",
    )(rel_bias, buckets)


def _attn_kernel(q_ref, k_ref, vt_ref, bias_ref, lam_ref, g_ref, o_ref, m_ref, l_ref, acc_ref,
                 *, tb, out_scale, lambda_init):
    qi = pl.program_id(2)
    lane = lax.broadcasted_iota(jnp.int32, (1, LANES), 1)
    q = q_ref[...]
    zero = jnp.zeros_like(q)
    qs = jnp.concatenate([jnp.where(lane < DIFF_HEAD, q, zero),
                          jnp.where(lane < DIFF_HEAD, zero, q)], axis=0)

    m_ref[...] = jnp.full_like(m_ref, NEG_BIG)
    l_ref[...] = jnp.zeros_like(l_ref)
    acc_ref[...] = jnp.zeros_like(acc_ref)

    def update(st, pv):
        m_old = m_ref[...]
        m_new = jnp.maximum(m_old, jnp.max(st, axis=0, keepdims=True))
        alpha = jnp.exp(m_old - m_new)
        p = jnp.exp(st - m_new)
        l_ref[...] = alpha * l_ref[...] + jnp.sum(p, axis=0, keepdims=True)
        acc_ref[...] = alpha * acc_ref[...] + pv(p.astype(BF16))
        m_ref[...] = m_new

    def one_block(j, bias):
        st = _dot_nt(k_ref[pl.ds(pl.multiple_of(j * tb, tb), tb), :], qs)
        if bias is not None:
            st = st + bias
        update(st, lambda p: _dot(vt_ref[j], p))

    def far_pair(t, carry):
        j = 2 * t
        st = _dot_nt(k_ref[pl.ds(pl.multiple_of(j * tb, 2 * tb), 2 * tb), :], qs)
        update(st, lambda p: _dot(vt_ref[j], p[:tb]) + _dot(vt_ref[j + 1], p[tb:]))
        return carry

    nfar = jnp.maximum(qi - 1, 0)
    lax.fori_loop(0, nfar // 2, far_pair, 0)

    @pl.when(nfar % 2 == 1)
    def _():
        one_block(nfar - 1, None)

    @pl.when(qi >= 1)
    def _():
        one_block(qi - 1, bias_ref[0, 1])

    one_block(qi, bias_ref[0, 0])

    lam = lam_ref[...]
    lam_full = (jnp.exp(jnp.sum(lam[0:1] * lam[1:2], axis=1, keepdims=True))
                - jnp.exp(jnp.sum(lam[2:3] * lam[3:4], axis=1, keepdims=True)) + lambda_init)
    o = acc_ref[...] * (1.0 / l_ref[...])
    o = o[:, :tb] - lam_full * o[:, tb:]
    o = o * lax.rsqrt(jnp.mean(o * o, axis=0, keepdims=True) + NORM_EPS)
    o_ref[...] = (o.T * (g_ref[...] * out_scale)).astype(o_ref.dtype)


def _attn(q, k, vt, bias, lam, subln_g, batch, seq, tb, layer_idx):
    n, d = q.shape
    nh = d // LANES
    nq = seq // tb
    lambda_init = 0.8 - 0.6 * math.exp(-0.3 * layer_idx)
    q_spec = pl.BlockSpec((tb, LANES), lambda b, h, i: (b * nq + i, h))
    return pl.pallas_call(
        functools.partial(_attn_kernel, tb=tb, out_scale=1.0 - lambda_init, lambda_init=lambda_init),
        grid=(batch, nh, nq),
        in_specs=[q_spec,
                  pl.BlockSpec((seq, LANES), lambda b, h, i: (b, h)),
                  pl.BlockSpec((nq, LANES, tb), lambda b, h, i: (b, h, 0)),
                  pl.BlockSpec((1, 2, tb, 2 * tb), lambda b, h, i: (h, 0, 0, 0)),
                  pl.BlockSpec(lam.shape, lambda b, h, i: (0, 0)),
                  pl.BlockSpec(subln_g.shape, lambda b, h, i: (0, 0))],
        out_specs=q_spec,
        out_shape=jax.ShapeDtypeStruct((n, d), BF16),
        scratch_shapes=[pltpu.VMEM((1, 2 * tb), F32), pltpu.VMEM((1, 2 * tb), F32),
                        pltpu.VMEM((LANES, 2 * tb), F32)],
        compiler_params=pltpu.CompilerParams(
            dimension_semantics=("arbitrary", "arbitrary", "arbitrary"),
            vmem_limit_bytes=VMEM_LIMIT),
        name="attn",
    )(q, k, vt, bias, lam, subln_g)


def kernel(x, p, norm_g, mlp_w1, mlp_w2, ple_w_up, ple_w_gate, rwkv_mix, rwkv_w_rkvo, rwkv_w0, rwkv_w1, rwkv_w2, rwkv_a0, rwkv_a1, rwkv_a2, rwkv_g1, rwkv_g2, rwkv_k_k, rwkv_k_a, rwkv_r_k, rwkv_ln_w, rwkv_ln_b, kv_norm_g, w_k_shared, w_v_shared, k_norm_g, diff_w_q, diff_q_norm_g, diff_lam, diff_subln_g, diff_w_o, rel_bias):
    batch, seq, d = x.shape
    assert d == D_MODEL and norm_g.shape[0] == 2
    n = batch * seq
    tm = min(256, seq)
    tb_rec = min(256, seq)
    tb_attn = min(256, seq)
    assert seq % tm == 0 and seq % tb_rec == 0 and seq % tb_attn == 0 and tb_attn >= MAX_DIST

    bf = lambda w: w.astype(BF16)
    x2d = x.reshape(n, d)
    p2d = p.reshape(p.shape[0], n, p.shape[-1])
    head_id = jnp.arange(d, dtype=jnp.int32) // RWKV_HEAD
    bd = (head_id[:, None] == head_id[None, :]).astype(BF16)

    vecs = jnp.stack([norm_g[0, 0], rwkv_w0[0], rwkv_a0[0], rwkv_k_k[0], rwkv_k_a[0],
                      rwkv_r_k[0].reshape(d), rwkv_ln_w[0], rwkv_ln_b[0]])
    r, lw, k, v, kk, a, g, bonus = _rwkv_pre(
        x2d, vecs, rwkv_mix[0], bf(rwkv_w_rkvo[0, 0]), bf(rwkv_w_rkvo[0, 1]), bf(rwkv_w_rkvo[0, 2]),
        bf(rwkv_w1[0]), bf(rwkv_w2[0]), bf(rwkv_a1[0]), bf(rwkv_a2[0]), bf(rwkv_g1[0]), bf(rwkv_g2[0]),
        bd, seq, tm)
    y = _rwkv_rec(r, lw, k, v, kk, a, batch, seq, tb_rec, d // LANES)
    pre = _rwkv_post(y, bonus, g, vecs, bd, tm)
    x2d = _tail(x2d, pre, p2d[0], norm_g[0, 1:3], bf(rwkv_w_rkvo[0, 3]), bf(mlp_w1[0]), bf(mlp_w2[0]),
                bf(ple_w_up[0]), bf(ple_w_gate[0]), tm)

    reps = d // DIFF_HEAD
    gn = jnp.stack([norm_g[1, 0], kv_norm_g,
                    jnp.tile(diff_q_norm_g[0], reps) * (DIFF_HEAD ** -0.5), jnp.tile(k_norm_g, reps)])
    q, kq, vq = _qkv(x2d, gn, bf(diff_w_q[0]), bf(w_k_shared), bf(w_v_shared), bd, tb_attn)
    bias = _bias_tiles(rel_bias, tb_attn)
    o = _attn(q, kq, vq, bias, diff_lam[0], diff_subln_g[0].reshape(1, LANES), batch, seq, tb_attn, 1)
    x2d = _tail(x2d, o, p2d[1], norm_g[1, 1:3], bf(diff_w_o[0]), bf(mlp_w1[1]), bf(mlp_w2[1]),
                bf(ple_w_up[1]), bf(ple_w_gate[1]), tm)
    return x2d.reshape(batch, seq, d)
```

```python
import functools
import math

import numpy as np
import jax
import jax.numpy as jnp
from jax import lax
from jax.experimental import pallas as pl
from jax.experimental.pallas import tpu as pltpu

F32 = jnp.float32
BF16 = jnp.bfloat16

D_MODEL = 1024
RWKV_HEAD = 64
DIFF_HEAD = 64
DIFF_HEADS = 8
LANES = 128
CHUNK = 64
SUB = 16
NORM_EPS = 1e-6
GN_EPS = 64e-5
N_BUCKETS = 32
MAX_DIST = 128
NEG_BIG = -1e30
LOG2E = math.log2(math.e)
VMEM_LIMIT = 56 * 1024 * 1024


def _dot(a, b):
    return jnp.dot(a.astype(BF16), b.astype(BF16), preferred_element_type=F32)


def _dot_nt(a, b):
    return lax.dot_general(a.astype(BF16), b.astype(BF16), (((1,), (1,)), ((), ())),
                           preferred_element_type=F32)


def _dot_tn(a, b):
    return lax.dot_general(a.astype(BF16), b.astype(BF16), (((0,), (0,)), ((), ())),
                           preferred_element_type=F32)


def _split(x):
    hi = x.astype(BF16)
    lo = (x - hi.astype(F32)).astype(BF16)
    return hi, lo


def _dot_x(a, b_exact):
    hi, lo = _split(a)
    return _dot(hi, b_exact) + _dot(lo, b_exact)


def _mm3(dotfn, a, b):
    ah, al = _split(a)
    bh, bl = _split(b)
    return dotfn(ah, bh) + (dotfn(ah, bl) + dotfn(al, bh))


def _rms(x, g):
    return x * lax.rsqrt(jnp.mean(x * x, axis=-1, keepdims=True) + NORM_EPS) * g


def _sigmoid(x):
    return 1.0 / (1.0 + jnp.exp(-x))


def _const_spec(shape):
    nd = len(shape)
    return pl.BlockSpec(shape, lambda *_: (0,) * nd, pipeline_mode=pl.Buffered(1))


def _row_spec(tm, width):
    return pl.BlockSpec((tm, width), lambda i: (i, 0))


def _rwkv_pre_kernel(x_ref, xp_ref, vec_ref, mix_ref, wr_ref, wk_ref, wv_ref,
                     w1_ref, w2_ref, a1_ref, a2_ref, g1_ref, g2_ref, bd_ref,
                     r_out, lw_out, k_out, v_out, kk_out, a_out, g_out, bonus_out,
                     *, blocks_per_seq):
    i = pl.program_id(0)
    g0 = vec_ref[0:1, :]
    h = _rms(x_ref[...], g0)
    prev = _rms(xp_ref[...], g0)[7:8, :]
    prev = jnp.where(i % blocks_per_seq == 0, 0.0, prev)
    row = lax.broadcasted_iota(jnp.int32, h.shape, 0)
    hprev = jnp.where(row == 0, prev, pltpu.roll(h, 1, 0))
    dx = hprev - h

    def mixed(j):
        return (h + dx * mix_ref[j:j + 1, :]).astype(BF16)

    r = _dot(mixed(0), wr_ref[...])
    k = _dot(mixed(2), wk_ref[...])
    v = _dot(mixed(3), wv_ref[...])
    w_lora = _dot(jnp.tanh(_dot(mixed(1), w1_ref[...])), w2_ref[...])
    a_lora = _dot(_dot(mixed(4), a1_ref[...]), a2_ref[...])
    g = _dot(_sigmoid(_dot(mixed(5), g1_ref[...])), g2_ref[...])

    z = -(vec_ref[1:2, :] + w_lora)
    softplus = jnp.maximum(z, 0.0) + jnp.log1p(jnp.exp(-jnp.abs(z)))
    lw = -jnp.exp(-softplus - 0.5)
    a = _sigmoid(vec_ref[2:3, :] + a_lora)

    bd = bd_ref[...]
    kk = k * vec_ref[3:4, :]
    ss = _dot_x(kk * kk, bd)
    kk = kk / jnp.maximum(jnp.sqrt(ss), 1e-12)
    k = k * (1.0 + (a - 1.0) * vec_ref[4:5, :])
    bonus = _dot_x(r * k * vec_ref[5:6, :], bd) * v

    r_out[...] = r
    lw_out[...] = lw
    k_out[...] = k
    v_out[...] = v
    kk_out[...] = kk
    a_out[...] = a
    g_out[...] = g
    bonus_out[...] = bonus


def _rwkv_pre(x2d, vecs, mix, wr, wk, wv, w1, w2, a1, a2, g1, g2, bd, seq, tm):
    n, d = x2d.shape
    blocks_per_seq = seq // tm
    consts = (vecs, mix, wr, wk, wv, w1, w2, a1, a2, g1, g2, bd)
    out_sd = jax.ShapeDtypeStruct((n, d), F32)
    return pl.pallas_call(
        functools.partial(_rwkv_pre_kernel, blocks_per_seq=blocks_per_seq),
        grid=(n // tm,),
        in_specs=[_row_spec(tm, d),
                  pl.BlockSpec((8, d), lambda i: (jnp.maximum(i * (tm // 8) - 1, 0), 0))]
                 + [_const_spec(c.shape) for c in consts],
        out_specs=[_row_spec(tm, d)] * 8,
        out_shape=[out_sd] * 8,
        compiler_params=pltpu.CompilerParams(dimension_semantics=("arbitrary",),
                                             vmem_limit_bytes=VMEM_LIMIT),
        name="rwkv_pre",
    )(x2d, x2d, *consts)


def _rec_kernel(r_ref, lw_ref, k_ref, v_ref, kk_ref, a_ref, y_ref, s_ref, *, nchunk, npair):
    c2 = 2 * CHUNK
    width = npair * LANES

    @pl.when(pl.program_id(2) == 0)
    def _():
        s_ref[...] = jnp.zeros_like(s_ref)

    ri = lax.broadcasted_iota(jnp.int32, (c2, c2), 0)
    ci = lax.broadcasted_iota(jnp.int32, (c2, c2), 1)
    sub_diag = (ri // SUB) == (ci // SUB)
    eye = (ri == ci).astype(F32)
    gi = lax.broadcasted_iota(jnp.int32, (2 * c2, 2 * c2), 0)
    gj = lax.broadcasted_iota(jnp.int32, (2 * c2, 2 * c2), 1)
    gram_mask = (gj % CHUNK) < (gi % CHUNK) + jnp.where(gi < c2, 0, 1)
    lo_lane = (lax.broadcasted_iota(jnp.int32, (1, width), 1) % LANES) < CHUNK
    tri = (lax.broadcasted_iota(jnp.int32, (CHUNK, CHUNK), 1)
           <= lax.broadcasted_iota(jnp.int32, (CHUNK, CHUNK), 0)).astype(BF16)
    pairs = range(npair)

    def per_pair(x):
        return [x[:, p * LANES:(p + 1) * LANES] for p in pairs]

    def stacked(x):
        return [jnp.concatenate([a, b], axis=0)
                for a, b in zip(per_pair(jnp.where(lo_lane, x, 0.0)), per_pair(jnp.where(lo_lane, 0.0, x)))]

    def chunk(c, carry):
        sl = pl.ds(pl.multiple_of(c * CHUNK, CHUNK), CHUNK)
        lw = lw_ref[sl, :]
        l_hi = lw.astype(BF16)
        l_mid = (lw - l_hi.astype(F32)).astype(BF16)
        l_lo = (lw - l_hi.astype(F32) - l_mid.astype(F32)).astype(BF16)
        cum = _dot(tri, l_hi) + _dot(tri, l_mid) + _dot(tri, l_lo)
        tot = cum[CHUNK - 1:CHUNK, :]
        e_neg = jnp.exp(-cum)
        e_rest = jnp.exp(tot - cum)
        k = k_ref[sl, :]
        kk = kk_ref[sl, :]
        kb = kk * a_ref[sl, :]
        at = stacked(-kk * jnp.exp(cum - lw))
        rt = stacked(r_ref[sl, :] * jnp.exp(cum))
        bt = stacked(kb * e_neg)
        kt = stacked(k * e_neg)
        bw = stacked(kb * e_rest)
        kw = stacked(k * e_rest)
        vm = stacked(v_ref[sl, :])
        decay = per_pair(jnp.exp(tot))

        lhs = [jnp.concatenate([a, r], axis=0).astype(BF16) for a, r in zip(at, rt)]
        gram = [jnp.where(gram_mask, _dot_nt(l, jnp.concatenate([b, kq], axis=0)), 0.0)
                for l, b, kq in zip(lhs, bt, kt)]
        abd = [g[:c2, :c2] for g in gram]
        dg = [jnp.where(sub_diag, x, 0.0) for x in abd]
        off = [x - d for x, d in zip(abd, dg)]
        t = [eye + d for d in dg]
        x = [_dot(d, d) for d in dg]
        for _ in range(2):
            res = [_dot(xi, jnp.concatenate([ti, xi], axis=1)) for xi, ti in zip(x, t)]
            t = [ti + r[:, :c2] for ti, r in zip(t, res)]
            x = [r[:, c2:] for r in res]
        t = [ti + _dot(xi, ti) for xi, ti in zip(x, t)]
        nn = [_dot(ti, o) for ti, o in zip(t, off)]
        res = [_dot(n, jnp.concatenate([n, ti], axis=1)) for n, ti in zip(nn, t)]
        q = [ti + r[:, c2:] for ti, r in zip(t, res)]
        tinv = [qi + _dot(r[:, :c2], qi) for qi, r in zip(q, res)]

        hs = [_dot_nt(l, s_ref[p]) for p, l in zip(pairs, lhs)]
        w = [h[:c2] + _dot(g[:c2, c2:], v) for h, g, v in zip(hs, gram, vm)]
        u = [_dot(ti, wi) for ti, wi in zip(tinv, w)]
        uv = [jnp.concatenate([ui, v], axis=0).astype(BF16) for ui, v in zip(u, vm)]
        ym = [h[c2:] + _dot(g[c2:], x2) for h, g, x2 in zip(hs, gram, uv)]
        y_ref[sl, :] = jnp.concatenate([yi[:CHUNK] + yi[CHUNK:] for yi in ym], axis=1)
        for p in pairs:
            s_ref[p] = s_ref[p] * decay[p] + _dot_tn(uv[p], jnp.concatenate([bw[p], kw[p]], axis=0))
        return carry

    lax.fori_loop(0, nchunk, chunk, 0)


def _rwkv_rec(r, lw, k, v, kk, a, batch, seq, tb, pairs_per_block):
    n, d = r.shape
    width = pairs_per_block * LANES
    nt = seq // tb
    spec = pl.BlockSpec((tb, width), lambda b, p, t: (b * nt + t, p))
    return pl.pallas_call(
        functools.partial(_rec_kernel, nchunk=tb // CHUNK, npair=pairs_per_block),
        grid=(batch, d // width, nt),
        in_specs=[spec] * 6,
        out_specs=spec,
        out_shape=jax.ShapeDtypeStruct((n, d), F32),
        scratch_shapes=[pltpu.VMEM((pairs_per_block, LANES, LANES), F32)],
        compiler_params=pltpu.CompilerParams(
            dimension_semantics=("arbitrary", "arbitrary", "arbitrary"),
            vmem_limit_bytes=VMEM_LIMIT),
        name="rwkv_rec",
    )(r, lw, k, v, kk, a)


def _rwkv_post_kernel(y_ref, bonus_ref, g_ref, vec_ref, bd_ref, out_ref):
    bd = bd_ref[...]
    y = y_ref[...]
    inv_n = 1.0 / RWKV_HEAD
    yc = y - _dot_x(y, bd) * inv_n
    var = _dot_x(yc * yc, bd) * inv_n
    yn = yc * lax.rsqrt(var + GN_EPS) * vec_ref[6:7, :] + vec_ref[7:8, :]
    out_ref[...] = ((yn + bonus_ref[...]) * g_ref[...]).astype(out_ref.dtype)


def _rwkv_post(y, bonus, g, vecs, bd, tm):
    n, d = y.shape
    return pl.pallas_call(
        _rwkv_post_kernel,
        grid=(n // tm,),
        in_specs=[_row_spec(tm, d)] * 3 + [_const_spec(vecs.shape), _const_spec(bd.shape)],
        out_specs=_row_spec(tm, d),
        out_shape=jax.ShapeDtypeStruct((n, d), BF16),
        compiler_params=pltpu.CompilerParams(dimension_semantics=("arbitrary",)),
        name="rwkv_post",
    )(y, bonus, g, vecs, bd)


def _tail_kernel(x_ref, pre_ref, p_ref, gn_ref, wo_ref, w1_ref, w2_ref, wup_ref, wgate_ref, out_ref,
                 *, ff_chunk):
    x = x_ref[...] + _dot(pre_ref[...], wo_ref[...])
    hn = _rms(x, gn_ref[0:1, :]).astype(BF16)
    d_ff = w1_ref.shape[1]
    acc = x
    for c in range(0, d_ff, ff_chunk):
        mid = jnp.maximum(_dot(hn, w1_ref[:, c:c + ff_chunk]), 0.0)
        acc = acc + _dot(mid * mid, w2_ref[c:c + ff_chunk, :])
    gate = _sigmoid(_dot(_rms(acc, gn_ref[1:2, :]), wgate_ref[...]))
    out_ref[...] = acc + _dot(p_ref[...], wup_ref[...]) * gate


def _tail(x2d, pre, p2d, gn, wo, w1, w2, wup, wgate, tm):
    n, d = x2d.shape
    consts = (gn, wo, w1, w2, wup, wgate)
    return pl.pallas_call(
        functools.partial(_tail_kernel, ff_chunk=1024),
        grid=(n // tm,),
        in_specs=[_row_spec(tm, d), _row_spec(tm, d), _row_spec(tm, p2d.shape[1])]
                 + [_const_spec(c.shape) for c in consts],
        out_specs=_row_spec(tm, d),
        out_shape=jax.ShapeDtypeStruct((n, d), F32),
        compiler_params=pltpu.CompilerParams(dimension_semantics=("arbitrary",),
                                             vmem_limit_bytes=VMEM_LIMIT),
        name="tail",
    )(x2d, pre, p2d, *consts)


def _qkv_kernel(x_ref, gn_ref, wq_ref, wk_ref, wv_ref, bd_ref, q_out, k_out, v_out):
    x = x_ref[...]
    bd = bd_ref[...]
    inv_n = 1.0 / DIFF_HEAD

    def head_rms(t, g):
        ms = _dot_x(t * t, bd) * inv_n
        return t * lax.rsqrt(ms + NORM_EPS) * g

    hq = _rms(x, gn_ref[0:1, :])
    hk = _rms(x, gn_ref[1:2, :]).astype(BF16)
    q = head_rms(_dot(hq, wq_ref[...]), gn_ref[2:3, :])
    q_out[...] = q.astype(q_out.dtype)
    k_out[...] = head_rms(_dot(hk, wk_ref[...]), gn_ref[3:4, :]).astype(k_out.dtype)
    v_out[0] = _dot(hk, wv_ref[...]).T.astype(v_out.dtype)


def _qkv(x2d, gn, wq, wk, wv, bd, tm):
    n, d = x2d.shape
    consts = (gn, wq, wk, wv, bd)
    out_sd = jax.ShapeDtypeStruct((n, d), BF16)
    return pl.pallas_call(
        _qkv_kernel,
        grid=(n // tm,),
        in_specs=[_row_spec(tm, d)] + [_const_spec(c.shape) for c in consts],
        out_specs=[_row_spec(tm, d)] * 2 + [pl.BlockSpec((1, d, tm), lambda i: (i, 0, 0))],
        out_shape=[out_sd] * 2 + [jax.ShapeDtypeStruct((n // tm, d, tm), BF16)],
        compiler_params=pltpu.CompilerParams(dimension_semantics=("arbitrary",),
                                             vmem_limit_bytes=VMEM_LIMIT),
        name="qkv",
    )(x2d, *consts)


def _bucket_tiles(tb):
    i = np.arange(tb, dtype=np.int64)[None, :]
    j = np.arange(tb, dtype=np.int64)[:, None]

    def bucket(rel):
        n = np.maximum(rel, 0)
        max_exact = N_BUCKETS // 2
        nf = np.maximum(n, 1).astype(np.float32)
        large = max_exact + (np.log(nf / np.float32(max_exact)) / np.float32(math.log(MAX_DIST / max_exact))
                             * np.float32(N_BUCKETS - max_exact)).astype(np.int32)
        large = np.minimum(large, N_BUCKETS - 1)
        return np.where(n < max_exact, n, large).astype(np.int32)

    diag = np.where(i - j >= 0, bucket(i - j), -1)
    near = bucket(tb + i - j)
    tiles = np.stack([diag, near]).astype(np.int32)
    return np.concatenate([tiles, tiles], axis=2)


def _bias_kernel(tab_ref, bucket_ref, out_ref):
    h = pl.program_id(0)
    b = bucket_ref[...]
    far = tab_ref[N_BUCKETS - 1, h]
    acc = jnp.where(b < 0, NEG_BIG, 0.0)
    for n in range(N_BUCKETS - 1):
        acc = jnp.where(b == n, (tab_ref[n, h] - far) * LOG2E, acc)
    out_ref[0] = acc


def _bias_tiles(rel_bias, tb):
    buckets = jnp.asarray(_bucket_tiles(tb))
    nh = rel_bias.shape[1]
    return pl.pallas_call(
        _bias_kernel,
        grid=(nh,),
        in_specs=[pl.BlockSpec(memory_space=pltpu.SMEM),
                  pl.BlockSpec((2, tb, 2 * tb), lambda h: (0, 0, 0))],
        out_specs=pl.BlockSpec((1, 2, tb, 2 * tb), lambda h: (h, 0, 0, 0)),
        out_shape=jax.ShapeDtypeStruct((nh, 2, tb, 2 * tb), F32),
        compiler_params=pltpu.CompilerParams(dimension_semantics=("arbitrary",)),
        name="bias",
    )(rel_bias, buckets)


def _attn_kernel(q_ref, k_ref, vt_ref, bias_ref, lam_ref, g_ref, o_ref, m_ref, l_ref, acc_ref,
                 *, tb, nhead, out_scale, lambda_init):
    qi = pl.program_id(2)
    heads = range(nhead)
    lane = lax.broadcasted_iota(jnp.int32, (1, LANES), 1)

    def head_lanes(h):
        return slice(h * LANES, (h + 1) * LANES)

    def stacked_q(h):
        q = q_ref[:, head_lanes(h)]
        zero = jnp.zeros_like(q)
        return jnp.concatenate([jnp.where(lane < DIFF_HEAD, q, zero),
                                jnp.where(lane < DIFF_HEAD, zero, q)], axis=0)

    qs = [stacked_q(h) for h in heads]
    m_ref[...] = jnp.full_like(m_ref, NEG_BIG)
    l_ref[...] = jnp.zeros_like(l_ref)
    acc_ref[...] = jnp.zeros_like(acc_ref)

    def advance(j, nblk, bias_idx):
        rows = pl.ds(pl.multiple_of(j * tb, tb), nblk * tb)
        logits = lambda h: _dot_nt(k_ref[rows, head_lanes(h)], qs[h])
        ahead = min(2, nhead)
        st = {h: logits(h) for h in range(ahead)}
        for h in heads:
            if h + ahead < nhead:
                st[h + ahead] = logits(h + ahead)
            s = st.pop(h) if bias_idx is None else st.pop(h) + bias_ref[h, bias_idx]
            m_old = m_ref[h]
            m_new = jnp.maximum(m_old, jnp.max(s, axis=0, keepdims=True))
            alpha = jnp.exp2(m_old - m_new)
            p = jnp.exp2(s - m_new)
            l_ref[h] = alpha * l_ref[h] + jnp.sum(p, axis=0, keepdims=True)
            p = p.astype(BF16)
            pv = _dot(vt_ref[j, head_lanes(h), :], p[:tb])
            for i in range(1, nblk):
                pv = pv + _dot(vt_ref[j + i, head_lanes(h), :], p[i * tb:(i + 1) * tb])
            acc_ref[h] = alpha * acc_ref[h] + pv
            m_ref[h] = m_new

    def far_pair(t, carry):
        advance(2 * t, 2, None)
        return carry

    nfar = jnp.maximum(qi - 1, 0)
    lax.fori_loop(0, nfar // 2, far_pair, 0)

    @pl.when(nfar % 2 == 1)
    def _():
        advance(nfar - 1, 1, None)

    @pl.when(qi >= 1)
    def _():
        advance(qi - 1, 1, 1)

    advance(qi, 1, 0)

    lam = lam_ref[...]
    lam_full = (jnp.exp(jnp.sum(lam[0:1] * lam[1:2], axis=1, keepdims=True))
                - jnp.exp(jnp.sum(lam[2:3] * lam[3:4], axis=1, keepdims=True)) + lambda_init)
    for h in heads:
        o = acc_ref[h] * (1.0 / l_ref[h])
        o = o[:, :tb] - lam_full * o[:, tb:]
        o = o * lax.rsqrt(jnp.mean(o * o, axis=0, keepdims=True) + NORM_EPS)
        o_ref[:, head_lanes(h)] = (o.T * (g_ref[...] * out_scale)).astype(o_ref.dtype)


def _attn(q, k, vt, bias, lam, subln_g, batch, seq, tb, nhead, layer_idx):
    n, d = q.shape
    ngroup = d // (LANES * nhead)
    width = LANES * nhead
    nq = seq // tb
    lambda_init = 0.8 - 0.6 * math.exp(-0.3 * layer_idx)
    q_spec = pl.BlockSpec((tb, width), lambda b, h, i: (b * nq + i, h))
    return pl.pallas_call(
        functools.partial(_attn_kernel, tb=tb, nhead=nhead, out_scale=1.0 - lambda_init,
                          lambda_init=lambda_init),
        grid=(batch, ngroup, nq),
        in_specs=[q_spec,
                  pl.BlockSpec((seq, width), lambda b, h, i: (b, h)),
                  pl.BlockSpec((nq, width, tb), lambda b, h, i: (b, h, 0)),
                  pl.BlockSpec((nhead, 2, tb, 2 * tb), lambda b, h, i: (h, 0, 0, 0)),
                  pl.BlockSpec(lam.shape, lambda b, h, i: (0, 0)),
                  pl.BlockSpec(subln_g.shape, lambda b, h, i: (0, 0))],
        out_specs=q_spec,
        out_shape=jax.ShapeDtypeStruct((n, d), BF16),
        scratch_shapes=[pltpu.VMEM((nhead, 1, 2 * tb), F32), pltpu.VMEM((nhead, 1, 2 * tb), F32),
                        pltpu.VMEM((nhead, LANES, 2 * tb), F32)],
        compiler_params=pltpu.CompilerParams(
            dimension_semantics=("arbitrary", "arbitrary", "arbitrary"),
            vmem_limit_bytes=VMEM_LIMIT),
        name="attn",
    )(q, k, vt, bias, lam, subln_g)


def kernel(x, p, norm_g, mlp_w1, mlp_w2, ple_w_up, ple_w_gate, rwkv_mix, rwkv_w_rkvo, rwkv_w0, rwkv_w1, rwkv_w2, rwkv_a0, rwkv_a1, rwkv_a2, rwkv_g1, rwkv_g2, rwkv_k_k, rwkv_k_a, rwkv_r_k, rwkv_ln_w, rwkv_ln_b, kv_norm_g, w_k_shared, w_v_shared, k_norm_g, diff_w_q, diff_q_norm_g, diff_lam, diff_subln_g, diff_w_o, rel_bias):
    batch, seq, d = x.shape
    assert d == D_MODEL and norm_g.shape[0] == 2
    n = batch * seq
    tm = min(256, seq)
    tb_rec = min(256, seq)
    tb_attn = min(256, seq)
    assert seq % tm == 0 and seq % tb_rec == 0 and seq % tb_attn == 0 and tb_attn >= MAX_DIST

    bf = lambda w: w.astype(BF16)
    x2d = x.reshape(n, d)
    p2d = p.reshape(p.shape[0], n, p.shape[-1])
    head_id = jnp.arange(d, dtype=jnp.int32) // RWKV_HEAD
    bd = (head_id[:, None] == head_id[None, :]).astype(BF16)

    vecs = jnp.stack([norm_g[0, 0], rwkv_w0[0], rwkv_a0[0], rwkv_k_k[0], rwkv_k_a[0],
                      rwkv_r_k[0].reshape(d), rwkv_ln_w[0], rwkv_ln_b[0]])
    r, lw, k, v, kk, a, g, bonus = _rwkv_pre(
        x2d, vecs, rwkv_mix[0], bf(rwkv_w_rkvo[0, 0]), bf(rwkv_w_rkvo[0, 1]), bf(rwkv_w_rkvo[0, 2]),
        bf(rwkv_w1[0]), bf(rwkv_w2[0]), bf(rwkv_a1[0]), bf(rwkv_a2[0]), bf(rwkv_g1[0]), bf(rwkv_g2[0]),
        bd, seq, tm)
    y = _rwkv_rec(r, lw, k, v, kk, a, batch, seq, tb_rec, d // LANES)
    pre = _rwkv_post(y, bonus, g, vecs, bd, tm)
    x2d = _tail(x2d, pre, p2d[0], norm_g[0, 1:3], bf(rwkv_w_rkvo[0, 3]), bf(mlp_w1[0]), bf(mlp_w2[0]),
                bf(ple_w_up[0]), bf(ple_w_gate[0]), tm)

    reps = d // DIFF_HEAD
    gn = jnp.stack([norm_g[1, 0], kv_norm_g,
                    jnp.tile(diff_q_norm_g[0], reps) * (DIFF_HEAD ** -0.5 * LOG2E), jnp.tile(k_norm_g, reps)])
    q, kq, vq = _qkv(x2d, gn, bf(diff_w_q[0]), bf(w_k_shared), bf(w_v_shared), bd, tb_attn)
    bias = _bias_tiles(rel_bias, tb_attn)
    o = _attn(q, kq, vq, bias, diff_lam[0], diff_subln_g[0].reshape(1, LANES), batch, seq, tb_attn, 4, 1)
    x2d = _tail(x2d, o, p2d[1], norm_g[1, 1:3], bf(diff_w_o[0]), bf(mlp_w1[1]), bf(mlp_w2[1]),
                bf(ple_w_up[1]), bf(ple_w_gate[1]), tm)
    return x2d.reshape(batch, seq, d)
```

```python
import functools
import math

import numpy as np
import jax
import jax.numpy as jnp
from jax import lax
from jax.experimental import pallas as pl
from jax.experimental.pallas import tpu as pltpu

F32 = jnp.float32
BF16 = jnp.bfloat16

D_MODEL = 1024
RWKV_HEAD = 64
DIFF_HEAD = 64
DIFF_HEADS = 8
LANES = 128
CHUNK = 64
SUB = 16
FF_CHUNK = 1024
NORM_EPS = 1e-6
GN_EPS = 64e-5
N_BUCKETS = 32
MAX_DIST = 128
NEG_BIG = -1e30
LOG2E = math.log2(math.e)
LOGIT_BOUND = 60.0
VMEM_LIMIT = 56 * 1024 * 1024


def _dot(a, b):
    return jnp.dot(a.astype(BF16), b.astype(BF16), preferred_element_type=F32)


def _dot_nt(a, b):
    return lax.dot_general(a.astype(BF16), b.astype(BF16), (((1,), (1,)), ((), ())),
                           preferred_element_type=F32)


def _dot_tn(a, b):
    return lax.dot_general(a.astype(BF16), b.astype(BF16), (((0,), (0,)), ((), ())),
                           preferred_element_type=F32)


def _rms(x, g):
    return x * lax.rsqrt(jnp.mean(x * x, axis=-1, keepdims=True) + NORM_EPS) * g


def _sigmoid(x):
    return 1.0 / (1.0 + jnp.exp(-x))


def _const_spec(shape):
    nd = len(shape)
    return pl.BlockSpec(shape, lambda *_: (0,) * nd, pipeline_mode=pl.Buffered(1))


def _row_spec(tm, width):
    return pl.BlockSpec((tm, width), lambda i: (i, 0))


def _rwkv_pre_kernel(x_ref, xp_ref, vec_ref, mix_ref, wr_ref, wk_ref, wv_ref,
                     w1_ref, w2_ref, a1_ref, a2_ref, g1_ref, g2_ref, bd_ref,
                     r_out, lw_out, k_out, v_out, kk_out, a_out, g_out, bonus_out,
                     *, blocks_per_seq):
    i = pl.program_id(0)
    g0 = vec_ref[0:1, :]
    h = _rms(x_ref[...], g0)
    prev = _rms(xp_ref[...], g0)[7:8, :]
    prev = jnp.where(i % blocks_per_seq == 0, 0.0, prev)
    row = lax.broadcasted_iota(jnp.int32, h.shape, 0)
    hprev = jnp.where(row == 0, prev, pltpu.roll(h, 1, 0))
    dx = hprev - h

    def mixed(j):
        return (h + dx * mix_ref[j:j + 1, :]).astype(BF16)

    r = _dot(mixed(0), wr_ref[...])
    k = _dot(mixed(2), wk_ref[...])
    v = _dot(mixed(3), wv_ref[...])
    w_lora = _dot(jnp.tanh(_dot(mixed(1), w1_ref[...])), w2_ref[...])
    a_lora = _dot(_dot(mixed(4), a1_ref[...]), a2_ref[...])
    g = _dot(_sigmoid(_dot(mixed(5), g1_ref[...])), g2_ref[...])

    z = -(vec_ref[1:2, :] + w_lora)
    softplus = jnp.maximum(z, 0.0) + jnp.log1p(jnp.exp(-jnp.abs(z)))
    lw = -jnp.exp(-softplus - 0.5)
    a = _sigmoid(vec_ref[2:3, :] + a_lora)

    bd = bd_ref[...]
    kk = k * vec_ref[3:4, :]
    ss = _dot(kk * kk, bd)
    kk = kk / jnp.maximum(jnp.sqrt(ss), 1e-12)
    k = k * (1.0 + (a - 1.0) * vec_ref[4:5, :])
    bonus = _dot(r * k * vec_ref[5:6, :], bd) * v

    r_out[...] = r
    lw_out[...] = lw
    k_out[...] = k
    v_out[...] = v
    kk_out[...] = kk
    a_out[...] = a
    g_out[...] = g.astype(g_out.dtype)
    bonus_out[...] = bonus.astype(bonus_out.dtype)


def _rwkv_pre(x2d, vecs, mix, wr, wk, wv, w1, w2, a1, a2, g1, g2, bd, seq, tm):
    n, d = x2d.shape
    blocks_per_seq = seq // tm
    consts = (vecs, mix, wr, wk, wv, w1, w2, a1, a2, g1, g2, bd)
    out_sd = jax.ShapeDtypeStruct((n, d), F32)
    return pl.pallas_call(
        functools.partial(_rwkv_pre_kernel, blocks_per_seq=blocks_per_seq),
        grid=(n // tm,),
        in_specs=[_row_spec(tm, d),
                  pl.BlockSpec((8, d), lambda i: (jnp.maximum(i * (tm // 8) - 1, 0), 0))]
                 + [_const_spec(c.shape) for c in consts],
        out_specs=[_row_spec(tm, d)] * 8,
        out_shape=[out_sd] * 6 + [jax.ShapeDtypeStruct((n, d), BF16)] * 2,
        compiler_params=pltpu.CompilerParams(dimension_semantics=("arbitrary",),
                                             vmem_limit_bytes=VMEM_LIMIT),
        name="rwkv_pre",
    )(x2d, x2d, *consts)


def _rec_kernel(r_ref, lw_ref, k_ref, v_ref, kk_ref, a_ref, y_ref, s_ref, *, nchunk, npair):
    c2 = 2 * CHUNK
    width = npair * LANES

    @pl.when(pl.program_id(2) == 0)
    def _():
        s_ref[...] = jnp.zeros_like(s_ref)

    ri = lax.broadcasted_iota(jnp.int32, (c2, c2), 0)
    ci = lax.broadcasted_iota(jnp.int32, (c2, c2), 1)
    sub_diag = (ri // SUB) == (ci // SUB)
    eye = (ri == ci).astype(F32)
    gi = lax.broadcasted_iota(jnp.int32, (2 * c2, 2 * c2), 0)
    gj = lax.broadcasted_iota(jnp.int32, (2 * c2, 2 * c2), 1)
    gram_mask = (gj % CHUNK) < (gi % CHUNK) + jnp.where(gi < c2, 0, 1)
    lo_lane = (lax.broadcasted_iota(jnp.int32, (1, width), 1) % LANES) < CHUNK
    tri = (lax.broadcasted_iota(jnp.int32, (CHUNK, CHUNK), 1)
           <= lax.broadcasted_iota(jnp.int32, (CHUNK, CHUNK), 0)).astype(BF16)
    pairs = range(npair)

    def per_pair(x):
        return [x[:, p * LANES:(p + 1) * LANES] for p in pairs]

    def stacked(x):
        return [jnp.concatenate([a, b], axis=0)
                for a, b in zip(per_pair(jnp.where(lo_lane, x, 0.0)), per_pair(jnp.where(lo_lane, 0.0, x)))]

    def chunk(c, carry):
        sl = pl.ds(pl.multiple_of(c * CHUNK, CHUNK), CHUNK)
        lw = lw_ref[sl, :]
        l_hi = lw.astype(BF16)
        l_mid = (lw - l_hi.astype(F32)).astype(BF16)
        l_lo = (lw - l_hi.astype(F32) - l_mid.astype(F32)).astype(BF16)
        cum = _dot(tri, l_hi) + _dot(tri, l_mid) + _dot(tri, l_lo)
        tot = cum[CHUNK - 1:CHUNK, :]
        e_neg = jnp.exp(-cum)
        e_rest = jnp.exp(tot - cum)
        k = k_ref[sl, :]
        kk = kk_ref[sl, :]
        kb = kk * a_ref[sl, :]
        at = stacked(-kk * jnp.exp(cum - lw))
        rt = stacked(r_ref[sl, :] * jnp.exp(cum))
        bt = stacked(kb * e_neg)
        kt = stacked(k * e_neg)
        bw = stacked(kb * e_rest)
        kw = stacked(k * e_rest)
        vm = stacked(v_ref[sl, :])
        decay = per_pair(jnp.exp(tot))

        lhs = [jnp.concatenate([a, r], axis=0).astype(BF16) for a, r in zip(at, rt)]
        gram = [jnp.where(gram_mask, _dot_nt(l, jnp.concatenate([b, kq], axis=0)), 0.0)
                for l, b, kq in zip(lhs, bt, kt)]
        abd = [g[:c2, :c2] for g in gram]
        dg = [jnp.where(sub_diag, x, 0.0) for x in abd]
        off = [x - d for x, d in zip(abd, dg)]
        t = [eye + d for d in dg]
        x = [_dot(d, d) for d in dg]
        for _ in range(2):
            res = [_dot(xi, jnp.concatenate([ti, xi], axis=1)) for xi, ti in zip(x, t)]
            t = [ti + r[:, :c2] for ti, r in zip(t, res)]
            x = [r[:, c2:] for r in res]
        t = [ti + _dot(xi, ti) for xi, ti in zip(x, t)]
        nn = [_dot(ti, o) for ti, o in zip(t, off)]
        res = [_dot(n, jnp.concatenate([n, ti], axis=1)) for n, ti in zip(nn, t)]
        q = [ti + r[:, c2:] for ti, r in zip(t, res)]
        tinv = [qi + _dot(r[:, :c2], qi) for qi, r in zip(q, res)]

        hs = [_dot_nt(l, s_ref[p]) for p, l in zip(pairs, lhs)]
        w = [h[:c2] + _dot(g[:c2, c2:], v) for h, g, v in zip(hs, gram, vm)]
        u = [_dot(ti, wi) for ti, wi in zip(tinv, w)]
        uv = [jnp.concatenate([ui, v], axis=0).astype(BF16) for ui, v in zip(u, vm)]
        ym = [h[c2:] + _dot(g[c2:], x2) for h, g, x2 in zip(hs, gram, uv)]
        y_ref[sl, :] = jnp.concatenate([yi[:CHUNK] + yi[CHUNK:] for yi in ym], axis=1)
        for p in pairs:
            s_ref[p] = s_ref[p] * decay[p] + _dot_tn(uv[p], jnp.concatenate([bw[p], kw[p]], axis=0))
        return carry

    lax.fori_loop(0, nchunk, chunk, 0)


def _rwkv_rec(r, lw, k, v, kk, a, batch, seq, tb, pairs_per_block):
    n, d = r.shape
    width = pairs_per_block * LANES
    nt = seq // tb
    spec = pl.BlockSpec((tb, width), lambda b, p, t: (b * nt + t, p))
    return pl.pallas_call(
        functools.partial(_rec_kernel, nchunk=tb // CHUNK, npair=pairs_per_block),
        grid=(batch, d // width, nt),
        in_specs=[spec] * 6,
        out_specs=spec,
        out_shape=jax.ShapeDtypeStruct((n, d), F32),
        scratch_shapes=[pltpu.VMEM((pairs_per_block, LANES, LANES), F32)],
        compiler_params=pltpu.CompilerParams(
            dimension_semantics=("arbitrary", "arbitrary", "arbitrary"),
            vmem_limit_bytes=VMEM_LIMIT),
        name="rwkv_rec",
    )(r, lw, k, v, kk, a)


def _tail_body(x, pre, p_ref, gn_ref, wo_ref, w1_ref, w2_ref, wup_ref, wgate_ref, out_ref):
    x = x + _dot(pre, wo_ref[...])
    hn = _rms(x, gn_ref[0:1, :]).astype(BF16)
    d_ff = w1_ref.shape[1]
    acc = x
    for c in range(0, d_ff, FF_CHUNK):
        mid = jnp.maximum(_dot(hn, w1_ref[:, c:c + FF_CHUNK]), 0.0)
        acc = acc + _dot(mid * mid, w2_ref[c:c + FF_CHUNK, :])
    gate = _sigmoid(_dot(_rms(acc, gn_ref[1:2, :]), wgate_ref[...]))
    out_ref[...] = acc + _dot(p_ref[...], wup_ref[...]) * gate


def _tail_attn_kernel(x_ref, o_ref, p_ref, *rest):
    _tail_body(x_ref[...], o_ref[...], p_ref, *rest)


def _tail_rwkv_kernel(x_ref, y_ref, bonus_ref, g_ref, p_ref, vec_ref, bd_ref, *rest):
    bd = bd_ref[...]
    y = y_ref[...]
    inv_n = 1.0 / RWKV_HEAD
    yc = y - _dot(y, bd) * inv_n
    var = _dot(yc * yc, bd) * inv_n
    yn = yc * lax.rsqrt(var + GN_EPS) * vec_ref[6:7, :] + vec_ref[7:8, :]
    _tail_body(x_ref[...], (yn + bonus_ref[...]) * g_ref[...], p_ref, *rest)


def _tail(kernel_fn, rows, p2d, extra_consts, gn, wo, w1, w2, wup, wgate, tm):
    n, d = rows[0].shape
    consts = tuple(extra_consts) + (gn, wo, w1, w2, wup, wgate)
    return pl.pallas_call(
        kernel_fn,
        grid=(n // tm,),
        in_specs=[_row_spec(tm, d)] * len(rows) + [_row_spec(tm, p2d.shape[1])]
                 + [_const_spec(c.shape) for c in consts],
        out_specs=_row_spec(tm, d),
        out_shape=jax.ShapeDtypeStruct((n, d), F32),
        compiler_params=pltpu.CompilerParams(dimension_semantics=("arbitrary",),
                                             vmem_limit_bytes=VMEM_LIMIT),
        name="tail",
    )(*rows, p2d, *consts)


def _qkv_kernel(x_ref, gn_ref, wq_ref, wk_ref, wv_ref, bd_ref, q_out, k_out, v_out):
    x = x_ref[...]
    bd = bd_ref[...]
    inv_n = 1.0 / DIFF_HEAD

    def head_rms(t, g):
        ms = _dot(t * t, bd) * inv_n
        return t * lax.rsqrt(ms + NORM_EPS) * g

    hq = _rms(x, gn_ref[0:1, :])
    hk = _rms(x, gn_ref[1:2, :]).astype(BF16)
    q = head_rms(_dot(hq, wq_ref[...]), gn_ref[2:3, :])
    q_out[...] = q.astype(q_out.dtype)
    k_out[...] = head_rms(_dot(hk, wk_ref[...]), gn_ref[3:4, :]).astype(k_out.dtype)
    v_out[0] = _dot(hk, wv_ref[...]).T.astype(v_out.dtype)


def _qkv(x2d, gn, wq, wk, wv, bd, tm):
    n, d = x2d.shape
    consts = (gn, wq, wk, wv, bd)
    out_sd = jax.ShapeDtypeStruct((n, d), BF16)
    return pl.pallas_call(
        _qkv_kernel,
        grid=(n // tm,),
        in_specs=[_row_spec(tm, d)] + [_const_spec(c.shape) for c in consts],
        out_specs=[_row_spec(tm, d)] * 2 + [pl.BlockSpec((1, d, tm), lambda i: (i, 0, 0))],
        out_shape=[out_sd] * 2 + [jax.ShapeDtypeStruct((n // tm, d, tm), BF16)],
        compiler_params=pltpu.CompilerParams(dimension_semantics=("arbitrary",),
                                             vmem_limit_bytes=VMEM_LIMIT),
        name="qkv",
    )(x2d, *consts)


def _bucket_tiles(tb):
    i = np.arange(tb, dtype=np.int64)[None, :]
    j = np.arange(tb, dtype=np.int64)[:, None]

    def bucket(rel):
        n = np.maximum(rel, 0)
        max_exact = N_BUCKETS // 2
        nf = np.maximum(n, 1).astype(np.float32)
        large = max_exact + (np.log(nf / np.float32(max_exact)) / np.float32(math.log(MAX_DIST / max_exact))
                             * np.float32(N_BUCKETS - max_exact)).astype(np.int32)
        large = np.minimum(large, N_BUCKETS - 1)
        return np.where(n < max_exact, n, large).astype(np.int32)

    diag = np.where(i - j >= 0, bucket(i - j), -1)
    near = bucket(tb + i - j)
    tiles = np.stack([diag, near]).astype(np.int32)
    return np.concatenate([tiles, tiles], axis=2)


def _bias_kernel(tab_ref, bucket_ref, out_ref):
    h = pl.program_id(0)
    b = bucket_ref[...]
    far = tab_ref[N_BUCKETS - 1, h]
    acc = jnp.where(b < 0, NEG_BIG, 0.0)
    for n in range(N_BUCKETS - 1):
        acc = jnp.where(b == n, (tab_ref[n, h] - far) * LOG2E, acc)
    out_ref[0] = acc


def _bias_tiles(rel_bias, tb):
    buckets = jnp.asarray(_bucket_tiles(tb))
    nh = rel_bias.shape[1]
    return pl.pallas_call(
        _bias_kernel,
        grid=(nh,),
        in_specs=[pl.BlockSpec(memory_space=pltpu.SMEM),
                  pl.BlockSpec((2, tb, 2 * tb), lambda h: (0, 0, 0))],
        out_specs=pl.BlockSpec((1, 2, tb, 2 * tb), lambda h: (h, 0, 0, 0)),
        out_shape=jax.ShapeDtypeStruct((nh, 2, tb, 2 * tb), F32),
        compiler_params=pltpu.CompilerParams(dimension_semantics=("arbitrary",)),
        name="bias",
    )(rel_bias, buckets)


def _attn_kernel(q_ref, k_ref, vt_ref, bias_ref, lam_ref, g_ref, o_ref, m_ref, l_ref, acc_ref,
                 *, tb, nhead, out_scale, lambda_init, bounded):
    qi = pl.program_id(2)
    heads = range(nhead)
    lane = lax.broadcasted_iota(jnp.int32, (1, LANES), 1)

    def head_lanes(h):
        return slice(h * LANES, (h + 1) * LANES)

    def stacked_q(h):
        q = q_ref[:, head_lanes(h)]
        zero = jnp.zeros_like(q)
        return jnp.concatenate([jnp.where(lane < DIFF_HEAD, q, zero),
                                jnp.where(lane < DIFF_HEAD, zero, q)], axis=0)

    qs = [stacked_q(h) for h in heads]
    m_ref[...] = jnp.full_like(m_ref, NEG_BIG)
    l_ref[...] = jnp.zeros_like(l_ref)
    acc_ref[...] = jnp.zeros_like(acc_ref)

    def advance(j, nblk, bias_idx):
        rows = pl.ds(pl.multiple_of(j * tb, tb), nblk * tb)
        logits = lambda h: _dot_nt(k_ref[rows, head_lanes(h)], qs[h])
        ahead = min(2, nhead)
        st = {h: logits(h) for h in range(ahead)}
        for h in heads:
            if h + ahead < nhead:
                st[h + ahead] = logits(h + ahead)
            s = st.pop(h) if bias_idx is None else st.pop(h) + bias_ref[h, bias_idx]
            if bounded:
                p = jnp.exp2(s)
                l_ref[h] = l_ref[h] + jnp.sum(p, axis=0, keepdims=True)
            else:
                m_old = m_ref[h]
                m_new = jnp.maximum(m_old, jnp.max(s, axis=0, keepdims=True))
                alpha = jnp.exp2(m_old - m_new)
                p = jnp.exp2(s - m_new)
                l_ref[h] = alpha * l_ref[h] + jnp.sum(p, axis=0, keepdims=True)
                m_ref[h] = m_new
            p = p.astype(BF16)
            pv = _dot(vt_ref[j, head_lanes(h), :], p[:tb])
            for i in range(1, nblk):
                pv = pv + _dot(vt_ref[j + i, head_lanes(h), :], p[i * tb:(i + 1) * tb])
            acc_ref[h] = acc_ref[h] + pv if bounded else alpha * acc_ref[h] + pv

    def far_pair(t, carry):
        advance(2 * t, 2, None)
        return carry

    nfar = jnp.maximum(qi - 1, 0)
    lax.fori_loop(0, nfar // 2, far_pair, 0)

    @pl.when(nfar % 2 == 1)
    def _():
        advance(nfar - 1, 1, None)

    @pl.when(qi >= 1)
    def _():
        advance(qi - 1, 1, 1)

    advance(qi, 1, 0)

    lam = lam_ref[...]
    lam_full = (jnp.exp(jnp.sum(lam[0:1] * lam[1:2], axis=1, keepdims=True))
                - jnp.exp(jnp.sum(lam[2:3] * lam[3:4], axis=1, keepdims=True)) + lambda_init)
    for h in heads:
        o = acc_ref[h] * (1.0 / l_ref[h])
        o = o[:, :tb] - lam_full * o[:, tb:]
        o = o * lax.rsqrt(jnp.mean(o * o, axis=0, keepdims=True) + NORM_EPS)
        o_ref[:, head_lanes(h)] = (o.T * (g_ref[...] * out_scale)).astype(o_ref.dtype)


def _attn(q, k, vt, bias, lam, subln_g, batch, seq, tb, nhead, layer_idx, bounded):
    n, d = q.shape
    ngroup = d // (LANES * nhead)
    width = LANES * nhead
    nq = seq // tb
    lambda_init = 0.8 - 0.6 * math.exp(-0.3 * layer_idx)
    q_spec = pl.BlockSpec((tb, width), lambda b, h, i: (b * nq + i, h))
    return pl.pallas_call(
        functools.partial(_attn_kernel, tb=tb, nhead=nhead, out_scale=1.0 - lambda_init,
                          lambda_init=lambda_init, bounded=bounded),
        grid=(batch, ngroup, nq),
        in_specs=[q_spec,
                  pl.BlockSpec((seq, width), lambda b, h, i: (b, h)),
                  pl.BlockSpec((nq, width, tb), lambda b, h, i: (b, h, 0)),
                  pl.BlockSpec((nhead, 2, tb, 2 * tb), lambda b, h, i: (h, 0, 0, 0)),
                  pl.BlockSpec(lam.shape, lambda b, h, i: (0, 0)),
                  pl.BlockSpec(subln_g.shape, lambda b, h, i: (0, 0))],
        out_specs=q_spec,
        out_shape=jax.ShapeDtypeStruct((n, d), BF16),
        scratch_shapes=[pltpu.VMEM((nhead, 1, 2 * tb), F32), pltpu.VMEM((nhead, 1, 2 * tb), F32),
                        pltpu.VMEM((nhead, LANES, 2 * tb), F32)],
        compiler_params=pltpu.CompilerParams(
            dimension_semantics=("arbitrary", "arbitrary", "arbitrary"),
            vmem_limit_bytes=VMEM_LIMIT),
        name="attn",
    )(q, k, vt, bias, lam, subln_g)


def kernel(x, p, norm_g, mlp_w1, mlp_w2, ple_w_up, ple_w_gate, rwkv_mix, rwkv_w_rkvo, rwkv_w0, rwkv_w1, rwkv_w2, rwkv_a0, rwkv_a1, rwkv_a2, rwkv_g1, rwkv_g2, rwkv_k_k, rwkv_k_a, rwkv_r_k, rwkv_ln_w, rwkv_ln_b, kv_norm_g, w_k_shared, w_v_shared, k_norm_g, diff_w_q, diff_q_norm_g, diff_lam, diff_subln_g, diff_w_o, rel_bias):
    batch, seq, d = x.shape
    assert d == D_MODEL and norm_g.shape[0] == 2
    n = batch * seq
    tm = min(256, seq)
    tb_rec = min(256, seq)
    tb_attn = min(256, seq)
    assert seq % tm == 0 and seq % tb_rec == 0 and seq % tb_attn == 0 and tb_attn >= MAX_DIST

    bf = lambda w: w.astype(BF16)
    x2d = x.reshape(n, d)
    p2d = p.reshape(p.shape[0], n, p.shape[-1])
    head_id = jnp.arange(d, dtype=jnp.int32) // RWKV_HEAD
    bd = (head_id[:, None] == head_id[None, :]).astype(BF16)

    vecs = jnp.stack([norm_g[0, 0], rwkv_w0[0], rwkv_a0[0], rwkv_k_k[0], rwkv_k_a[0],
                      rwkv_r_k[0].reshape(d), rwkv_ln_w[0], rwkv_ln_b[0]])
    r, lw, k, v, kk, a, g, bonus = _rwkv_pre(
        x2d, vecs, rwkv_mix[0], bf(rwkv_w_rkvo[0, 0]), bf(rwkv_w_rkvo[0, 1]), bf(rwkv_w_rkvo[0, 2]),
        bf(rwkv_w1[0]), bf(rwkv_w2[0]), bf(rwkv_a1[0]), bf(rwkv_a2[0]), bf(rwkv_g1[0]), bf(rwkv_g2[0]),
        bd, seq, tm)
    y = _rwkv_rec(r, lw, k, v, kk, a, batch, seq, tb_rec, d // LANES)
    x2d = _tail(_tail_rwkv_kernel, (x2d, y, bonus, g), p2d[0], (vecs, bd), norm_g[0, 1:3],
                bf(rwkv_w_rkvo[0, 3]), bf(mlp_w1[0]), bf(mlp_w2[0]), bf(ple_w_up[0]), bf(ple_w_gate[0]), tm)

    reps = d // DIFF_HEAD
    gn = jnp.stack([norm_g[1, 0], kv_norm_g,
                    jnp.tile(diff_q_norm_g[0], reps) * (DIFF_HEAD ** -0.5 * LOG2E), jnp.tile(k_norm_g, reps)])
    q, kq, vq = _qkv(x2d, gn, bf(diff_w_q[0]), bf(w_k_shared), bf(w_v_shared), bd, tb_attn)
    bias = _bias_tiles(rel_bias, tb_attn)
    logit_bound = 1.02 * LOG2E * (DIFF_HEAD ** 0.5 * jnp.max(jnp.abs(diff_q_norm_g[0] * k_norm_g))
                                  + jnp.max(jnp.abs(rel_bias - rel_bias[-1:])))
    attn = functools.partial(_attn, q, kq, vq, bias, diff_lam[0], diff_subln_g[0].reshape(1, LANES),
                             batch, seq, tb_attn, 4, 1)
    o = lax.cond(logit_bound <= LOGIT_BOUND, lambda: attn(True), lambda: attn(False))
    x2d = _tail(_tail_attn_kernel, (x2d, o), p2d[1], (), norm_g[1, 1:3],
                bf(diff_w_o[0]), bf(mlp_w1[1]), bf(mlp_w2[1]), bf(ple_w_up[1]), bf(ple_w_gate[1]), tm)
    return x2d.reshape(batch, seq, d)
```

```python
import functools
import math

import numpy as np
import jax
import jax.numpy as jnp
from jax import lax
from jax.experimental import pallas as pl
from jax.experimental.pallas import tpu as pltpu

F32 = jnp.float32
BF16 = jnp.bfloat16

D_MODEL = 1024
RWKV_HEAD = 64
DIFF_HEAD = 64
DIFF_HEADS = 8
LANES = 128
CHUNK = 64
SUB = 16
FF_CHUNK = 1024
NORM_EPS = 1e-6
GN_EPS = 64e-5
N_BUCKETS = 32
MAX_DIST = 128
NEG_BIG = -1e30
LOG2E = math.log2(math.e)
LOGIT_BOUND = 60.0
VMEM_LIMIT = 56 * 1024 * 1024


def _dot(a, b):
    return jnp.dot(a.astype(BF16), b.astype(BF16), preferred_element_type=F32)


def _dot_nt(a, b):
    return lax.dot_general(a.astype(BF16), b.astype(BF16), (((1,), (1,)), ((), ())),
                           preferred_element_type=F32)


def _dot_tn(a, b):
    return lax.dot_general(a.astype(BF16), b.astype(BF16), (((0,), (0,)), ((), ())),
                           preferred_element_type=F32)


def _rms(x, g):
    return x * lax.rsqrt(jnp.mean(x * x, axis=-1, keepdims=True) + NORM_EPS) * g


def _sigmoid(x):
    return 1.0 / (1.0 + jnp.exp(-x))


def _const_spec(shape):
    nd = len(shape)
    return pl.BlockSpec(shape, lambda *_: (0,) * nd, pipeline_mode=pl.Buffered(1))


def _row_spec(tm, width):
    return pl.BlockSpec((tm, width), lambda i: (i, 0))


def _rwkv_pre_kernel(x_ref, xp_ref, vec_ref, mix_ref, wr_ref, wk_ref, wv_ref,
                     w1_ref, w2_ref, a1_ref, a2_ref, g1_ref, g2_ref, bd_ref,
                     r_out, lw_out, k_out, v_out, kk_out, a_out, g_out, bonus_out,
                     *, blocks_per_seq):
    i = pl.program_id(0)
    g0 = vec_ref[0:1, :]
    h = _rms(x_ref[...], g0)
    prev = _rms(xp_ref[...], g0)[7:8, :]
    prev = jnp.where(i % blocks_per_seq == 0, 0.0, prev)
    row = lax.broadcasted_iota(jnp.int32, h.shape, 0)
    hprev = jnp.where(row == 0, prev, pltpu.roll(h, 1, 0))
    dx = hprev - h

    def mixed(j):
        return (h + dx * mix_ref[j:j + 1, :]).astype(BF16)

    r = _dot(mixed(0), wr_ref[...])
    k = _dot(mixed(2), wk_ref[...])
    v = _dot(mixed(3), wv_ref[...])
    w_lora = _dot(jnp.tanh(_dot(mixed(1), w1_ref[...])), w2_ref[...])
    a_lora = _dot(_dot(mixed(4), a1_ref[...]), a2_ref[...])
    g = _dot(_sigmoid(_dot(mixed(5), g1_ref[...])), g2_ref[...])

    z = -(vec_ref[1:2, :] + w_lora)
    softplus = jnp.maximum(z, 0.0) + jnp.log1p(jnp.exp(-jnp.abs(z)))
    lw = -jnp.exp(-softplus - 0.5)
    a = _sigmoid(vec_ref[2:3, :] + a_lora)

    bd = bd_ref[...]
    kk = k * vec_ref[3:4, :]
    ss = _dot(kk * kk, bd)
    kk = kk / jnp.maximum(jnp.sqrt(ss), 1e-12)
    k = k * (1.0 + (a - 1.0) * vec_ref[4:5, :])
    bonus = _dot(r * k * vec_ref[5:6, :], bd) * v

    r_out[...] = r
    lw_out[...] = lw
    k_out[...] = k
    v_out[...] = v
    kk_out[...] = kk
    a_out[...] = a
    g_out[...] = g.astype(g_out.dtype)
    bonus_out[...] = bonus.astype(bonus_out.dtype)


def _rwkv_pre(x2d, vecs, mix, wr, wk, wv, w1, w2, a1, a2, g1, g2, bd, seq, tm):
    n, d = x2d.shape
    blocks_per_seq = seq // tm
    consts = (vecs, mix, wr, wk, wv, w1, w2, a1, a2, g1, g2, bd)
    out_sd = jax.ShapeDtypeStruct((n, d), F32)
    return pl.pallas_call(
        functools.partial(_rwkv_pre_kernel, blocks_per_seq=blocks_per_seq),
        grid=(n // tm,),
        in_specs=[_row_spec(tm, d),
                  pl.BlockSpec((8, d), lambda i: (jnp.maximum(i * (tm // 8) - 1, 0), 0))]
                 + [_const_spec(c.shape) for c in consts],
        out_specs=[_row_spec(tm, d)] * 8,
        out_shape=[out_sd] * 6 + [jax.ShapeDtypeStruct((n, d), BF16)] * 2,
        compiler_params=pltpu.CompilerParams(dimension_semantics=("arbitrary",),
                                             vmem_limit_bytes=VMEM_LIMIT),
        name="rwkv_pre",
    )(x2d, x2d, *consts)


def _rec_kernel(r_ref, lw_ref, k_ref, v_ref, kk_ref, a_ref, y_ref, s_ref, *, nchunk, npair):
    c2 = 2 * CHUNK
    width = npair * LANES

    @pl.when(pl.program_id(2) == 0)
    def _():
        s_ref[...] = jnp.zeros_like(s_ref)

    ri = lax.broadcasted_iota(jnp.int32, (c2, c2), 0)
    ci = lax.broadcasted_iota(jnp.int32, (c2, c2), 1)
    sub_diag = (ri // SUB) == (ci // SUB)
    eye = (ri == ci).astype(F32)
    gi = lax.broadcasted_iota(jnp.int32, (2 * c2, 2 * c2), 0)
    gj = lax.broadcasted_iota(jnp.int32, (2 * c2, 2 * c2), 1)
    gram_mask = (gj % CHUNK) < (gi % CHUNK) + jnp.where(gi < c2, 0, 1)
    lo_lane = (lax.broadcasted_iota(jnp.int32, (1, width), 1) % LANES) < CHUNK
    tri = (lax.broadcasted_iota(jnp.int32, (CHUNK, CHUNK), 1)
           <= lax.broadcasted_iota(jnp.int32, (CHUNK, CHUNK), 0)).astype(BF16)
    pairs = range(npair)

    def per_pair(x):
        return [x[:, p * LANES:(p + 1) * LANES] for p in pairs]

    def stacked(x):
        return [jnp.concatenate([a, b], axis=0)
                for a, b in zip(per_pair(jnp.where(lo_lane, x, 0.0)), per_pair(jnp.where(lo_lane, 0.0, x)))]

    def chunk(c, carry):
        sl = pl.ds(pl.multiple_of(c * CHUNK, CHUNK), CHUNK)
        lw = lw_ref[sl, :]
        l_hi = lw.astype(BF16)
        l_mid = (lw - l_hi.astype(F32)).astype(BF16)
        l_lo = (lw - l_hi.astype(F32) - l_mid.astype(F32)).astype(BF16)
        cum = _dot(tri, l_hi) + _dot(tri, l_mid) + _dot(tri, l_lo)
        tot = cum[CHUNK - 1:CHUNK, :]
        e_neg = jnp.exp(-cum)
        e_rest = jnp.exp(tot - cum)
        k = k_ref[sl, :]
        kk = kk_ref[sl, :]
        kb = kk * a_ref[sl, :]
        at = stacked(-kk * jnp.exp(cum - lw))
        rt = stacked(r_ref[sl, :] * jnp.exp(cum))
        bt = stacked(kb * e_neg)
        kt = stacked(k * e_neg)
        bw = stacked(kb * e_rest)
        kw = stacked(k * e_rest)
        vm = stacked(v_ref[sl, :])
        decay = per_pair(jnp.exp(tot))

        lhs = [jnp.concatenate([a, r], axis=0).astype(BF16) for a, r in zip(at, rt)]
        gram = [jnp.where(gram_mask, _dot_nt(l, jnp.concatenate([b, kq], axis=0)), 0.0)
                for l, b, kq in zip(lhs, bt, kt)]
        abd = [g[:c2, :c2] for g in gram]
        dg = [jnp.where(sub_diag, x, 0.0) for x in abd]
        off = [x - d for x, d in zip(abd, dg)]
        t = [eye + d for d in dg]
        x = [_dot(d, d) for d in dg]
        for _ in range(2):
            res = [_dot(xi, jnp.concatenate([ti, xi], axis=1)) for xi, ti in zip(x, t)]
            t = [ti + r[:, :c2] for ti, r in zip(t, res)]
            x = [r[:, c2:] for r in res]
        t = [ti + _dot(xi, ti) for xi, ti in zip(x, t)]
        nn = [_dot(ti, o) for ti, o in zip(t, off)]
        res = [_dot(n, jnp.concatenate([n, ti], axis=1)) for n, ti in zip(nn, t)]
        q = [ti + r[:, c2:] for ti, r in zip(t, res)]
        tinv = [qi + _dot(r[:, :c2], qi) for qi, r in zip(q, res)]

        hs = [_dot_nt(l, s_ref[p]) for p, l in zip(pairs, lhs)]
        w = [h[:c2] + _dot(g[:c2, c2:], v) for h, g, v in zip(hs, gram, vm)]
        u = [_dot(ti, wi) for ti, wi in zip(tinv, w)]
        uv = [jnp.concatenate([ui, v], axis=0).astype(BF16) for ui, v in zip(u, vm)]
        ym = [h[c2:] + _dot(g[c2:], x2) for h, g, x2 in zip(hs, gram, uv)]
        y_ref[sl, :] = jnp.concatenate([yi[:CHUNK] + yi[CHUNK:] for yi in ym], axis=1)
        for p in pairs:
            s_ref[p] = s_ref[p] * decay[p] + _dot_tn(uv[p], jnp.concatenate([bw[p], kw[p]], axis=0))
        return carry

    lax.fori_loop(0, nchunk, chunk, 0)


def _rwkv_rec(r, lw, k, v, kk, a, batch, seq, tb, pairs_per_block):
    n, d = r.shape
    width = pairs_per_block * LANES
    nt = seq // tb
    spec = pl.BlockSpec((tb, width), lambda b, p, t: (b * nt + t, p))
    return pl.pallas_call(
        functools.partial(_rec_kernel, nchunk=tb // CHUNK, npair=pairs_per_block),
        grid=(batch, d // width, nt),
        in_specs=[spec] * 6,
        out_specs=spec,
        out_shape=jax.ShapeDtypeStruct((n, d), F32),
        scratch_shapes=[pltpu.VMEM((pairs_per_block, LANES, LANES), F32)],
        compiler_params=pltpu.CompilerParams(
            dimension_semantics=("arbitrary", "arbitrary", "arbitrary"),
            vmem_limit_bytes=VMEM_LIMIT),
        name="rwkv_rec",
    )(r, lw, k, v, kk, a)


def _tail_body(x, pre, p_ref, gn_ref, wo_ref, w1_ref, w2_ref, wup_ref, wgate_ref, out_ref):
    x = x + _dot(pre, wo_ref[...])
    hn = _rms(x, gn_ref[0:1, :]).astype(BF16)
    d_ff = w1_ref.shape[1]
    acc = x
    for c in range(0, d_ff, FF_CHUNK):
        mid = jnp.maximum(_dot(hn, w1_ref[:, c:c + FF_CHUNK]), 0.0)
        acc = acc + _dot(mid * mid, w2_ref[c:c + FF_CHUNK, :])
    gate = _sigmoid(_dot(_rms(acc, gn_ref[1:2, :]), wgate_ref[...]))
    out_ref[...] = acc + _dot(p_ref[...], wup_ref[...]) * gate


def _tail_attn_kernel(x_ref, o_ref, p_ref, *rest):
    _tail_body(x_ref[...], o_ref[...], p_ref, *rest)


def _tail_rwkv_kernel(x_ref, y_ref, bonus_ref, g_ref, p_ref, vec_ref, bd_ref, *rest):
    bd = bd_ref[...]
    y = y_ref[...]
    inv_n = 1.0 / RWKV_HEAD
    yc = y - _dot(y, bd) * inv_n
    var = _dot(yc * yc, bd) * inv_n
    yn = yc * lax.rsqrt(var + GN_EPS) * vec_ref[6:7, :] + vec_ref[7:8, :]
    _tail_body(x_ref[...], (yn + bonus_ref[...]) * g_ref[...], p_ref, *rest)


def _tail(kernel_fn, rows, p2d, extra_consts, gn, wo, w1, w2, wup, wgate, tm):
    n, d = rows[0].shape
    consts = tuple(extra_consts) + (gn, wo, w1, w2, wup, wgate)
    return pl.pallas_call(
        kernel_fn,
        grid=(n // tm,),
        in_specs=[_row_spec(tm, d)] * len(rows) + [_row_spec(tm, p2d.shape[1])]
                 + [_const_spec(c.shape) for c in consts],
        out_specs=_row_spec(tm, d),
        out_shape=jax.ShapeDtypeStruct((n, d), F32),
        compiler_params=pltpu.CompilerParams(dimension_semantics=("arbitrary",),
                                             vmem_limit_bytes=VMEM_LIMIT),
        name="tail",
    )(*rows, p2d, *consts)


def _qkv_kernel(x_ref, gn_ref, wq_ref, wk_ref, wv_ref, bd_ref, q_out, k_out, v_out):
    x = x_ref[...]
    bd = bd_ref[...]
    inv_n = 1.0 / DIFF_HEAD

    def head_rms(t, g):
        ms = _dot(t * t, bd) * inv_n
        return t * lax.rsqrt(ms + NORM_EPS) * g

    hq = _rms(x, gn_ref[0:1, :])
    hk = _rms(x, gn_ref[1:2, :]).astype(BF16)
    q = head_rms(_dot(hq, wq_ref[...]), gn_ref[2:3, :])
    q_out[...] = q.astype(q_out.dtype)
    k_out[...] = head_rms(_dot(hk, wk_ref[...]), gn_ref[3:4, :]).astype(k_out.dtype)
    v_out[0] = _dot(hk, wv_ref[...]).T.astype(v_out.dtype)


def _qkv(x2d, gn, wq, wk, wv, bd, tm):
    n, d = x2d.shape
    consts = (gn, wq, wk, wv, bd)
    out_sd = jax.ShapeDtypeStruct((n, d), BF16)
    return pl.pallas_call(
        _qkv_kernel,
        grid=(n // tm,),
        in_specs=[_row_spec(tm, d)] + [_const_spec(c.shape) for c in consts],
        out_specs=[_row_spec(tm, d)] * 2 + [pl.BlockSpec((1, d, tm), lambda i: (i, 0, 0))],
        out_shape=[out_sd] * 2 + [jax.ShapeDtypeStruct((n // tm, d, tm), BF16)],
        compiler_params=pltpu.CompilerParams(dimension_semantics=("arbitrary",),
                                             vmem_limit_bytes=VMEM_LIMIT),
        name="qkv",
    )(x2d, *consts)


def _bucket_tiles(tb):
    i = np.arange(tb, dtype=np.int64)[None, :]
    j = np.arange(tb, dtype=np.int64)[:, None]

    def bucket(rel):
        n = np.maximum(rel, 0)
        max_exact = N_BUCKETS // 2
        nf = np.maximum(n, 1).astype(np.float32)
        large = max_exact + (np.log(nf / np.float32(max_exact)) / np.float32(math.log(MAX_DIST / max_exact))
                             * np.float32(N_BUCKETS - max_exact)).astype(np.int32)
        large = np.minimum(large, N_BUCKETS - 1)
        return np.where(n < max_exact, n, large).astype(np.int32)

    diag = np.where(i - j >= 0, bucket(i - j), -1)
    near = bucket(tb + i - j)
    tiles = np.stack([diag, near]).astype(np.int32)
    return np.concatenate([tiles, tiles], axis=2)


def _bias_kernel(tab_ref, bucket_ref, out_ref):
    h = pl.program_id(0)
    b = bucket_ref[...]
    far = tab_ref[N_BUCKETS - 1, h]
    acc = jnp.where(b < 0, NEG_BIG, 0.0)
    for n in range(N_BUCKETS - 1):
        acc = jnp.where(b == n, (tab_ref[n, h] - far) * LOG2E, acc)
    out_ref[0] = acc


def _bias_tiles(rel_bias, tb):
    buckets = jnp.asarray(_bucket_tiles(tb))
    nh = rel_bias.shape[1]
    return pl.pallas_call(
        _bias_kernel,
        grid=(nh,),
        in_specs=[pl.BlockSpec(memory_space=pltpu.SMEM),
                  pl.BlockSpec((2, tb, 2 * tb), lambda h: (0, 0, 0))],
        out_specs=pl.BlockSpec((1, 2, tb, 2 * tb), lambda h: (h, 0, 0, 0)),
        out_shape=jax.ShapeDtypeStruct((nh, 2, tb, 2 * tb), F32),
        compiler_params=pltpu.CompilerParams(dimension_semantics=("arbitrary",)),
        name="bias",
    )(rel_bias, buckets)


def _attn_kernel(q_ref, k_ref, vt_ref, bias_ref, lam_ref, g_ref, o_ref, m_ref, l_ref, acc_ref,
                 *, tb, nhead, out_scale, lambda_init, bounded):
    qi = pl.program_id(2)
    heads = range(nhead)
    lane = lax.broadcasted_iota(jnp.int32, (1, LANES), 1)

    def head_lanes(h):
        return slice(h * LANES, (h + 1) * LANES)

    def stacked_q(h):
        q = q_ref[:, head_lanes(h)]
        zero = jnp.zeros_like(q)
        return jnp.concatenate([jnp.where(lane < DIFF_HEAD, q, zero),
                                jnp.where(lane < DIFF_HEAD, zero, q)], axis=0)

    qs = [stacked_q(h) for h in heads]
    m_ref[...] = jnp.full_like(m_ref, NEG_BIG)
    l_ref[...] = jnp.zeros_like(l_ref)
    acc_ref[...] = jnp.zeros_like(acc_ref)

    def advance(j, biases):
        nblk = len(biases)
        rows = pl.ds(pl.multiple_of(j * tb, tb), nblk * tb)
        logits = lambda h: _dot_nt(k_ref[rows, head_lanes(h)], qs[h])
        ahead = min(2, nhead)
        st = {h: logits(h) for h in range(ahead)}
        spans = []
        for i, b in enumerate(biases):
            if b is None and spans and spans[-1][2] is None:
                spans[-1] = (spans[-1][0], i + 1, None)
            else:
                spans.append((i, i + 1, b))
        for h in heads:
            if h + ahead < nhead:
                st[h + ahead] = logits(h + ahead)
            s = st.pop(h)
            parts = [s[i * tb:e * tb] if b is None else s[i * tb:e * tb] + bias_ref[h, b] for i, e, b in spans]
            if bounded:
                ps = [jnp.exp2(x) for x in parts]
                l_new = l_ref[h]
            else:
                m_old = m_ref[h]
                m_new = m_old
                for x in parts:
                    m_new = jnp.maximum(m_new, jnp.max(x, axis=0, keepdims=True))
                alpha = jnp.exp2(m_old - m_new)
                ps = [jnp.exp2(x - m_new) for x in parts]
                l_new = alpha * l_ref[h]
                m_ref[h] = m_new
            for p in ps:
                l_new = l_new + jnp.sum(p, axis=0, keepdims=True)
            l_ref[h] = l_new
            pv = None
            for (i, e, _), p in zip(spans, ps):
                p = p.astype(BF16)
                for b in range(i, e):
                    term = _dot(vt_ref[j + b, head_lanes(h), :], p[(b - i) * tb:(b - i + 1) * tb])
                    pv = term if pv is None else pv + term
            acc_ref[h] = acc_ref[h] + pv if bounded else alpha * acc_ref[h] + pv

    def far_pair(t, carry):
        advance(2 * t, (None, None))
        return carry

    nfar = jnp.maximum(qi - 1, 0)
    lax.fori_loop(0, nfar // 2, far_pair, 0)

    @pl.when(qi == 0)
    def _():
        advance(0, (0,))

    @pl.when((qi >= 1) & (nfar % 2 == 0))
    def _():
        advance(qi - 1, (1, 0))

    @pl.when(nfar % 2 == 1)
    def _():
        advance(qi - 2, (None, 1, 0))

    lam = lam_ref[...]
    lam_full = (jnp.exp(jnp.sum(lam[0:1] * lam[1:2], axis=1, keepdims=True))
                - jnp.exp(jnp.sum(lam[2:3] * lam[3:4], axis=1, keepdims=True)) + lambda_init)
    for h in heads:
        o = acc_ref[h] * (1.0 / l_ref[h])
        o = o[:, :tb] - lam_full * o[:, tb:]
        o = o * lax.rsqrt(jnp.mean(o * o, axis=0, keepdims=True) + NORM_EPS)
        o_ref[:, head_lanes(h)] = (o.T * (g_ref[...] * out_scale)).astype(o_ref.dtype)


def _attn(q, k, vt, bias, lam, subln_g, batch, seq, tb, nhead, layer_idx, bounded):
    n, d = q.shape
    ngroup = d // (LANES * nhead)
    width = LANES * nhead
    nq = seq // tb
    lambda_init = 0.8 - 0.6 * math.exp(-0.3 * layer_idx)
    q_spec = pl.BlockSpec((tb, width), lambda b, h, i: (b * nq + i, h))
    return pl.pallas_call(
        functools.partial(_attn_kernel, tb=tb, nhead=nhead, out_scale=1.0 - lambda_init,
                          lambda_init=lambda_init, bounded=bounded),
        grid=(batch, ngroup, nq),
        in_specs=[q_spec,
                  pl.BlockSpec((seq, width), lambda b, h, i: (b, h)),
                  pl.BlockSpec((nq, width, tb), lambda b, h, i: (b, h, 0)),
                  pl.BlockSpec((nhead, 2, tb, 2 * tb), lambda b, h, i: (h, 0, 0, 0)),
                  pl.BlockSpec(lam.shape, lambda b, h, i: (0, 0)),
                  pl.BlockSpec(subln_g.shape, lambda b, h, i: (0, 0))],
        out_specs=q_spec,
        out_shape=jax.ShapeDtypeStruct((n, d), BF16),
        scratch_shapes=[pltpu.VMEM((nhead, 1, 2 * tb), F32), pltpu.VMEM((nhead, 1, 2 * tb), F32),
                        pltpu.VMEM((nhead, LANES, 2 * tb), F32)],
        compiler_params=pltpu.CompilerParams(
            dimension_semantics=("arbitrary", "arbitrary", "arbitrary"),
            vmem_limit_bytes=VMEM_LIMIT),
        name="attn",
    )(q, k, vt, bias, lam, subln_g)


def kernel(x, p, norm_g, mlp_w1, mlp_w2, ple_w_up, ple_w_gate, rwkv_mix, rwkv_w_rkvo, rwkv_w0, rwkv_w1, rwkv_w2, rwkv_a0, rwkv_a1, rwkv_a2, rwkv_g1, rwkv_g2, rwkv_k_k, rwkv_k_a, rwkv_r_k, rwkv_ln_w, rwkv_ln_b, kv_norm_g, w_k_shared, w_v_shared, k_norm_g, diff_w_q, diff_q_norm_g, diff_lam, diff_subln_g, diff_w_o, rel_bias):
    batch, seq, d = x.shape
    assert d == D_MODEL and norm_g.shape[0] == 2
    n = batch * seq
    tm = min(256, seq)
    tb_rec = min(256, seq)
    tb_attn = min(256, seq)
    assert seq % tm == 0 and seq % tb_rec == 0 and seq % tb_attn == 0 and tb_attn >= MAX_DIST

    bf = lambda w: w.astype(BF16)
    x2d = x.reshape(n, d)
    p2d = p.reshape(p.shape[0], n, p.shape[-1])
    head_id = jnp.arange(d, dtype=jnp.int32) // RWKV_HEAD
    bd = (head_id[:, None] == head_id[None, :]).astype(BF16)

    vecs = jnp.stack([norm_g[0, 0], rwkv_w0[0], rwkv_a0[0], rwkv_k_k[0], rwkv_k_a[0],
                      rwkv_r_k[0].reshape(d), rwkv_ln_w[0], rwkv_ln_b[0]])
    r, lw, k, v, kk, a, g, bonus = _rwkv_pre(
        x2d, vecs, rwkv_mix[0], bf(rwkv_w_rkvo[0, 0]), bf(rwkv_w_rkvo[0, 1]), bf(rwkv_w_rkvo[0, 2]),
        bf(rwkv_w1[0]), bf(rwkv_w2[0]), bf(rwkv_a1[0]), bf(rwkv_a2[0]), bf(rwkv_g1[0]), bf(rwkv_g2[0]),
        bd, seq, tm)
    y = _rwkv_rec(r, lw, k, v, kk, a, batch, seq, tb_rec, d // LANES)
    x2d = _tail(_tail_rwkv_kernel, (x2d, y, bonus, g), p2d[0], (vecs, bd), norm_g[0, 1:3],
                bf(rwkv_w_rkvo[0, 3]), bf(mlp_w1[0]), bf(mlp_w2[0]), bf(ple_w_up[0]), bf(ple_w_gate[0]), tm)

    reps = d // DIFF_HEAD
    gn = jnp.stack([norm_g[1, 0], kv_norm_g,
                    jnp.tile(diff_q_norm_g[0], reps) * (DIFF_HEAD ** -0.5 * LOG2E), jnp.tile(k_norm_g, reps)])
    q, kq, vq = _qkv(x2d, gn, bf(diff_w_q[0]), bf(w_k_shared), bf(w_v_shared), bd, tb_attn)
    bias = _bias_tiles(rel_bias, tb_attn)
    logit_bound = 1.02 * LOG2E * (DIFF_HEAD ** 0.5 * jnp.max(jnp.abs(diff_q_norm_g[0] * k_norm_g))
                                  + jnp.max(jnp.abs(rel_bias - rel_bias[-1:])))
    attn = functools.partial(_attn, q, kq, vq, bias, diff_lam[0], diff_subln_g[0].reshape(1, LANES),
                             batch, seq, tb_attn, 4, 1)
    o = lax.cond(logit_bound <= LOGIT_BOUND, lambda: attn(True), lambda: attn(False))
    x2d = _tail(_tail_attn_kernel, (x2d, o), p2d[1], (), norm_g[1, 1:3],
                bf(diff_w_o[0]), bf(mlp_w1[1]), bf(mlp_w2[1]), bf(ple_w_up[1]), bf(ple_w_gate[1]), tm)
    return x2d.reshape(batch, seq, d)
```

```python
import functools
import math

import numpy as np
import jax
import jax.numpy as jnp
from jax import lax
from jax.experimental import pallas as pl
from jax.experimental.pallas import tpu as pltpu

F32 = jnp.float32
BF16 = jnp.bfloat16

D_MODEL = 1024
RWKV_HEAD = 64
DIFF_HEAD = 64
DIFF_HEADS = 8
LANES = 128
CHUNK = 64
SUB = 16
FF_CHUNK = 1024
NORM_EPS = 1e-6
GN_EPS = 64e-5
N_BUCKETS = 32
MAX_DIST = 128
NEG_BIG = -1e30
LOG2E = math.log2(math.e)
LOGIT_BOUND = 60.0
VMEM_LIMIT = 56 * 1024 * 1024


def _dot(a, b):
    return jnp.dot(a.astype(BF16), b.astype(BF16), preferred_element_type=F32)


def _dot_nt(a, b):
    return lax.dot_general(a.astype(BF16), b.astype(BF16), (((1,), (1,)), ((), ())),
                           preferred_element_type=F32)


def _dot_tn(a, b):
    return lax.dot_general(a.astype(BF16), b.astype(BF16), (((0,), (0,)), ((), ())),
                           preferred_element_type=F32)


def _rms(x, g):
    return x * lax.rsqrt(jnp.mean(x * x, axis=-1, keepdims=True) + NORM_EPS) * g


def _sigmoid(x):
    return 1.0 / (1.0 + jnp.exp(-x))


def _const_spec(shape):
    nd = len(shape)
    return pl.BlockSpec(shape, lambda *_: (0,) * nd, pipeline_mode=pl.Buffered(1))


def _row_spec(tm, width):
    return pl.BlockSpec((tm, width), lambda i: (i, 0))


def _rwkv_pre_kernel(x_ref, xp_ref, vec_ref, mix_ref, wr_ref, wk_ref, wv_ref,
                     w1_ref, w2_ref, a1_ref, a2_ref, g1_ref, g2_ref, bd_ref,
                     r_out, lw_out, k_out, v_out, kk_out, a_out, g_out, bonus_out, *, first_block):
    g0 = vec_ref[0:1, :]
    h = _rms(x_ref[...], g0)
    prev = _rms(xp_ref[...], g0)[7:8, :]
    prev = jnp.where(first_block, 0.0, prev)
    row = lax.broadcasted_iota(jnp.int32, h.shape, 0)
    hprev = jnp.where(row == 0, prev, pltpu.roll(h, 1, 0))
    dx = hprev - h

    def mixed(j):
        return (h + dx * mix_ref[j:j + 1, :]).astype(BF16)

    r = _dot(mixed(0), wr_ref[...])
    k = _dot(mixed(2), wk_ref[...])
    v = _dot(mixed(3), wv_ref[...])
    w_lora = _dot(jnp.tanh(_dot(mixed(1), w1_ref[...])), w2_ref[...])
    a_lora = _dot(_dot(mixed(4), a1_ref[...]), a2_ref[...])
    g = _dot(_sigmoid(_dot(mixed(5), g1_ref[...])), g2_ref[...])

    z = -(vec_ref[1:2, :] + w_lora)
    softplus = jnp.maximum(z, 0.0) + jnp.log1p(jnp.exp(-jnp.abs(z)))
    lw = -jnp.exp(-softplus - 0.5)
    a = _sigmoid(vec_ref[2:3, :] + a_lora)

    bd = bd_ref[...]
    kk = k * vec_ref[3:4, :]
    ss = _dot(kk * kk, bd)
    kk = kk / jnp.maximum(jnp.sqrt(ss), 1e-12)
    k = k * (1.0 + (a - 1.0) * vec_ref[4:5, :])
    bonus = _dot(r * k * vec_ref[5:6, :], bd) * v

    r_out[...] = r
    lw_out[...] = lw
    k_out[...] = k
    v_out[...] = v
    kk_out[...] = kk
    a_out[...] = a
    g_out[...] = g.astype(g_out.dtype)
    bonus_out[...] = bonus.astype(bonus_out.dtype)


def _rec_kernel(r_ref, lw_ref, k_ref, v_ref, kk_ref, a_ref, y_ref, s_ref, *, nchunk, npair, first_block):
    c2 = 2 * CHUNK
    width = npair * LANES

    @pl.when(first_block)
    def _():
        s_ref[...] = jnp.zeros_like(s_ref)

    ri = lax.broadcasted_iota(jnp.int32, (c2, c2), 0)
    ci = lax.broadcasted_iota(jnp.int32, (c2, c2), 1)
    sub_diag = (ri // SUB) == (ci // SUB)
    eye = (ri == ci).astype(F32)
    gi = lax.broadcasted_iota(jnp.int32, (2 * c2, 2 * c2), 0)
    gj = lax.broadcasted_iota(jnp.int32, (2 * c2, 2 * c2), 1)
    gram_mask = (gj % CHUNK) < (gi % CHUNK) + jnp.where(gi < c2, 0, 1)
    lo_lane = (lax.broadcasted_iota(jnp.int32, (1, width), 1) % LANES) < CHUNK
    tri = (lax.broadcasted_iota(jnp.int32, (CHUNK, CHUNK), 1)
           <= lax.broadcasted_iota(jnp.int32, (CHUNK, CHUNK), 0)).astype(BF16)
    pairs = range(npair)

    def per_pair(x):
        return [x[:, p * LANES:(p + 1) * LANES] for p in pairs]

    def stacked(x):
        return [jnp.concatenate([a, b], axis=0)
                for a, b in zip(per_pair(jnp.where(lo_lane, x, 0.0)), per_pair(jnp.where(lo_lane, 0.0, x)))]

    def chunk(c, carry):
        sl = pl.ds(pl.multiple_of(c * CHUNK, CHUNK), CHUNK)
        lw = lw_ref[sl, :]
        l_hi = lw.astype(BF16)
        l_mid = (lw - l_hi.astype(F32)).astype(BF16)
        l_lo = (lw - l_hi.astype(F32) - l_mid.astype(F32)).astype(BF16)
        cum = _dot(tri, l_hi) + _dot(tri, l_mid) + _dot(tri, l_lo)
        tot = cum[CHUNK - 1:CHUNK, :]
        e_neg = jnp.exp(-cum)
        e_rest = jnp.exp(tot - cum)
        k = k_ref[sl, :]
        kk = kk_ref[sl, :]
        kb = kk * a_ref[sl, :]
        at = stacked(-kk * jnp.exp(cum - lw))
        rt = stacked(r_ref[sl, :] * jnp.exp(cum))
        bt = stacked(kb * e_neg)
        kt = stacked(k * e_neg)
        bw = stacked(kb * e_rest)
        kw = stacked(k * e_rest)
        vm = stacked(v_ref[sl, :])
        decay = per_pair(jnp.exp(tot))

        lhs = [jnp.concatenate([a, r], axis=0).astype(BF16) for a, r in zip(at, rt)]
        gram = [jnp.where(gram_mask, _dot_nt(l, jnp.concatenate([b, kq], axis=0)), 0.0)
                for l, b, kq in zip(lhs, bt, kt)]
        abd = [g[:c2, :c2] for g in gram]
        dg = [jnp.where(sub_diag, x, 0.0) for x in abd]
        off = [x - d for x, d in zip(abd, dg)]
        t = [eye + d for d in dg]
        x = [_dot(d, d) for d in dg]
        for _ in range(2):
            res = [_dot(xi, jnp.concatenate([ti, xi], axis=1)) for xi, ti in zip(x, t)]
            t = [ti + r[:, :c2] for ti, r in zip(t, res)]
            x = [r[:, c2:] for r in res]
        t = [ti + _dot(xi, ti) for xi, ti in zip(x, t)]
        nn = [_dot(ti, o) for ti, o in zip(t, off)]
        res = [_dot(n, jnp.concatenate([n, ti], axis=1)) for n, ti in zip(nn, t)]
        q = [ti + r[:, c2:] for ti, r in zip(t, res)]
        tinv = [qi + _dot(r[:, :c2], qi) for qi, r in zip(q, res)]

        hs = [_dot_nt(l, s_ref[p]) for p, l in zip(pairs, lhs)]
        w = [h[:c2] + _dot(g[:c2, c2:], v) for h, g, v in zip(hs, gram, vm)]
        u = [_dot(ti, wi) for ti, wi in zip(tinv, w)]
        uv = [jnp.concatenate([ui, v], axis=0).astype(BF16) for ui, v in zip(u, vm)]
        ym = [h[c2:] + _dot(g[c2:], x2) for h, g, x2 in zip(hs, gram, uv)]
        y_ref[sl, :] = jnp.concatenate([yi[:CHUNK] + yi[CHUNK:] for yi in ym], axis=1)
        for p in pairs:
            s_ref[p] = s_ref[p] * decay[p] + _dot_tn(uv[p], jnp.concatenate([bw[p], kw[p]], axis=0))
        return carry

    lax.fori_loop(0, nchunk, chunk, 0)


def _rwkv_kernel(*refs, nchunk, npair):
    ins, (y_out, g_out, bonus_out), (r_s, lw_s, k_s, v_s, kk_s, a_s, s_ref) = refs[:14], refs[14:17], refs[17:]
    first_block = pl.program_id(1) == 0
    _rwkv_pre_kernel(*ins, r_s, lw_s, k_s, v_s, kk_s, a_s, g_out, bonus_out, first_block=first_block)
    _rec_kernel(r_s, lw_s, k_s, v_s, kk_s, a_s, y_out, s_ref, nchunk=nchunk, npair=npair,
                first_block=first_block)


def _rwkv(x2d, vecs, mix, wr, wk, wv, w1, w2, a1, a2, g1, g2, bd, batch, seq, tb):
    n, d = x2d.shape
    nt = seq // tb
    npair = d // LANES
    consts = (vecs, mix, wr, wk, wv, w1, w2, a1, a2, g1, g2, bd)
    row_spec = pl.BlockSpec((tb, d), lambda b, t: (b * nt + t, 0))
    prev_spec = pl.BlockSpec((8, d), lambda b, t: (jnp.maximum((b * nt + t) * (tb // 8) - 1, 0), 0))
    return pl.pallas_call(
        functools.partial(_rwkv_kernel, nchunk=tb // CHUNK, npair=npair),
        grid=(batch, nt),
        in_specs=[row_spec, prev_spec] + [_const_spec(c.shape) for c in consts],
        out_specs=[row_spec] * 3,
        out_shape=[jax.ShapeDtypeStruct((n, d), F32)] + [jax.ShapeDtypeStruct((n, d), BF16)] * 2,
        scratch_shapes=[pltpu.VMEM((tb, d), F32)] * 6 + [pltpu.VMEM((npair, LANES, LANES), F32)],
        compiler_params=pltpu.CompilerParams(dimension_semantics=("arbitrary", "arbitrary"),
                                             vmem_limit_bytes=VMEM_LIMIT),
        name="rwkv",
    )(x2d, x2d, *consts)


def _tail_body(x, pre, p_ref, gn_ref, wo_ref, w1_ref, w2_ref, wup_ref, wgate_ref, out_ref):
    x = x + _dot(pre, wo_ref[...])
    hn = _rms(x, gn_ref[0:1, :]).astype(BF16)
    d_ff = w1_ref.shape[1]
    acc = x
    for c in range(0, d_ff, FF_CHUNK):
        mid = jnp.maximum(_dot(hn, w1_ref[:, c:c + FF_CHUNK]), 0.0)
        acc = acc + _dot(mid * mid, w2_ref[c:c + FF_CHUNK, :])
    gate = _sigmoid(_dot(_rms(acc, gn_ref[1:2, :]), wgate_ref[...]))
    out_ref[...] = acc + _dot(p_ref[...], wup_ref[...]) * gate


def _tail_attn_kernel(x_ref, o_ref, p_ref, *rest):
    _tail_body(x_ref[...], o_ref[...], p_ref, *rest)


def _tail_rwkv_kernel(x_ref, y_ref, bonus_ref, g_ref, p_ref, vec_ref, bd_ref, *rest):
    bd = bd_ref[...]
    y = y_ref[...]
    inv_n = 1.0 / RWKV_HEAD
    yc = y - _dot(y, bd) * inv_n
    var = _dot(yc * yc, bd) * inv_n
    yn = yc * lax.rsqrt(var + GN_EPS) * vec_ref[6:7, :] + vec_ref[7:8, :]
    _tail_body(x_ref[...], (yn + bonus_ref[...]) * g_ref[...], p_ref, *rest)


def _tail(kernel_fn, rows, p2d, extra_consts, gn, wo, w1, w2, wup, wgate, tm):
    n, d = rows[0].shape
    consts = tuple(extra_consts) + (gn, wo, w1, w2, wup, wgate)
    return pl.pallas_call(
        kernel_fn,
        grid=(n // tm,),
        in_specs=[_row_spec(tm, d)] * len(rows) + [_row_spec(tm, p2d.shape[1])]
                 + [_const_spec(c.shape) for c in consts],
        out_specs=_row_spec(tm, d),
        out_shape=jax.ShapeDtypeStruct((n, d), F32),
        compiler_params=pltpu.CompilerParams(dimension_semantics=("arbitrary",),
                                             vmem_limit_bytes=VMEM_LIMIT),
        name="tail",
    )(*rows, p2d, *consts)


def _qkv_kernel(x_ref, gn_ref, wq_ref, wk_ref, wv_ref, bd_ref, q_out, k_out, v_out):
    x = x_ref[...]
    bd = bd_ref[...]
    inv_n = 1.0 / DIFF_HEAD

    def head_rms(t, g):
        ms = _dot(t * t, bd) * inv_n
        return t * lax.rsqrt(ms + NORM_EPS) * g

    hq = _rms(x, gn_ref[0:1, :])
    hk = _rms(x, gn_ref[1:2, :]).astype(BF16)
    q = head_rms(_dot(hq, wq_ref[...]), gn_ref[2:3, :])
    q_out[...] = q.astype(q_out.dtype)
    k_out[...] = head_rms(_dot(hk, wk_ref[...]), gn_ref[3:4, :]).astype(k_out.dtype)
    v_out[0] = _dot(hk, wv_ref[...]).T.astype(v_out.dtype)


def _qkv(x2d, gn, wq, wk, wv, bd, tm):
    n, d = x2d.shape
    consts = (gn, wq, wk, wv, bd)
    out_sd = jax.ShapeDtypeStruct((n, d), BF16)
    return pl.pallas_call(
        _qkv_kernel,
        grid=(n // tm,),
        in_specs=[_row_spec(tm, d)] + [_const_spec(c.shape) for c in consts],
        out_specs=[_row_spec(tm, d)] * 2 + [pl.BlockSpec((1, d, tm), lambda i: (i, 0, 0))],
        out_shape=[out_sd] * 2 + [jax.ShapeDtypeStruct((n // tm, d, tm), BF16)],
        compiler_params=pltpu.CompilerParams(dimension_semantics=("arbitrary",),
                                             vmem_limit_bytes=VMEM_LIMIT),
        name="qkv",
    )(x2d, *consts)


def _bucket_tiles(tb):
    i = np.arange(tb, dtype=np.int64)[None, :]
    j = np.arange(tb, dtype=np.int64)[:, None]

    def bucket(rel):
        n = np.maximum(rel, 0)
        max_exact = N_BUCKETS // 2
        nf = np.maximum(n, 1).astype(np.float32)
        large = max_exact + (np.log(nf / np.float32(max_exact)) / np.float32(math.log(MAX_DIST / max_exact))
                             * np.float32(N_BUCKETS - max_exact)).astype(np.int32)
        large = np.minimum(large, N_BUCKETS - 1)
        return np.where(n < max_exact, n, large).astype(np.int32)

    diag = np.where(i - j >= 0, bucket(i - j), -1)
    near = bucket(tb + i - j)
    tiles = np.stack([diag, near]).astype(np.int32)
    return np.concatenate([tiles, tiles], axis=2)


def _bias_kernel(tab_ref, bucket_ref, out_ref):
    h = pl.program_id(0)
    b = bucket_ref[...]
    far = tab_ref[N_BUCKETS - 1, h]
    acc = jnp.where(b < 0, NEG_BIG, 0.0)
    for n in range(N_BUCKETS - 1):
        acc = jnp.where(b == n, (tab_ref[n, h] - far) * LOG2E, acc)
    out_ref[0] = acc


def _bias_tiles(rel_bias, tb):
    buckets = jnp.asarray(_bucket_tiles(tb))
    nh = rel_bias.shape[1]
    return pl.pallas_call(
        _bias_kernel,
        grid=(nh,),
        in_specs=[pl.BlockSpec(memory_space=pltpu.SMEM),
                  pl.BlockSpec((2, tb, 2 * tb), lambda h: (0, 0, 0))],
        out_specs=pl.BlockSpec((1, 2, tb, 2 * tb), lambda h: (h, 0, 0, 0)),
        out_shape=jax.ShapeDtypeStruct((nh, 2, tb, 2 * tb), F32),
        compiler_params=pltpu.CompilerParams(dimension_semantics=("arbitrary",)),
        name="bias",
    )(rel_bias, buckets)


def _attn_kernel(q_ref, k_ref, vt_ref, bias_ref, lam_ref, g_ref, o_ref, m_ref, l_ref, acc_ref,
                 *, tb, nhead, out_scale, lambda_init, bounded):
    qi = pl.program_id(2)
    heads = range(nhead)
    lane = lax.broadcasted_iota(jnp.int32, (1, LANES), 1)

    def head_lanes(h):
        return slice(h * LANES, (h + 1) * LANES)

    def stacked_q(h):
        q = q_ref[:, head_lanes(h)]
        zero = jnp.zeros_like(q)
        return jnp.concatenate([jnp.where(lane < DIFF_HEAD, q, zero),
                                jnp.where(lane < DIFF_HEAD, zero, q)], axis=0)

    qs = [stacked_q(h) for h in heads]
    m_ref[...] = jnp.full_like(m_ref, NEG_BIG)
    l_ref[...] = jnp.zeros_like(l_ref)
    acc_ref[...] = jnp.zeros_like(acc_ref)

    def advance(j, biases):
        nblk = len(biases)
        rows = pl.ds(pl.multiple_of(j * tb, tb), nblk * tb)
        logits = lambda h: _dot_nt(k_ref[rows, head_lanes(h)], qs[h])
        ahead = min(2, nhead)
        st = {h: logits(h) for h in range(ahead)}
        spans = []
        for i, b in enumerate(biases):
            if b is None and spans and spans[-1][2] is None:
                spans[-1] = (spans[-1][0], i + 1, None)
            else:
                spans.append((i, i + 1, b))
        for h in heads:
            if h + ahead < nhead:
                st[h + ahead] = logits(h + ahead)
            s = st.pop(h)
            parts = [s[i * tb:e * tb] if b is None else s[i * tb:e * tb] + bias_ref[h, b] for i, e, b in spans]
            if bounded:
                ps = [jnp.exp2(x) for x in parts]
                l_new = l_ref[h]
            else:
                m_old = m_ref[h]
                m_new = m_old
                for x in parts:
                    m_new = jnp.maximum(m_new, jnp.max(x, axis=0, keepdims=True))
                alpha = jnp.exp2(m_old - m_new)
                ps = [jnp.exp2(x - m_new) for x in parts]
                l_new = alpha * l_ref[h]
                m_ref[h] = m_new
            for p in ps:
                l_new = l_new + jnp.sum(p, axis=0, keepdims=True)
            l_ref[h] = l_new
            pv = None
            for (i, e, _), p in zip(spans, ps):
                p = p.astype(BF16)
                for b in range(i, e):
                    term = _dot(vt_ref[j + b, head_lanes(h), :], p[(b - i) * tb:(b - i + 1) * tb])
                    pv = term if pv is None else pv + term
            acc_ref[h] = acc_ref[h] + pv if bounded else alpha * acc_ref[h] + pv

    def far_pair(t, carry):
        advance(2 * t, (None, None))
        return carry

    nfar = jnp.maximum(qi - 1, 0)
    lax.fori_loop(0, nfar // 2, far_pair, 0)

    @pl.when(qi == 0)
    def _():
        advance(0, (0,))

    @pl.when((qi >= 1) & (nfar % 2 == 0))
    def _():
        advance(qi - 1, (1, 0))

    @pl.when(nfar % 2 == 1)
    def _():
        advance(qi - 2, (None, 1, 0))

    lam = lam_ref[...]
    lam_full = (jnp.exp(jnp.sum(lam[0:1] * lam[1:2], axis=1, keepdims=True))
                - jnp.exp(jnp.sum(lam[2:3] * lam[3:4], axis=1, keepdims=True)) + lambda_init)
    for h in heads:
        o = acc_ref[h] * (1.0 / l_ref[h])
        o = o[:, :tb] - lam_full * o[:, tb:]
        o = o * lax.rsqrt(jnp.mean(o * o, axis=0, keepdims=True) + NORM_EPS)
        o_ref[:, head_lanes(h)] = (o.T * (g_ref[...] * out_scale)).astype(o_ref.dtype)


def _attn(q, k, vt, bias, lam, subln_g, batch, seq, tb, nhead, layer_idx, bounded):
    n, d = q.shape
    ngroup = d // (LANES * nhead)
    width = LANES * nhead
    nq = seq // tb
    lambda_init = 0.8 - 0.6 * math.exp(-0.3 * layer_idx)
    q_spec = pl.BlockSpec((tb, width), lambda b, h, i: (b * nq + i, h))
    return pl.pallas_call(
        functools.partial(_attn_kernel, tb=tb, nhead=nhead, out_scale=1.0 - lambda_init,
                          lambda_init=lambda_init, bounded=bounded),
        grid=(batch, ngroup, nq),
        in_specs=[q_spec,
                  pl.BlockSpec((seq, width), lambda b, h, i: (b, h)),
                  pl.BlockSpec((nq, width, tb), lambda b, h, i: (b, h, 0)),
                  pl.BlockSpec((nhead, 2, tb, 2 * tb), lambda b, h, i: (h, 0, 0, 0)),
                  pl.BlockSpec(lam.shape, lambda b, h, i: (0, 0)),
                  pl.BlockSpec(subln_g.shape, lambda b, h, i: (0, 0))],
        out_specs=q_spec,
        out_shape=jax.ShapeDtypeStruct((n, d), BF16),
        scratch_shapes=[pltpu.VMEM((nhead, 1, 2 * tb), F32), pltpu.VMEM((nhead, 1, 2 * tb), F32),
                        pltpu.VMEM((nhead, LANES, 2 * tb), F32)],
        compiler_params=pltpu.CompilerParams(
            dimension_semantics=("arbitrary", "arbitrary", "arbitrary"),
            vmem_limit_bytes=VMEM_LIMIT),
        name="attn",
    )(q, k, vt, bias, lam, subln_g)


def kernel(x, p, norm_g, mlp_w1, mlp_w2, ple_w_up, ple_w_gate, rwkv_mix, rwkv_w_rkvo, rwkv_w0, rwkv_w1, rwkv_w2, rwkv_a0, rwkv_a1, rwkv_a2, rwkv_g1, rwkv_g2, rwkv_k_k, rwkv_k_a, rwkv_r_k, rwkv_ln_w, rwkv_ln_b, kv_norm_g, w_k_shared, w_v_shared, k_norm_g, diff_w_q, diff_q_norm_g, diff_lam, diff_subln_g, diff_w_o, rel_bias):
    batch, seq, d = x.shape
    assert d == D_MODEL and norm_g.shape[0] == 2
    n = batch * seq
    tm = min(256, seq)
    tb_rec = min(256, seq)
    tb_attn = min(256, seq)
    assert seq % tm == 0 and seq % tb_rec == 0 and seq % tb_attn == 0 and tb_attn >= MAX_DIST

    bf = lambda w: w.astype(BF16)
    x2d = x.reshape(n, d)
    p2d = p.reshape(p.shape[0], n, p.shape[-1])
    head_id = jnp.arange(d, dtype=jnp.int32) // RWKV_HEAD
    bd = (head_id[:, None] == head_id[None, :]).astype(BF16)

    vecs = jnp.stack([norm_g[0, 0], rwkv_w0[0], rwkv_a0[0], rwkv_k_k[0], rwkv_k_a[0],
                      rwkv_r_k[0].reshape(d), rwkv_ln_w[0], rwkv_ln_b[0]])
    y, g, bonus = _rwkv(
        x2d, vecs, rwkv_mix[0], bf(rwkv_w_rkvo[0, 0]), bf(rwkv_w_rkvo[0, 1]), bf(rwkv_w_rkvo[0, 2]),
        bf(rwkv_w1[0]), bf(rwkv_w2[0]), bf(rwkv_a1[0]), bf(rwkv_a2[0]), bf(rwkv_g1[0]), bf(rwkv_g2[0]),
        bd, batch, seq, tb_rec)
    x2d = _tail(_tail_rwkv_kernel, (x2d, y, bonus, g), p2d[0], (vecs, bd), norm_g[0, 1:3],
                bf(rwkv_w_rkvo[0, 3]), bf(mlp_w1[0]), bf(mlp_w2[0]), bf(ple_w_up[0]), bf(ple_w_gate[0]), tm)

    reps = d // DIFF_HEAD
    gn = jnp.stack([norm_g[1, 0], kv_norm_g,
                    jnp.tile(diff_q_norm_g[0], reps) * (DIFF_HEAD ** -0.5 * LOG2E), jnp.tile(k_norm_g, reps)])
    q, kq, vq = _qkv(x2d, gn, bf(diff_w_q[0]), bf(w_k_shared), bf(w_v_shared), bd, tb_attn)
    bias = _bias_tiles(rel_bias, tb_attn)
    logit_bound = 1.02 * LOG2E * (DIFF_HEAD ** 0.5 * jnp.max(jnp.abs(diff_q_norm_g[0] * k_norm_g))
                                  + jnp.max(jnp.abs(rel_bias - rel_bias[-1:])))
    attn = functools.partial(_attn, q, kq, vq, bias, diff_lam[0], diff_subln_g[0].reshape(1, LANES),
                             batch, seq, tb_attn, 4, 1)
    o = lax.cond(logit_bound <= LOGIT_BOUND, lambda: attn(True), lambda: attn(False))
    x2d = _tail(_tail_attn_kernel, (x2d, o), p2d[1], (), norm_g[1, 1:3],
                bf(diff_w_o[0]), bf(mlp_w1[1]), bf(mlp_w2[1]), bf(ple_w_up[1]), bf(ple_w_gate[1]), tm)
    return x2d.reshape(batch, seq, d)
```

```python
import functools
import math

import numpy as np
import jax
import jax.numpy as jnp
from jax import lax
from jax.experimental import pallas as pl
from jax.experimental.pallas import tpu as pltpu

F32 = jnp.float32
BF16 = jnp.bfloat16

D_MODEL = 1024
RWKV_HEAD = 64
N_HEADS = D_MODEL // RWKV_HEAD
DIFF_HEAD = 64
DIFF_HEADS = 8
LANES = 128
CHUNK = 64
SUB = 16
FF_CHUNK = 1024
NORM_EPS = 1e-6
GN_EPS = 64e-5
N_BUCKETS = 32
MAX_DIST = 128
NEG_BIG = -1e30
LOG2E = math.log2(math.e)
LOGIT_BOUND = 60.0
VMEM_LIMIT = 56 * 1024 * 1024


def _dot(a, b):
    return jnp.dot(a.astype(BF16), b.astype(BF16), preferred_element_type=F32)


def _dot_nt(a, b):
    return lax.dot_general(a.astype(BF16), b.astype(BF16), (((1,), (1,)), ((), ())),
                           preferred_element_type=F32)


def _dot_tn(a, b):
    return lax.dot_general(a.astype(BF16), b.astype(BF16), (((0,), (0,)), ((), ())),
                           preferred_element_type=F32)


def _rms(x, g):
    return x * lax.rsqrt(jnp.mean(x * x, axis=-1, keepdims=True) + NORM_EPS) * g


def _sigmoid(x):
    return 1.0 / (1.0 + jnp.exp(-x))


def _head_reduce(x, bc_ref):
    return _dot(x, bc_ref[...])


def _head_expand(c, be_ref):
    lane = lax.broadcasted_iota(jnp.int32, (1, LANES), 1)
    hi = c.astype(BF16).astype(F32)
    return _dot(jnp.where(lane < N_HEADS, hi, c - hi), be_ref[...])


def _const_spec(shape):
    nd = len(shape)
    return pl.BlockSpec(shape, lambda *_: (0,) * nd, pipeline_mode=pl.Buffered(1))


def _row_spec(tm, width):
    return pl.BlockSpec((tm, width), lambda i: (i, 0))


def _rwkv_pre_kernel(x_ref, xp_ref, vec_ref, mix_ref, wr_ref, wk_ref, wv_ref,
                     w1_ref, w2_ref, a1_ref, a2_ref, g1_ref, g2_ref, bc_ref, be_ref,
                     r_out, lw_out, k_out, v_out, kk_out, a_out, g_out, bonus_out, *, first_block):
    g0 = vec_ref[0:1, :]
    h = _rms(x_ref[...], g0)
    prev = _rms(xp_ref[...], g0)[7:8, :]
    prev = jnp.where(first_block, 0.0, prev)
    row = lax.broadcasted_iota(jnp.int32, h.shape, 0)
    hprev = jnp.where(row == 0, prev, pltpu.roll(h, 1, 0))
    dx = hprev - h

    def mixed(j):
        return (h + dx * mix_ref[j:j + 1, :]).astype(BF16)

    r = _dot(mixed(0), wr_ref[...])
    k = _dot(mixed(2), wk_ref[...])
    v = _dot(mixed(3), wv_ref[...])
    w_lora = _dot(jnp.tanh(_dot(mixed(1), w1_ref[...])), w2_ref[...])
    a_lora = _dot(_dot(mixed(4), a1_ref[...]), a2_ref[...])
    g = _dot(_sigmoid(_dot(mixed(5), g1_ref[...])), g2_ref[...])

    z = -(vec_ref[1:2, :] + w_lora)
    softplus = jnp.maximum(z, 0.0) + jnp.log1p(jnp.exp(-jnp.abs(z)))
    lw = -jnp.exp(-softplus - 0.5)
    a = _sigmoid(vec_ref[2:3, :] + a_lora)

    kk = k * vec_ref[3:4, :]
    ss = _head_reduce(kk * kk, bc_ref)
    kk = kk * _head_expand(1.0 / jnp.maximum(jnp.sqrt(ss), 1e-12), be_ref)
    k = k * (1.0 + (a - 1.0) * vec_ref[4:5, :])
    bonus = _head_expand(_head_reduce(r * k * vec_ref[5:6, :], bc_ref), be_ref) * v

    r_out[...] = r
    lw_out[...] = lw
    k_out[...] = k
    v_out[...] = v
    kk_out[...] = kk
    a_out[...] = a
    g_out[...] = g.astype(g_out.dtype)
    bonus_out[...] = bonus.astype(bonus_out.dtype)


def _rec_kernel(r_ref, lw_ref, k_ref, v_ref, kk_ref, a_ref, y_ref, s_ref, *, nchunk, npair, first_block):
    c2 = 2 * CHUNK
    width = npair * LANES

    @pl.when(first_block)
    def _():
        s_ref[...] = jnp.zeros_like(s_ref)

    ri = lax.broadcasted_iota(jnp.int32, (c2, c2), 0)
    ci = lax.broadcasted_iota(jnp.int32, (c2, c2), 1)
    sub_diag = (ri // SUB) == (ci // SUB)
    eye = (ri == ci).astype(F32)
    gi = lax.broadcasted_iota(jnp.int32, (2 * c2, 2 * c2), 0)
    gj = lax.broadcasted_iota(jnp.int32, (2 * c2, 2 * c2), 1)
    gram_mask = (gj % CHUNK) < (gi % CHUNK) + jnp.where(gi < c2, 0, 1)
    lo_lane = (lax.broadcasted_iota(jnp.int32, (1, width), 1) % LANES) < CHUNK
    tri = (lax.broadcasted_iota(jnp.int32, (CHUNK, CHUNK), 1)
           <= lax.broadcasted_iota(jnp.int32, (CHUNK, CHUNK), 0)).astype(BF16)
    pairs = range(npair)

    def per_pair(x):
        return [x[:, p * LANES:(p + 1) * LANES] for p in pairs]

    def stacked(x):
        return [jnp.concatenate([a, b], axis=0)
                for a, b in zip(per_pair(jnp.where(lo_lane, x, 0.0)), per_pair(jnp.where(lo_lane, 0.0, x)))]

    def chunk(c, carry):
        sl = pl.ds(pl.multiple_of(c * CHUNK, CHUNK), CHUNK)
        lw = lw_ref[sl, :]
        l_hi = lw.astype(BF16)
        l_mid = (lw - l_hi.astype(F32)).astype(BF16)
        l_lo = (lw - l_hi.astype(F32) - l_mid.astype(F32)).astype(BF16)
        cum = _dot(tri, l_hi) + _dot(tri, l_mid) + _dot(tri, l_lo)
        tot = cum[CHUNK - 1:CHUNK, :]
        e_neg = jnp.exp(-cum)
        e_rest = jnp.exp(tot - cum)
        k = k_ref[sl, :]
        kk = kk_ref[sl, :]
        kb = kk * a_ref[sl, :]
        at = stacked(-kk * jnp.exp(cum - lw))
        rt = stacked(r_ref[sl, :] * jnp.exp(cum))
        bt = stacked(kb * e_neg)
        kt = stacked(k * e_neg)
        bw = stacked(kb * e_rest)
        kw = stacked(k * e_rest)
        vm = stacked(v_ref[sl, :])
        decay = per_pair(jnp.exp(tot))

        lhs = [jnp.concatenate([a, r], axis=0).astype(BF16) for a, r in zip(at, rt)]
        gram = [jnp.where(gram_mask, _dot_nt(l, jnp.concatenate([b, kq], axis=0)), 0.0)
                for l, b, kq in zip(lhs, bt, kt)]
        abd = [g[:c2, :c2] for g in gram]
        dg = [jnp.where(sub_diag, x, 0.0) for x in abd]
        off = [x - d for x, d in zip(abd, dg)]
        t = [eye + d for d in dg]
        x = [_dot(d, d) for d in dg]
        for _ in range(2):
            res = [_dot(xi, jnp.concatenate([ti, xi], axis=1)) for xi, ti in zip(x, t)]
            t = [ti + r[:, :c2] for ti, r in zip(t, res)]
            x = [r[:, c2:] for r in res]
        t = [ti + _dot(xi, ti) for xi, ti in zip(x, t)]
        nn = [_dot(ti, o) for ti, o in zip(t, off)]
        res = [_dot(n, jnp.concatenate([n, ti], axis=1)) for n, ti in zip(nn, t)]
        q = [ti + r[:, c2:] for ti, r in zip(t, res)]
        tinv = [qi + _dot(r[:, :c2], qi) for qi, r in zip(q, res)]

        hs = [_dot_nt(l, s_ref[p]) for p, l in zip(pairs, lhs)]
        w = [h[:c2] + _dot(g[:c2, c2:], v) for h, g, v in zip(hs, gram, vm)]
        u = [_dot(ti, wi) for ti, wi in zip(tinv, w)]
        uv = [jnp.concatenate([ui, v], axis=0).astype(BF16) for ui, v in zip(u, vm)]
        ym = [h[c2:] + _dot(g[c2:], x2) for h, g, x2 in zip(hs, gram, uv)]
        y_ref[sl, :] = jnp.concatenate([yi[:CHUNK] + yi[CHUNK:] for yi in ym], axis=1)
        for p in pairs:
            s_ref[p] = s_ref[p] * decay[p] + _dot_tn(uv[p], jnp.concatenate([bw[p], kw[p]], axis=0))
        return carry

    lax.fori_loop(0, nchunk, chunk, 0)


def _rwkv_kernel(*refs, nchunk, npair):
    ins, (y_out, g_out, bonus_out), (r_s, lw_s, k_s, v_s, kk_s, a_s, s_ref) = refs[:15], refs[15:18], refs[18:]
    first_block = pl.program_id(1) == 0
    _rwkv_pre_kernel(*ins, r_s, lw_s, k_s, v_s, kk_s, a_s, g_out, bonus_out, first_block=first_block)
    _rec_kernel(r_s, lw_s, k_s, v_s, kk_s, a_s, y_out, s_ref, nchunk=nchunk, npair=npair,
                first_block=first_block)


def _rwkv(x2d, vecs, mix, wr, wk, wv, w1, w2, a1, a2, g1, g2, bc, be, batch, seq, tb):
    n, d = x2d.shape
    nt = seq // tb
    npair = d // LANES
    consts = (vecs, mix, wr, wk, wv, w1, w2, a1, a2, g1, g2, bc, be)
    row_spec = pl.BlockSpec((tb, d), lambda b, t: (b * nt + t, 0))
    prev_spec = pl.BlockSpec((8, d), lambda b, t: (jnp.maximum((b * nt + t) * (tb // 8) - 1, 0), 0))
    return pl.pallas_call(
        functools.partial(_rwkv_kernel, nchunk=tb // CHUNK, npair=npair),
        grid=(batch, nt),
        in_specs=[row_spec, prev_spec] + [_const_spec(c.shape) for c in consts],
        out_specs=[row_spec] * 3,
        out_shape=[jax.ShapeDtypeStruct((n, d), F32)] + [jax.ShapeDtypeStruct((n, d), BF16)] * 2,
        scratch_shapes=[pltpu.VMEM((tb, d), F32)] * 6 + [pltpu.VMEM((npair, LANES, LANES), F32)],
        compiler_params=pltpu.CompilerParams(dimension_semantics=("arbitrary", "arbitrary"),
                                             vmem_limit_bytes=VMEM_LIMIT),
        name="rwkv",
    )(x2d, x2d, *consts)


def _tail_body(x, pre, p_ref, gn_ref, wo_ref, w1_ref, w2_ref, wup_ref, wgate_ref, out_ref):
    x = x + _dot(pre, wo_ref[...])
    hn = _rms(x, gn_ref[0:1, :]).astype(BF16)
    d_ff = w1_ref.shape[1]
    acc = x
    for c in range(0, d_ff, FF_CHUNK):
        mid = jnp.maximum(_dot(hn, w1_ref[:, c:c + FF_CHUNK]), 0.0)
        acc = acc + _dot(mid * mid, w2_ref[c:c + FF_CHUNK, :])
    gate = _sigmoid(_dot(_rms(acc, gn_ref[1:2, :]), wgate_ref[...]))
    out_ref[...] = acc + _dot(p_ref[...], wup_ref[...]) * gate


def _tail_attn_kernel(x_ref, o_ref, p_ref, *rest):
    _tail_body(x_ref[...], o_ref[...], p_ref, *rest)


def _tail_rwkv_kernel(x_ref, y_ref, bonus_ref, g_ref, p_ref, vec_ref, bc_ref, be_ref, *rest):
    y = y_ref[...]
    inv_n = 1.0 / RWKV_HEAD
    yc = y - _head_expand(_head_reduce(y, bc_ref) * inv_n, be_ref)
    var = _head_reduce(yc * yc, bc_ref) * inv_n
    yn = yc * _head_expand(lax.rsqrt(var + GN_EPS), be_ref) * vec_ref[6:7, :] + vec_ref[7:8, :]
    _tail_body(x_ref[...], (yn + bonus_ref[...]) * g_ref[...], p_ref, *rest)


def _tail(kernel_fn, rows, p3d, layer, extra_consts, gn, wo, w1, w2, wup, wgate, tm):
    n, d = rows[0].shape
    consts = tuple(extra_consts) + (gn, wo, w1, w2, wup, wgate)
    return pl.pallas_call(
        kernel_fn,
        grid=(n // tm,),
        in_specs=[_row_spec(tm, d)] * len(rows)
                 + [pl.BlockSpec((None, tm, p3d.shape[2]), lambda i: (layer, i, 0))]
                 + [_const_spec(c.shape) for c in consts],
        out_specs=_row_spec(tm, d),
        out_shape=jax.ShapeDtypeStruct((n, d), F32),
        compiler_params=pltpu.CompilerParams(dimension_semantics=("arbitrary",),
                                             vmem_limit_bytes=VMEM_LIMIT),
        name="tail",
    )(*rows, p3d, *consts)


def _qkv_kernel(x_ref, gn_ref, wq_ref, wk_ref, wv_ref, bc_ref, be_ref, q_out, k_out, v_out):
    x = x_ref[...]
    inv_n = 1.0 / DIFF_HEAD

    def head_rms(t, g):
        ms = _head_reduce(t * t, bc_ref) * inv_n
        return t * _head_expand(lax.rsqrt(ms + NORM_EPS), be_ref) * g

    hq = _rms(x, gn_ref[0:1, :])
    hk = _rms(x, gn_ref[1:2, :]).astype(BF16)
    q = head_rms(_dot(hq, wq_ref[...]), gn_ref[2:3, :])
    q_out[...] = q.astype(q_out.dtype)
    k_out[...] = head_rms(_dot(hk, wk_ref[...]), gn_ref[3:4, :]).astype(k_out.dtype)
    v_out[0] = _dot(hk, wv_ref[...]).T.astype(v_out.dtype)


def _qkv(x2d, gn, wq, wk, wv, bc, be, tm):
    n, d = x2d.shape
    consts = (gn, wq, wk, wv, bc, be)
    out_sd = jax.ShapeDtypeStruct((n, d), BF16)
    return pl.pallas_call(
        _qkv_kernel,
        grid=(n // tm,),
        in_specs=[_row_spec(tm, d)] + [_const_spec(c.shape) for c in consts],
        out_specs=[_row_spec(tm, d)] * 2 + [pl.BlockSpec((1, d, tm), lambda i: (i, 0, 0))],
        out_shape=[out_sd] * 2 + [jax.ShapeDtypeStruct((n // tm, d, tm), BF16)],
        compiler_params=pltpu.CompilerParams(dimension_semantics=("arbitrary",),
                                             vmem_limit_bytes=VMEM_LIMIT),
        name="qkv",
    )(x2d, *consts)


def _bucket_tiles(tb):
    i = np.arange(tb, dtype=np.int64)[None, :]
    j = np.arange(tb, dtype=np.int64)[:, None]

    def bucket(rel):
        n = np.maximum(rel, 0)
        max_exact = N_BUCKETS // 2
        nf = np.maximum(n, 1).astype(np.float32)
        large = max_exact + (np.log(nf / np.float32(max_exact)) / np.float32(math.log(MAX_DIST / max_exact))
                             * np.float32(N_BUCKETS - max_exact)).astype(np.int32)
        large = np.minimum(large, N_BUCKETS - 1)
        return np.where(n < max_exact, n, large).astype(np.int32)

    diag = np.where(i - j >= 0, bucket(i - j), -1)
    near = bucket(tb + i - j)
    tiles = np.stack([diag, near]).astype(np.int32)
    return np.concatenate([tiles, tiles], axis=2)


def _bias_kernel(tab_ref, bucket_ref, out_ref):
    h = pl.program_id(0)
    b = bucket_ref[...]
    far = tab_ref[N_BUCKETS - 1, h]
    acc = jnp.where(b < 0, NEG_BIG, 0.0)
    for n in range(N_BUCKETS - 1):
        acc = jnp.where(b == n, (tab_ref[n, h] - far) * LOG2E, acc)
    out_ref[0] = acc


def _bias_tiles(rel_bias, tb):
    buckets = jnp.asarray(_bucket_tiles(tb))
    nh = rel_bias.shape[1]
    return pl.pallas_call(
        _bias_kernel,
        grid=(nh,),
        in_specs=[pl.BlockSpec(memory_space=pltpu.SMEM),
                  pl.BlockSpec((2, tb, 2 * tb), lambda h: (0, 0, 0))],
        out_specs=pl.BlockSpec((1, 2, tb, 2 * tb), lambda h: (h, 0, 0, 0)),
        out_shape=jax.ShapeDtypeStruct((nh, 2, tb, 2 * tb), F32),
        compiler_params=pltpu.CompilerParams(dimension_semantics=("arbitrary",)),
        name="bias",
    )(rel_bias, buckets)


def _attn_kernel(q_ref, k_ref, vt_ref, bias_ref, lam_ref, g_ref, o_ref, m_ref, l_ref, acc_ref,
                 *, tb, nhead, out_scale, lambda_init, bounded):
    qi = pl.program_id(2)
    heads = range(nhead)
    lane = lax.broadcasted_iota(jnp.int32, (1, LANES), 1)

    def head_lanes(h):
        return slice(h * LANES, (h + 1) * LANES)

    def stacked_q(h):
        q = q_ref[:, head_lanes(h)]
        zero = jnp.zeros_like(q)
        return jnp.concatenate([jnp.where(lane < DIFF_HEAD, q, zero),
                                jnp.where(lane < DIFF_HEAD, zero, q)], axis=0)

    qs = [stacked_q(h) for h in heads]
    m_ref[...] = jnp.full_like(m_ref, NEG_BIG)
    l_ref[...] = jnp.zeros_like(l_ref)
    acc_ref[...] = jnp.zeros_like(acc_ref)

    def advance(j, biases):
        nblk = len(biases)
        rows = pl.ds(pl.multiple_of(j * tb, tb), nblk * tb)
        logits = lambda h: _dot_nt(k_ref[rows, head_lanes(h)], qs[h])
        ahead = min(2, nhead)
        st = {h: logits(h) for h in range(ahead)}
        spans = []
        for i, b in enumerate(biases):
            if b is None and spans and spans[-1][2] is None:
                spans[-1] = (spans[-1][0], i + 1, None)
            else:
                spans.append((i, i + 1, b))
        for h in heads:
            if h + ahead < nhead:
                st[h + ahead] = logits(h + ahead)
            s = st.pop(h)
            parts = [s[i * tb:e * tb] if b is None else s[i * tb:e * tb] + bias_ref[h, b] for i, e, b in spans]
            if bounded:
                ps = [jnp.exp2(x) for x in parts]
                l_new = l_ref[h]
            else:
                m_old = m_ref[h]
                m_new = m_old
                for x in parts:
                    m_new = jnp.maximum(m_new, jnp.max(x, axis=0, keepdims=True))
                alpha = jnp.exp2(m_old - m_new)
                ps = [jnp.exp2(x - m_new) for x in parts]
                l_new = alpha * l_ref[h]
                m_ref[h] = m_new
            for p in ps:
                l_new = l_new + jnp.sum(p, axis=0, keepdims=True)
            l_ref[h] = l_new
            pv = None
            for (i, e, _), p in zip(spans, ps):
                p = p.astype(BF16)
                for b in range(i, e):
                    term = _dot(vt_ref[j + b, head_lanes(h), :], p[(b - i) * tb:(b - i + 1) * tb])
                    pv = term if pv is None else pv + term
            acc_ref[h] = acc_ref[h] + pv if bounded else alpha * acc_ref[h] + pv

    def far_pair(t, carry):
        advance(2 * t, (None, None))
        return carry

    nfar = jnp.maximum(qi - 1, 0)
    lax.fori_loop(0, nfar // 2, far_pair, 0)

    @pl.when(qi == 0)
    def _():
        advance(0, (0,))

    @pl.when((qi >= 1) & (nfar % 2 == 0))
    def _():
        advance(qi - 1, (1, 0))

    @pl.when(nfar % 2 == 1)
    def _():
        advance(qi - 2, (None, 1, 0))

    lam = lam_ref[...]
    lam_full = (jnp.exp(jnp.sum(lam[0:1] * lam[1:2], axis=1, keepdims=True))
                - jnp.exp(jnp.sum(lam[2:3] * lam[3:4], axis=1, keepdims=True)) + lambda_init)
    for h in heads:
        o = acc_ref[h] * (1.0 / l_ref[h])
        o = o[:, :tb] - lam_full * o[:, tb:]
        o = o * lax.rsqrt(jnp.mean(o * o, axis=0, keepdims=True) + NORM_EPS)
        o_ref[:, head_lanes(h)] = (o.T * (g_ref[...] * out_scale)).astype(o_ref.dtype)


def _attn(q, k, vt, bias, lam, subln_g, batch, seq, tb, nhead, layer_idx, bounded):
    n, d = q.shape
    ngroup = d // (LANES * nhead)
    width = LANES * nhead
    nq = seq // tb
    lambda_init = 0.8 - 0.6 * math.exp(-0.3 * layer_idx)
    q_spec = pl.BlockSpec((tb, width), lambda b, h, i: (b * nq + i, h))
    return pl.pallas_call(
        functools.partial(_attn_kernel, tb=tb, nhead=nhead, out_scale=1.0 - lambda_init,
                          lambda_init=lambda_init, bounded=bounded),
        grid=(batch, ngroup, nq),
        in_specs=[q_spec,
                  pl.BlockSpec((seq, width), lambda b, h, i: (b, h)),
                  pl.BlockSpec((nq, width, tb), lambda b, h, i: (b, h, 0)),
                  pl.BlockSpec((nhead, 2, tb, 2 * tb), lambda b, h, i: (h, 0, 0, 0)),
                  pl.BlockSpec(lam.shape, lambda b, h, i: (0, 0)),
                  pl.BlockSpec(subln_g.shape, lambda b, h, i: (0, 0))],
        out_specs=q_spec,
        out_shape=jax.ShapeDtypeStruct((n, d), BF16),
        scratch_shapes=[pltpu.VMEM((nhead, 1, 2 * tb), F32), pltpu.VMEM((nhead, 1, 2 * tb), F32),
                        pltpu.VMEM((nhead, LANES, 2 * tb), F32)],
        compiler_params=pltpu.CompilerParams(
            dimension_semantics=("arbitrary", "arbitrary", "arbitrary"),
            vmem_limit_bytes=VMEM_LIMIT),
        name="attn",
    )(q, k, vt, bias, lam, subln_g)


def kernel(x, p, norm_g, mlp_w1, mlp_w2, ple_w_up, ple_w_gate, rwkv_mix, rwkv_w_rkvo, rwkv_w0, rwkv_w1, rwkv_w2, rwkv_a0, rwkv_a1, rwkv_a2, rwkv_g1, rwkv_g2, rwkv_k_k, rwkv_k_a, rwkv_r_k, rwkv_ln_w, rwkv_ln_b, kv_norm_g, w_k_shared, w_v_shared, k_norm_g, diff_w_q, diff_q_norm_g, diff_lam, diff_subln_g, diff_w_o, rel_bias):
    batch, seq, d = x.shape
    assert d == D_MODEL and norm_g.shape[0] == 2
    n = batch * seq
    tm = min(256, seq)
    tb_rec = min(256, seq)
    tb_attn = min(256, seq)
    assert seq % tm == 0 and seq % tb_rec == 0 and seq % tb_attn == 0 and tb_attn >= MAX_DIST

    bf = lambda w: w.astype(BF16)
    x2d = x.reshape(n, d)
    p3d = p.reshape(p.shape[0], n, p.shape[-1])
    head_id = jnp.arange(d, dtype=jnp.int32) // RWKV_HEAD
    slot = jnp.arange(LANES, dtype=jnp.int32)
    bc = ((head_id[:, None] == slot[None, :] % N_HEADS) & (slot[None, :] < 2 * N_HEADS)).astype(BF16)
    be = bc.T

    vecs = jnp.stack([norm_g[0, 0], rwkv_w0[0], rwkv_a0[0], rwkv_k_k[0], rwkv_k_a[0],
                      rwkv_r_k[0].reshape(d), rwkv_ln_w[0], rwkv_ln_b[0]])
    y, g, bonus = _rwkv(
        x2d, vecs, rwkv_mix[0], bf(rwkv_w_rkvo[0, 0]), bf(rwkv_w_rkvo[0, 1]), bf(rwkv_w_rkvo[0, 2]),
        bf(rwkv_w1[0]), bf(rwkv_w2[0]), bf(rwkv_a1[0]), bf(rwkv_a2[0]), bf(rwkv_g1[0]), bf(rwkv_g2[0]),
        bc, be, batch, seq, tb_rec)
    x2d = _tail(_tail_rwkv_kernel, (x2d, y, bonus, g), p3d, 0, (vecs, bc, be), norm_g[0, 1:3],
                bf(rwkv_w_rkvo[0, 3]), bf(mlp_w1[0]), bf(mlp_w2[0]), bf(ple_w_up[0]), bf(ple_w_gate[0]), tm)

    reps = d // DIFF_HEAD
    gn = jnp.stack([norm_g[1, 0], kv_norm_g,
                    jnp.tile(diff_q_norm_g[0], reps) * (DIFF_HEAD ** -0.5 * LOG2E), jnp.tile(k_norm_g, reps)])
    q, kq, vq = _qkv(x2d, gn, bf(diff_w_q[0]), bf(w_k_shared), bf(w_v_shared), bc, be, tb_attn)
    bias = _bias_tiles(rel_bias, tb_attn)
    logit_bound = 1.02 * LOG2E * (DIFF_HEAD ** 0.5 * jnp.max(jnp.abs(diff_q_norm_g[0] * k_norm_g))
                                  + jnp.max(jnp.abs(rel_bias - rel_bias[-1:])))
    attn = functools.partial(_attn, q, kq, vq, bias, diff_lam[0], diff_subln_g[0].reshape(1, LANES),
                             batch, seq, tb_attn, 4, 1)
    o = lax.cond(logit_bound <= LOGIT_BOUND, lambda: attn(True), lambda: attn(False))
    x2d = _tail(_tail_attn_kernel, (x2d, o), p3d, 1, (), norm_g[1, 1:3],
                bf(diff_w_o[0]), bf(mlp_w1[1]), bf(mlp_w2[1]), bf(ple_w_up[1]), bf(ple_w_gate[1]), tm)
    return x2d.reshape(batch, seq, d)
```

```python
import functools
import math

import numpy as np
import jax
import jax.numpy as jnp
from jax import lax
from jax.experimental import pallas as pl
from jax.experimental.pallas import tpu as pltpu

F32 = jnp.float32
BF16 = jnp.bfloat16

D_MODEL = 1024
RWKV_HEAD = 64
N_HEADS = D_MODEL // RWKV_HEAD
DIFF_HEAD = 64
DIFF_HEADS = 8
LANES = 128
CHUNK = 64
SUB = 16
FF_CHUNK = 1024
NORM_EPS = 1e-6
GN_EPS = 64e-5
N_BUCKETS = 32
MAX_DIST = 128
NEG_BIG = -1e30
LOG2E = math.log2(math.e)
LOGIT_BOUND = 60.0
VMEM_LIMIT = 56 * 1024 * 1024


def _dot(a, b):
    return jnp.dot(a.astype(BF16), b.astype(BF16), preferred_element_type=F32)


def _dot_nt(a, b):
    return lax.dot_general(a.astype(BF16), b.astype(BF16), (((1,), (1,)), ((), ())),
                           preferred_element_type=F32)


def _dot_tn(a, b):
    return lax.dot_general(a.astype(BF16), b.astype(BF16), (((0,), (0,)), ((), ())),
                           preferred_element_type=F32)


def _rms(x, g):
    return x * lax.rsqrt(jnp.mean(x * x, axis=-1, keepdims=True) + NORM_EPS) * g


def _sigmoid(x):
    return 1.0 / (1.0 + jnp.exp(-x))


def _head_reduce(x, bc_ref):
    return _dot(x, bc_ref[...])


def _head_expand(c, be_ref):
    lane = lax.broadcasted_iota(jnp.int32, (1, LANES), 1)
    hi = c.astype(BF16).astype(F32)
    return _dot(jnp.where(lane < N_HEADS, hi, c - hi), be_ref[...])


def _const_spec(shape):
    nd = len(shape)
    return pl.BlockSpec(shape, lambda *_: (0,) * nd, pipeline_mode=pl.Buffered(1))


def _row_spec(tm, width):
    return pl.BlockSpec((tm, width), lambda i: (i, 0))


def _rwkv_pre_kernel(x_ref, xp_ref, vec_ref, mix_ref, wr_ref, wk_ref, wv_ref,
                     w1_ref, w2_ref, a1_ref, a2_ref, g1_ref, g2_ref, bc_ref, be_ref,
                     r_out, lw_out, k_out, v_out, kk_out, a_out, g_out, bonus_out, *, first_block):
    g0 = vec_ref[0:1, :]
    h = _rms(x_ref[...], g0)
    prev = _rms(xp_ref[...], g0)[7:8, :]
    prev = jnp.where(first_block, 0.0, prev)
    row = lax.broadcasted_iota(jnp.int32, h.shape, 0)
    hprev = jnp.where(row == 0, prev, pltpu.roll(h, 1, 0))
    dx = hprev - h

    def mixed(j):
        return (h + dx * mix_ref[j:j + 1, :]).astype(BF16)

    r = _dot(mixed(0), wr_ref[...])
    k = _dot(mixed(2), wk_ref[...])
    v = _dot(mixed(3), wv_ref[...])
    w_lora = _dot(jnp.tanh(_dot(mixed(1), w1_ref[...])), w2_ref[...])
    a_lora = _dot(_dot(mixed(4), a1_ref[...]), a2_ref[...])
    g = _dot(_sigmoid(_dot(mixed(5), g1_ref[...])), g2_ref[...])

    z = -(vec_ref[1:2, :] + w_lora)
    softplus = jnp.maximum(z, 0.0) + jnp.log1p(jnp.exp(-jnp.abs(z)))
    lw = -jnp.exp(-softplus - 0.5)
    a = _sigmoid(vec_ref[2:3, :] + a_lora)

    kk = k * vec_ref[3:4, :]
    ss = _head_reduce(kk * kk, bc_ref)
    kk = kk * _head_expand(1.0 / jnp.maximum(jnp.sqrt(ss), 1e-12), be_ref)
    k = k * (1.0 + (a - 1.0) * vec_ref[4:5, :])
    bonus = _head_expand(_head_reduce(r * k * vec_ref[5:6, :], bc_ref), be_ref) * v

    r_out[...] = r
    lw_out[...] = lw
    k_out[...] = k
    v_out[...] = v
    kk_out[...] = kk
    a_out[...] = a
    g_out[...] = g.astype(g_out.dtype)
    bonus_out[...] = bonus.astype(bonus_out.dtype)


def _rec_kernel(r_ref, lw_ref, k_ref, v_ref, kk_ref, a_ref, y_ref, s_ref, *, nchunk, npair, first_block):
    c2 = 2 * CHUNK
    width = npair * LANES

    @pl.when(first_block)
    def _():
        s_ref[...] = jnp.zeros_like(s_ref)

    ri = lax.broadcasted_iota(jnp.int32, (c2, c2), 0)
    ci = lax.broadcasted_iota(jnp.int32, (c2, c2), 1)
    sub_diag = (ri // SUB) == (ci // SUB)
    eye = (ri == ci).astype(F32)
    gi = lax.broadcasted_iota(jnp.int32, (2 * c2, 2 * c2), 0)
    gj = lax.broadcasted_iota(jnp.int32, (2 * c2, 2 * c2), 1)
    gram_mask = (gj % CHUNK) < (gi % CHUNK) + jnp.where(gi < c2, 0, 1)
    lo_lane = (lax.broadcasted_iota(jnp.int32, (1, width), 1) % LANES) < CHUNK
    tri = (lax.broadcasted_iota(jnp.int32, (CHUNK, CHUNK), 1)
           <= lax.broadcasted_iota(jnp.int32, (CHUNK, CHUNK), 0)).astype(BF16)
    pairs = range(npair)
    nsub = c2 // SUB
    eye_p = (lax.broadcasted_iota(jnp.int32, (SUB, c2), 0)
             == lax.broadcasted_iota(jnp.int32, (SUB, c2), 1) % SUB).astype(F32)

    def pack(m):
        return m.reshape(nsub, SUB, c2).sum(axis=0)

    def unpack(p):
        return jnp.where(sub_diag, jnp.tile(p, (nsub, 1)), 0.0)

    def per_pair(x):
        return [x[:, p * LANES:(p + 1) * LANES] for p in pairs]

    def stacked(x):
        return [jnp.concatenate([a, b], axis=0)
                for a, b in zip(per_pair(jnp.where(lo_lane, x, 0.0)), per_pair(jnp.where(lo_lane, 0.0, x)))]

    def setup(c, out):
        sl = slice(c * CHUNK, (c + 1) * CHUNK)
        lw = lw_ref[sl, :]
        l_hi = lw.astype(BF16)
        l_mid = (lw - l_hi.astype(F32)).astype(BF16)
        l_lo = (lw - l_hi.astype(F32) - l_mid.astype(F32)).astype(BF16)
        cum = _dot(tri, l_hi) + _dot(tri, l_mid) + _dot(tri, l_lo)
        tot = cum[CHUNK - 1:CHUNK, :]
        e_neg = jnp.exp(-cum)
        e_rest = jnp.exp(tot - cum)
        k = k_ref[sl, :]
        kk = kk_ref[sl, :]
        kb = kk * a_ref[sl, :]
        at = stacked(-kk * jnp.exp(cum - lw))
        rt = stacked(r_ref[sl, :] * jnp.exp(cum))
        bt = stacked(kb * e_neg)
        kt = stacked(k * e_neg)
        out["bkw"] = [jnp.concatenate([b, kq], axis=0).astype(BF16)
                      for b, kq in zip(stacked(kb * e_rest), stacked(k * e_rest))]
        out["vm"] = stacked(v_ref[sl, :])
        out["decay"] = per_pair(jnp.exp(tot))
        lhs = out["lhs"] = [jnp.concatenate([a, r], axis=0).astype(BF16) for a, r in zip(at, rt)]
        yield
        gram = out["gram"] = [jnp.where(gram_mask, _dot_nt(l, jnp.concatenate([b, kq], axis=0)), 0.0)
                              for l, b, kq in zip(lhs, bt, kt)]
        abd = [g[:c2, :c2] for g in gram]
        dg = [jnp.where(sub_diag, x, 0.0) for x in abd]
        off = [x - d for x, d in zip(abd, dg)]
        yield
        tp = [eye_p + pack(d) for d in dg]
        xp = [_dot(pack(d), d) for d in dg]
        for _ in range(2):
            yield
            res = [_dot(jnp.concatenate([ti, xi], axis=0), unpack(xi)) for ti, xi in zip(tp, xp)]
            tp = [ti + r[:SUB] for ti, r in zip(tp, res)]
            xp = [r[SUB:] for r in res]
        yield
        t = [unpack(ti + _dot(ti, unpack(xi))) for ti, xi in zip(tp, xp)]
        yield
        nn = [_dot(ti, o) for ti, o in zip(t, off)]
        yield
        res = [_dot(n, jnp.concatenate([n, ti], axis=1)) for n, ti in zip(nn, t)]
        q = [ti + r[:, c2:] for ti, r in zip(t, res)]
        yield
        out["tinv"] = [qi + _dot(r[:, :c2], qi) for qi, r in zip(q, res)]

    def carry(c, pre):
        sl = slice(c * CHUNK, (c + 1) * CHUNK)
        gram, vm = pre["gram"], pre["vm"]
        hs = [_dot_nt(l, s_ref[p]) for p, l in zip(pairs, pre["lhs"])]
        yield
        w = [h[:c2] + _dot(g[:c2, c2:], v) for h, g, v in zip(hs, gram, vm)]
        yield
        u = [_dot(ti, wi) for ti, wi in zip(pre["tinv"], w)]
        uv = [jnp.concatenate([ui, v], axis=0).astype(BF16) for ui, v in zip(u, vm)]
        yield
        ym = [h[c2:] + _dot(g[c2:], x2) for h, g, x2 in zip(hs, gram, uv)]
        y_ref[sl, :] = jnp.concatenate([yi[:CHUNK] + yi[CHUNK:] for yi in ym], axis=1)
        yield
        for p in pairs:
            s_ref[p] = s_ref[p] * pre["decay"][p] + _dot_tn(uv[p], pre["bkw"][p])

    pre = [dict() for _ in range(nchunk)]
    for _ in setup(0, pre[0]):
        pass
    for c in range(nchunk):
        active = [carry(c, pre[c])] + ([setup(c + 1, pre[c + 1])] if c + 1 < nchunk else [])
        while active:
            for gen in list(active):
                if next(gen, StopIteration) is StopIteration:
                    active.remove(gen)


def _rwkv_kernel(*refs, nchunk, npair):
    ins, (y_out, g_out, bonus_out), (r_s, lw_s, k_s, v_s, kk_s, a_s, s_ref) = refs[:15], refs[15:18], refs[18:]
    first_block = pl.program_id(1) == 0
    _rwkv_pre_kernel(*ins, r_s, lw_s, k_s, v_s, kk_s, a_s, g_out, bonus_out, first_block=first_block)
    _rec_kernel(r_s, lw_s, k_s, v_s, kk_s, a_s, y_out, s_ref, nchunk=nchunk, npair=npair,
                first_block=first_block)


def _rwkv(x2d, vecs, mix, wr, wk, wv, w1, w2, a1, a2, g1, g2, bc, be, batch, seq, tb):
    n, d = x2d.shape
    nt = seq // tb
    npair = d // LANES
    consts = (vecs, mix, wr, wk, wv, w1, w2, a1, a2, g1, g2, bc, be)
    row_spec = pl.BlockSpec((tb, d), lambda b, t: (b * nt + t, 0))
    prev_spec = pl.BlockSpec((8, d), lambda b, t: (jnp.maximum((b * nt + t) * (tb // 8) - 1, 0), 0))
    return pl.pallas_call(
        functools.partial(_rwkv_kernel, nchunk=tb // CHUNK, npair=npair),
        grid=(batch, nt),
        in_specs=[row_spec, prev_spec] + [_const_spec(c.shape) for c in consts],
        out_specs=[row_spec] * 3,
        out_shape=[jax.ShapeDtypeStruct((n, d), F32)] + [jax.ShapeDtypeStruct((n, d), BF16)] * 2,
        scratch_shapes=[pltpu.VMEM((tb, d), F32)] * 6 + [pltpu.VMEM((npair, LANES, LANES), F32)],
        compiler_params=pltpu.CompilerParams(dimension_semantics=("arbitrary", "arbitrary"),
                                             vmem_limit_bytes=VMEM_LIMIT),
        name="rwkv",
    )(x2d, x2d, *consts)


def _tail_body(x, pre, p_ref, gn_ref, wo_ref, w1_ref, w2_ref, wup_ref, wgate_ref, out_ref):
    x = x + _dot(pre, wo_ref[...])
    hn = _rms(x, gn_ref[0:1, :]).astype(BF16)
    d_ff = w1_ref.shape[1]
    acc = x
    for c in range(0, d_ff, FF_CHUNK):
        mid = jnp.maximum(_dot(hn, w1_ref[:, c:c + FF_CHUNK]), 0.0)
        acc = acc + _dot(mid * mid, w2_ref[c:c + FF_CHUNK, :])
    gate = _sigmoid(_dot(_rms(acc, gn_ref[1:2, :]), wgate_ref[...]))
    out_ref[...] = acc + _dot(p_ref[...], wup_ref[...]) * gate


def _tail_attn_kernel(x_ref, o_ref, p_ref, *rest):
    _tail_body(x_ref[...], o_ref[...], p_ref, *rest)


def _tail_rwkv_kernel(x_ref, y_ref, bonus_ref, g_ref, p_ref, vec_ref, bc_ref, be_ref, *rest):
    y = y_ref[...]
    inv_n = 1.0 / RWKV_HEAD
    yc = y - _head_expand(_head_reduce(y, bc_ref) * inv_n, be_ref)
    var = _head_reduce(yc * yc, bc_ref) * inv_n
    yn = yc * _head_expand(lax.rsqrt(var + GN_EPS), be_ref) * vec_ref[6:7, :] + vec_ref[7:8, :]
    _tail_body(x_ref[...], (yn + bonus_ref[...]) * g_ref[...], p_ref, *rest)


def _tail(kernel_fn, rows, p3d, layer, extra_consts, gn, wo, w1, w2, wup, wgate, tm):
    n, d = rows[0].shape
    consts = tuple(extra_consts) + (gn, wo, w1, w2, wup, wgate)
    return pl.pallas_call(
        kernel_fn,
        grid=(n // tm,),
        in_specs=[_row_spec(tm, d)] * len(rows)
                 + [pl.BlockSpec((None, tm, p3d.shape[2]), lambda i: (layer, i, 0))]
                 + [_const_spec(c.shape) for c in consts],
        out_specs=_row_spec(tm, d),
        out_shape=jax.ShapeDtypeStruct((n, d), F32),
        compiler_params=pltpu.CompilerParams(dimension_semantics=("arbitrary",),
                                             vmem_limit_bytes=VMEM_LIMIT),
        name="tail",
    )(*rows, p3d, *consts)


def _qkv_kernel(x_ref, gn_ref, wq_ref, wk_ref, wv_ref, bc_ref, be_ref, q_out, k_out, v_out):
    x = x_ref[...]
    inv_n = 1.0 / DIFF_HEAD

    def head_rms(t, g):
        ms = _head_reduce(t * t, bc_ref) * inv_n
        return t * _head_expand(lax.rsqrt(ms + NORM_EPS), be_ref) * g

    hq = _rms(x, gn_ref[0:1, :])
    hk = _rms(x, gn_ref[1:2, :]).astype(BF16)
    q = head_rms(_dot(hq, wq_ref[...]), gn_ref[2:3, :])
    q_out[...] = q.astype(q_out.dtype)
    k_out[...] = head_rms(_dot(hk, wk_ref[...]), gn_ref[3:4, :]).astype(k_out.dtype)
    v_out[0] = _dot(hk, wv_ref[...]).T.astype(v_out.dtype)


def _qkv(x2d, gn, wq, wk, wv, bc, be, tm):
    n, d = x2d.shape
    consts = (gn, wq, wk, wv, bc, be)
    out_sd = jax.ShapeDtypeStruct((n, d), BF16)
    return pl.pallas_call(
        _qkv_kernel,
        grid=(n // tm,),
        in_specs=[_row_spec(tm, d)] + [_const_spec(c.shape) for c in consts],
        out_specs=[_row_spec(tm, d)] * 2 + [pl.BlockSpec((1, d, tm), lambda i: (i, 0, 0))],
        out_shape=[out_sd] * 2 + [jax.ShapeDtypeStruct((n // tm, d, tm), BF16)],
        compiler_params=pltpu.CompilerParams(dimension_semantics=("arbitrary",),
                                             vmem_limit_bytes=VMEM_LIMIT),
        name="qkv",
    )(x2d, *consts)


def _bucket_tiles(tb):
    i = np.arange(tb, dtype=np.int64)[None, :]
    j = np.arange(tb, dtype=np.int64)[:, None]

    def bucket(rel):
        n = np.maximum(rel, 0)
        max_exact = N_BUCKETS // 2
        nf = np.maximum(n, 1).astype(np.float32)
        large = max_exact + (np.log(nf / np.float32(max_exact)) / np.float32(math.log(MAX_DIST / max_exact))
                             * np.float32(N_BUCKETS - max_exact)).astype(np.int32)
        large = np.minimum(large, N_BUCKETS - 1)
        return np.where(n < max_exact, n, large).astype(np.int32)

    diag = np.where(i - j >= 0, bucket(i - j), -1)
    near = bucket(tb + i - j)
    tiles = np.stack([diag, near]).astype(np.int32)
    return np.concatenate([tiles, tiles], axis=2)


def _bias_kernel(tab_ref, bucket_ref, out_ref):
    h = pl.program_id(0)
    b = bucket_ref[...]
    far = tab_ref[N_BUCKETS - 1, h]
    acc = jnp.where(b < 0, NEG_BIG, 0.0)
    for n in range(N_BUCKETS - 1):
        acc = jnp.where(b == n, (tab_ref[n, h] - far) * LOG2E, acc)
    out_ref[0] = acc


def _bias_tiles(rel_bias, tb):
    buckets = jnp.asarray(_bucket_tiles(tb))
    nh = rel_bias.shape[1]
    return pl.pallas_call(
        _bias_kernel,
        grid=(nh,),
        in_specs=[pl.BlockSpec(memory_space=pltpu.SMEM),
                  pl.BlockSpec((2, tb, 2 * tb), lambda h: (0, 0, 0))],
        out_specs=pl.BlockSpec((1, 2, tb, 2 * tb), lambda h: (h, 0, 0, 0)),
        out_shape=jax.ShapeDtypeStruct((nh, 2, tb, 2 * tb), F32),
        compiler_params=pltpu.CompilerParams(dimension_semantics=("arbitrary",)),
        name="bias",
    )(rel_bias, buckets)


def _attn_kernel(q_ref, k_ref, vt_ref, bias_ref, lam_ref, g_ref, o_ref, m_ref, l_ref, acc_ref,
                 *, tb, nhead, out_scale, lambda_init, bounded):
    qi = pl.program_id(2)
    heads = range(nhead)
    lane = lax.broadcasted_iota(jnp.int32, (1, LANES), 1)

    def head_lanes(h):
        return slice(h * LANES, (h + 1) * LANES)

    def stacked_q(h):
        q = q_ref[:, head_lanes(h)]
        zero = jnp.zeros_like(q)
        return jnp.concatenate([jnp.where(lane < DIFF_HEAD, q, zero),
                                jnp.where(lane < DIFF_HEAD, zero, q)], axis=0)

    qs = [stacked_q(h) for h in heads]
    m_ref[...] = jnp.full_like(m_ref, NEG_BIG)
    l_ref[...] = jnp.zeros_like(l_ref)
    acc_ref[...] = jnp.zeros_like(acc_ref)

    def advance(j, biases):
        nblk = len(biases)
        rows = pl.ds(pl.multiple_of(j * tb, tb), nblk * tb)
        logits = lambda h: _dot_nt(k_ref[rows, head_lanes(h)], qs[h])
        ahead = min(2, nhead)
        st = {h: logits(h) for h in range(ahead)}
        spans = []
        for i, b in enumerate(biases):
            if b is None and spans and spans[-1][2] is None:
                spans[-1] = (spans[-1][0], i + 1, None)
            else:
                spans.append((i, i + 1, b))
        for h in heads:
            if h + ahead < nhead:
                st[h + ahead] = logits(h + ahead)
            s = st.pop(h)
            parts = [s[i * tb:e * tb] if b is None else s[i * tb:e * tb] + bias_ref[h, b] for i, e, b in spans]
            if bounded:
                ps = [jnp.exp2(x) for x in parts]
                l_new = l_ref[h]
            else:
                m_old = m_ref[h]
                m_new = m_old
                for x in parts:
                    m_new = jnp.maximum(m_new, jnp.max(x, axis=0, keepdims=True))
                alpha = jnp.exp2(m_old - m_new)
                ps = [jnp.exp2(x - m_new) for x in parts]
                l_new = alpha * l_ref[h]
                m_ref[h] = m_new
            for p in ps:
                l_new = l_new + jnp.sum(p, axis=0, keepdims=True)
            l_ref[h] = l_new
            pv = None
            for (i, e, _), p in zip(spans, ps):
                p = p.astype(BF16)
                for b in range(i, e):
                    term = _dot(vt_ref[j + b, head_lanes(h), :], p[(b - i) * tb:(b - i + 1) * tb])
                    pv = term if pv is None else pv + term
            acc_ref[h] = acc_ref[h] + pv if bounded else alpha * acc_ref[h] + pv

    def far_pair(t, carry):
        advance(2 * t, (None, None))
        return carry

    nfar = jnp.maximum(qi - 1, 0)
    lax.fori_loop(0, nfar // 2, far_pair, 0)

    @pl.when(qi == 0)
    def _():
        advance(0, (0,))

    @pl.when((qi >= 1) & (nfar % 2 == 0))
    def _():
        advance(qi - 1, (1, 0))

    @pl.when(nfar % 2 == 1)
    def _():
        advance(qi - 2, (None, 1, 0))

    lam = lam_ref[...]
    lam_full = (jnp.exp(jnp.sum(lam[0:1] * lam[1:2], axis=1, keepdims=True))
                - jnp.exp(jnp.sum(lam[2:3] * lam[3:4], axis=1, keepdims=True)) + lambda_init)
    for h in heads:
        o = acc_ref[h] * (1.0 / l_ref[h])
        o = o[:, :tb] - lam_full * o[:, tb:]
        o = o * lax.rsqrt(jnp.mean(o * o, axis=0, keepdims=True) + NORM_EPS)
        o_ref[:, head_lanes(h)] = (o.T * (g_ref[...] * out_scale)).astype(o_ref.dtype)


def _attn(q, k, vt, bias, lam, subln_g, batch, seq, tb, nhead, layer_idx, bounded):
    n, d = q.shape
    ngroup = d // (LANES * nhead)
    width = LANES * nhead
    nq = seq // tb
    lambda_init = 0.8 - 0.6 * math.exp(-0.3 * layer_idx)
    q_spec = pl.BlockSpec((tb, width), lambda b, h, i: (b * nq + i, h))
    return pl.pallas_call(
        functools.partial(_attn_kernel, tb=tb, nhead=nhead, out_scale=1.0 - lambda_init,
                          lambda_init=lambda_init, bounded=bounded),
        grid=(batch, ngroup, nq),
        in_specs=[q_spec,
                  pl.BlockSpec((seq, width), lambda b, h, i: (b, h)),
                  pl.BlockSpec((nq, width, tb), lambda b, h, i: (b, h, 0)),
                  pl.BlockSpec((nhead, 2, tb, 2 * tb), lambda b, h, i: (h, 0, 0, 0)),
                  pl.BlockSpec(lam.shape, lambda b, h, i: (0, 0)),
                  pl.BlockSpec(subln_g.shape, lambda b, h, i: (0, 0))],
        out_specs=q_spec,
        out_shape=jax.ShapeDtypeStruct((n, d), BF16),
        scratch_shapes=[pltpu.VMEM((nhead, 1, 2 * tb), F32), pltpu.VMEM((nhead, 1, 2 * tb), F32),
                        pltpu.VMEM((nhead, LANES, 2 * tb), F32)],
        compiler_params=pltpu.CompilerParams(
            dimension_semantics=("arbitrary", "arbitrary", "arbitrary"),
            vmem_limit_bytes=VMEM_LIMIT),
        name="attn",
    )(q, k, vt, bias, lam, subln_g)


def kernel(x, p, norm_g, mlp_w1, mlp_w2, ple_w_up, ple_w_gate, rwkv_mix, rwkv_w_rkvo, rwkv_w0, rwkv_w1, rwkv_w2, rwkv_a0, rwkv_a1, rwkv_a2, rwkv_g1, rwkv_g2, rwkv_k_k, rwkv_k_a, rwkv_r_k, rwkv_ln_w, rwkv_ln_b, kv_norm_g, w_k_shared, w_v_shared, k_norm_g, diff_w_q, diff_q_norm_g, diff_lam, diff_subln_g, diff_w_o, rel_bias):
    batch, seq, d = x.shape
    assert d == D_MODEL and norm_g.shape[0] == 2
    n = batch * seq
    tm = min(256, seq)
    tb_rec = min(256, seq)
    tb_attn = min(256, seq)
    assert seq % tm == 0 and seq % tb_rec == 0 and seq % tb_attn == 0 and tb_attn >= MAX_DIST

    bf = lambda w: w.astype(BF16)
    x2d = x.reshape(n, d)
    p3d = p.reshape(p.shape[0], n, p.shape[-1])
    head_id = jnp.arange(d, dtype=jnp.int32) // RWKV_HEAD
    slot = jnp.arange(LANES, dtype=jnp.int32)
    bc = ((head_id[:, None] == slot[None, :] % N_HEADS) & (slot[None, :] < 2 * N_HEADS)).astype(BF16)
    be = bc.T

    vecs = jnp.stack([norm_g[0, 0], rwkv_w0[0], rwkv_a0[0], rwkv_k_k[0], rwkv_k_a[0],
                      rwkv_r_k[0].reshape(d), rwkv_ln_w[0], rwkv_ln_b[0]])
    y, g, bonus = _rwkv(
        x2d, vecs, rwkv_mix[0], bf(rwkv_w_rkvo[0, 0]), bf(rwkv_w_rkvo[0, 1]), bf(rwkv_w_rkvo[0, 2]),
        bf(rwkv_w1[0]), bf(rwkv_w2[0]), bf(rwkv_a1[0]), bf(rwkv_a2[0]), bf(rwkv_g1[0]), bf(rwkv_g2[0]),
        bc, be, batch, seq, tb_rec)
    x2d = _tail(_tail_rwkv_kernel, (x2d, y, bonus, g), p3d, 0, (vecs, bc, be), norm_g[0, 1:3],
                bf(rwkv_w_rkvo[0, 3]), bf(mlp_w1[0]), bf(mlp_w2[0]), bf(ple_w_up[0]), bf(ple_w_gate[0]), tm)

    reps = d // DIFF_HEAD
    gn = jnp.stack([norm_g[1, 0], kv_norm_g,
                    jnp.tile(diff_q_norm_g[0], reps) * (DIFF_HEAD ** -0.5 * LOG2E), jnp.tile(k_norm_g, reps)])
    q, kq, vq = _qkv(x2d, gn, bf(diff_w_q[0]), bf(w_k_shared), bf(w_v_shared), bc, be, tb_attn)
    bias = _bias_tiles(rel_bias, tb_attn)
    logit_bound = 1.02 * LOG2E * (DIFF_HEAD ** 0.5 * jnp.max(jnp.abs(diff_q_norm_g[0] * k_norm_g))
                                  + jnp.max(jnp.abs(rel_bias - rel_bias[-1:])))
    attn = functools.partial(_attn, q, kq, vq, bias, diff_lam[0], diff_subln_g[0].reshape(1, LANES),
                             batch, seq, tb_attn, 4, 1)
    o = lax.cond(logit_bound <= LOGIT_BOUND, lambda: attn(True), lambda: attn(False))
    x2d = _tail(_tail_attn_kernel, (x2d, o), p3d, 1, (), norm_g[1, 1:3],
                bf(diff_w_o[0]), bf(mlp_w1[1]), bf(mlp_w2[1]), bf(ple_w_up[1]), bf(ple_w_gate[1]), tm)
    return x2d.reshape(batch, seq, d)
```

```python
import functools
import math

import numpy as np
import jax
import jax.numpy as jnp
from jax import lax
from jax.experimental import pallas as pl
from jax.experimental.pallas import tpu as pltpu

F32 = jnp.float32
BF16 = jnp.bfloat16

D_MODEL = 1024
RWKV_HEAD = 64
N_HEADS = D_MODEL // RWKV_HEAD
DIFF_HEAD = 64
DIFF_HEADS = 8
LANES = 128
CHUNK = 64
SUB = 16
PRE_STAGE_EVERY = 5
FF_CHUNK = 1024
NORM_EPS = 1e-6
GN_EPS = 64e-5
N_BUCKETS = 32
MAX_DIST = 128
NEG_BIG = -1e30
LOG2E = math.log2(math.e)
LOGIT_BOUND = 60.0
VMEM_LIMIT = 56 * 1024 * 1024


def _dot(a, b):
    return jnp.dot(a.astype(BF16), b.astype(BF16), preferred_element_type=F32)


def _dot_nt(a, b):
    return lax.dot_general(a.astype(BF16), b.astype(BF16), (((1,), (1,)), ((), ())),
                           preferred_element_type=F32)


def _dot_tn(a, b):
    return lax.dot_general(a.astype(BF16), b.astype(BF16), (((0,), (0,)), ((), ())),
                           preferred_element_type=F32)


def _rms(x, g):
    return x * lax.rsqrt(jnp.mean(x * x, axis=-1, keepdims=True) + NORM_EPS) * g


def _sigmoid(x):
    return 1.0 / (1.0 + jnp.exp(-x))


def _head_reduce(x, bc_ref):
    return _dot(x, bc_ref[...])


def _head_expand(c, be_ref):
    lane = lax.broadcasted_iota(jnp.int32, (1, LANES), 1)
    hi = c.astype(BF16).astype(F32)
    return _dot(jnp.where(lane < N_HEADS, hi, c - hi), be_ref[...])


def _const_spec(shape):
    nd = len(shape)
    return pl.BlockSpec(shape, lambda *_: (0,) * nd, pipeline_mode=pl.Buffered(1))


def _row_spec(tm, width):
    return pl.BlockSpec((tm, width), lambda i: (i, 0))


def _pre_stages(x_ref, xp_ref, vec_ref, mix_ref, wr_ref, wk_ref, wv_ref,
                w1_ref, w2_ref, a1_ref, a2_ref, g1_ref, g2_ref, bc_ref, be_ref,
                r_out, lw_out, k_out, v_out, kk_out, a_out, g_out, bonus_out, *, first_block):
    g0 = vec_ref[0:1, :]
    h = _rms(x_ref[...], g0)
    prev = _rms(xp_ref[...], g0)[7:8, :]
    prev = jnp.where(first_block, 0.0, prev)
    row = lax.broadcasted_iota(jnp.int32, h.shape, 0)
    hprev = jnp.where(row == 0, prev, pltpu.roll(h, 1, 0))
    dx = hprev - h

    def mixed(j):
        return (h + dx * mix_ref[j:j + 1, :]).astype(BF16)

    xs = [mixed(j) for j in range(6)]
    yield
    r = _dot(xs[0], wr_ref[...])
    r_out[...] = r
    yield
    k = _dot(xs[2], wk_ref[...])
    yield
    v = _dot(xs[3], wv_ref[...])
    v_out[...] = v
    yield
    t_w = jnp.tanh(_dot(xs[1], w1_ref[...]))
    t_a = _dot(xs[4], a1_ref[...])
    t_g = _sigmoid(_dot(xs[5], g1_ref[...]))
    yield
    z = -(vec_ref[1:2, :] + _dot(t_w, w2_ref[...]))
    softplus = jnp.maximum(z, 0.0) + jnp.log1p(jnp.exp(-jnp.abs(z)))
    lw_out[...] = -jnp.exp(-softplus - 0.5)
    yield
    a = _sigmoid(vec_ref[2:3, :] + _dot(t_a, a2_ref[...]))
    a_out[...] = a
    yield
    g_out[...] = _dot(t_g, g2_ref[...]).astype(g_out.dtype)
    yield
    kk = k * vec_ref[3:4, :]
    ss = _head_reduce(kk * kk, bc_ref)
    k = k * (1.0 + (a - 1.0) * vec_ref[4:5, :])
    k_out[...] = k
    bsum = _head_reduce(r * k * vec_ref[5:6, :], bc_ref)
    yield
    kk_out[...] = kk * _head_expand(1.0 / jnp.maximum(jnp.sqrt(ss), 1e-12), be_ref)
    bonus_out[...] = (_head_expand(bsum, be_ref) * v).astype(bonus_out.dtype)


def _rec_stages(r_ref, lw_ref, k_ref, v_ref, kk_ref, a_ref, y_ref, s_ref, *, nchunk, npair, first_block):
    c2 = 2 * CHUNK
    width = npair * LANES

    @pl.when(first_block)
    def _():
        s_ref[...] = jnp.zeros_like(s_ref)

    ri = lax.broadcasted_iota(jnp.int32, (c2, c2), 0)
    ci = lax.broadcasted_iota(jnp.int32, (c2, c2), 1)
    sub_diag = (ri // SUB) == (ci // SUB)
    eye = (ri == ci).astype(F32)
    gi = lax.broadcasted_iota(jnp.int32, (2 * c2, 2 * c2), 0)
    gj = lax.broadcasted_iota(jnp.int32, (2 * c2, 2 * c2), 1)
    gram_mask = (gj % CHUNK) < (gi % CHUNK) + jnp.where(gi < c2, 0, 1)
    lo_lane = (lax.broadcasted_iota(jnp.int32, (1, width), 1) % LANES) < CHUNK
    tri = (lax.broadcasted_iota(jnp.int32, (CHUNK, CHUNK), 1)
           <= lax.broadcasted_iota(jnp.int32, (CHUNK, CHUNK), 0)).astype(BF16)
    pairs = range(npair)
    nsub = c2 // SUB
    eye_p = (lax.broadcasted_iota(jnp.int32, (SUB, c2), 0)
             == lax.broadcasted_iota(jnp.int32, (SUB, c2), 1) % SUB).astype(F32)

    def pack(m):
        return m.reshape(nsub, SUB, c2).sum(axis=0)

    def unpack(p):
        return jnp.where(sub_diag, jnp.tile(p, (nsub, 1)), 0.0)

    def per_pair(x):
        return [x[:, p * LANES:(p + 1) * LANES] for p in pairs]

    def stacked(x):
        return [jnp.concatenate([a, b], axis=0)
                for a, b in zip(per_pair(jnp.where(lo_lane, x, 0.0)), per_pair(jnp.where(lo_lane, 0.0, x)))]

    def setup(c, out):
        sl = slice(c * CHUNK, (c + 1) * CHUNK)
        lw = lw_ref[sl, :]
        l_hi = lw.astype(BF16)
        l_mid = (lw - l_hi.astype(F32)).astype(BF16)
        l_lo = (lw - l_hi.astype(F32) - l_mid.astype(F32)).astype(BF16)
        cum = _dot(tri, l_hi) + _dot(tri, l_mid) + _dot(tri, l_lo)
        tot = cum[CHUNK - 1:CHUNK, :]
        e_neg = jnp.exp(-cum)
        e_rest = jnp.exp(tot - cum)
        k = k_ref[sl, :]
        kk = kk_ref[sl, :]
        kb = kk * a_ref[sl, :]
        at = stacked(-kk * jnp.exp(cum - lw))
        rt = stacked(r_ref[sl, :] * jnp.exp(cum))
        bt = stacked(kb * e_neg)
        kt = stacked(k * e_neg)
        out["bkw"] = [jnp.concatenate([b, kq], axis=0).astype(BF16)
                      for b, kq in zip(stacked(kb * e_rest), stacked(k * e_rest))]
        out["vm"] = stacked(v_ref[sl, :])
        out["decay"] = per_pair(jnp.exp(tot))
        lhs = out["lhs"] = [jnp.concatenate([a, r], axis=0).astype(BF16) for a, r in zip(at, rt)]
        yield
        gram = out["gram"] = [jnp.where(gram_mask, _dot_nt(l, jnp.concatenate([b, kq], axis=0)), 0.0)
                              for l, b, kq in zip(lhs, bt, kt)]
        abd = [g[:c2, :c2] for g in gram]
        dg = [jnp.where(sub_diag, x, 0.0) for x in abd]
        off = [x - d for x, d in zip(abd, dg)]
        yield
        tp = [eye_p + pack(d) for d in dg]
        xp = [_dot(pack(d), d) for d in dg]
        for _ in range(2):
            yield
            res = [_dot(jnp.concatenate([ti, xi], axis=0), unpack(xi)) for ti, xi in zip(tp, xp)]
            tp = [ti + r[:SUB] for ti, r in zip(tp, res)]
            xp = [r[SUB:] for r in res]
        yield
        t = [unpack(ti + _dot(ti, unpack(xi))) for ti, xi in zip(tp, xp)]
        yield
        nn = [_dot(ti, o) for ti, o in zip(t, off)]
        yield
        res = [_dot(n, jnp.concatenate([n, ti], axis=1)) for n, ti in zip(nn, t)]
        q = [ti + r[:, c2:] for ti, r in zip(t, res)]
        yield
        out["tinv"] = [qi + _dot(r[:, :c2], qi) for qi, r in zip(q, res)]

    def carry(c, pre):
        sl = slice(c * CHUNK, (c + 1) * CHUNK)
        gram, vm = pre["gram"], pre["vm"]
        hs = [_dot_nt(l, s_ref[p]) for p, l in zip(pairs, pre["lhs"])]
        yield
        w = [h[:c2] + _dot(g[:c2, c2:], v) for h, g, v in zip(hs, gram, vm)]
        yield
        u = [_dot(ti, wi) for ti, wi in zip(pre["tinv"], w)]
        uv = [jnp.concatenate([ui, v], axis=0).astype(BF16) for ui, v in zip(u, vm)]
        yield
        ym = [h[c2:] + _dot(g[c2:], x2) for h, g, x2 in zip(hs, gram, uv)]
        y_ref[sl, :] = jnp.concatenate([yi[:CHUNK] + yi[CHUNK:] for yi in ym], axis=1)
        yield
        for p in pairs:
            s_ref[p] = s_ref[p] * pre["decay"][p] + _dot_tn(uv[p], pre["bkw"][p])

    pre = [dict() for _ in range(nchunk)]
    yield from setup(0, pre[0])
    for c in range(nchunk):
        active = [carry(c, pre[c])] + ([setup(c + 1, pre[c + 1])] if c + 1 < nchunk else [])
        while active:
            for gen in list(active):
                if next(gen, StopIteration) is StopIteration:
                    active.remove(gen)
                else:
                    yield


def _drain(gen):
    for _ in gen:
        pass


def _rwkv_kernel(*refs, nchunk, npair, nt):
    ins, (y_out, g_out, bonus_out), (r_s, lw_s, k_s, v_s, kk_s, a_s, s_ref) = refs[:15], refs[15:18], refs[18:]
    s = pl.program_id(1)
    bufs = (r_s, lw_s, k_s, v_s, kk_s, a_s)
    pre = lambda: _pre_stages(*ins, *[b.at[s % 2] for b in bufs], g_out, bonus_out, first_block=s == 0)
    rec = lambda: _rec_stages(*[b.at[(s + 1) % 2] for b in bufs], y_out, s_ref, nchunk=nchunk, npair=npair,
                              first_block=s == 1)

    @pl.when(s == 0)
    def _():
        _drain(pre())

    @pl.when((s > 0) & (s < nt))
    def _():
        side = pre()
        for i, _ in enumerate(rec()):
            if i % PRE_STAGE_EVERY == 0:
                next(side, None)
        _drain(side)

    @pl.when(s == nt)
    def _():
        _drain(rec())


def _rwkv(x2d, vecs, mix, wr, wk, wv, w1, w2, a1, a2, g1, g2, bc, be, batch, seq, tb):
    n, d = x2d.shape
    nt = seq // tb
    npair = d // LANES
    consts = (vecs, mix, wr, wk, wv, w1, w2, a1, a2, g1, g2, bc, be)
    pre_blk = lambda b, s: b * nt + jnp.minimum(s, nt - 1)
    rec_blk = lambda b, s: b * nt + jnp.maximum(s - 1, 0)
    in_spec = pl.BlockSpec((tb, d), lambda b, s: (pre_blk(b, s), 0))
    prev_spec = pl.BlockSpec((8, d), lambda b, s: (jnp.maximum(pre_blk(b, s) * (tb // 8) - 1, 0), 0))
    return pl.pallas_call(
        functools.partial(_rwkv_kernel, nchunk=tb // CHUNK, npair=npair, nt=nt),
        grid=(batch, nt + 1),
        in_specs=[in_spec, prev_spec] + [_const_spec(c.shape) for c in consts],
        out_specs=[pl.BlockSpec((tb, d), lambda b, s: (rec_blk(b, s), 0)), in_spec, in_spec],
        out_shape=[jax.ShapeDtypeStruct((n, d), F32)] + [jax.ShapeDtypeStruct((n, d), BF16)] * 2,
        scratch_shapes=[pltpu.VMEM((2, tb, d), F32)] * 6 + [pltpu.VMEM((npair, LANES, LANES), F32)],
        compiler_params=pltpu.CompilerParams(dimension_semantics=("arbitrary", "arbitrary"),
                                             vmem_limit_bytes=VMEM_LIMIT),
        name="rwkv",
    )(x2d, x2d, *consts)


def _tail_body(x, pre, p_ref, gn_ref, wo_ref, w1_ref, w2_ref, wup_ref, wgate_ref, out_ref):
    x = x + _dot(pre, wo_ref[...])
    hn = _rms(x, gn_ref[0:1, :]).astype(BF16)
    d_ff = w1_ref.shape[1]
    acc = x
    for c in range(0, d_ff, FF_CHUNK):
        mid = jnp.maximum(_dot(hn, w1_ref[:, c:c + FF_CHUNK]), 0.0)
        acc = acc + _dot(mid * mid, w2_ref[c:c + FF_CHUNK, :])
    gate = _sigmoid(_dot(_rms(acc, gn_ref[1:2, :]), wgate_ref[...]))
    out_ref[...] = acc + _dot(p_ref[...], wup_ref[...]) * gate


def _tail_attn_kernel(x_ref, o_ref, p_ref, *rest):
    _tail_body(x_ref[...], o_ref[...], p_ref, *rest)


def _tail_rwkv_kernel(x_ref, y_ref, bonus_ref, g_ref, p_ref, vec_ref, bc_ref, be_ref, *rest):
    y = y_ref[...]
    inv_n = 1.0 / RWKV_HEAD
    yc = y - _head_expand(_head_reduce(y, bc_ref) * inv_n, be_ref)
    var = _head_reduce(yc * yc, bc_ref) * inv_n
    yn = yc * _head_expand(lax.rsqrt(var + GN_EPS), be_ref) * vec_ref[6:7, :] + vec_ref[7:8, :]
    _tail_body(x_ref[...], (yn + bonus_ref[...]) * g_ref[...], p_ref, *rest)


def _tail(kernel_fn, rows, p3d, layer, extra_consts, gn, wo, w1, w2, wup, wgate, tm):
    n, d = rows[0].shape
    consts = tuple(extra_consts) + (gn, wo, w1, w2, wup, wgate)
    return pl.pallas_call(
        kernel_fn,
        grid=(n // tm,),
        in_specs=[_row_spec(tm, d)] * len(rows)
                 + [pl.BlockSpec((None, tm, p3d.shape[2]), lambda i: (layer, i, 0))]
                 + [_const_spec(c.shape) for c in consts],
        out_specs=_row_spec(tm, d),
        out_shape=jax.ShapeDtypeStruct((n, d), F32),
        compiler_params=pltpu.CompilerParams(dimension_semantics=("arbitrary",),
                                             vmem_limit_bytes=VMEM_LIMIT),
        name="tail",
    )(*rows, p3d, *consts)


def _qkv_kernel(x_ref, gn_ref, wq_ref, wk_ref, wv_ref, bc_ref, be_ref, q_out, k_out, v_out):
    x = x_ref[...]
    inv_n = 1.0 / DIFF_HEAD

    def head_rms(t, g):
        ms = _head_reduce(t * t, bc_ref) * inv_n
        return t * _head_expand(lax.rsqrt(ms + NORM_EPS), be_ref) * g

    hq = _rms(x, gn_ref[0:1, :])
    hk = _rms(x, gn_ref[1:2, :]).astype(BF16)
    q = head_rms(_dot(hq, wq_ref[...]), gn_ref[2:3, :])
    q_out[...] = q.astype(q_out.dtype)
    k_out[...] = head_rms(_dot(hk, wk_ref[...]), gn_ref[3:4, :]).astype(k_out.dtype)
    v_out[0] = _dot(hk, wv_ref[...]).T.astype(v_out.dtype)


def _qkv(x2d, gn, wq, wk, wv, bc, be, tm):
    n, d = x2d.shape
    consts = (gn, wq, wk, wv, bc, be)
    out_sd = jax.ShapeDtypeStruct((n, d), BF16)
    return pl.pallas_call(
        _qkv_kernel,
        grid=(n // tm,),
        in_specs=[_row_spec(tm, d)] + [_const_spec(c.shape) for c in consts],
        out_specs=[_row_spec(tm, d)] * 2 + [pl.BlockSpec((1, d, tm), lambda i: (i, 0, 0))],
        out_shape=[out_sd] * 2 + [jax.ShapeDtypeStruct((n // tm, d, tm), BF16)],
        compiler_params=pltpu.CompilerParams(dimension_semantics=("arbitrary",),
                                             vmem_limit_bytes=VMEM_LIMIT),
        name="qkv",
    )(x2d, *consts)


def _bucket_tiles(tb):
    i = np.arange(tb, dtype=np.int64)[None, :]
    j = np.arange(tb, dtype=np.int64)[:, None]

    def bucket(rel):
        n = np.maximum(rel, 0)
        max_exact = N_BUCKETS // 2
        nf = np.maximum(n, 1).astype(np.float32)
        large = max_exact + (np.log(nf / np.float32(max_exact)) / np.float32(math.log(MAX_DIST / max_exact))
                             * np.float32(N_BUCKETS - max_exact)).astype(np.int32)
        large = np.minimum(large, N_BUCKETS - 1)
        return np.where(n < max_exact, n, large).astype(np.int32)

    diag = np.where(i - j >= 0, bucket(i - j), -1)
    near = bucket(tb + i - j)
    tiles = np.stack([diag, near]).astype(np.int32)
    return np.concatenate([tiles, tiles], axis=2)


def _bias_kernel(tab_ref, bucket_ref, out_ref):
    h = pl.program_id(0)
    b = bucket_ref[...]
    far = tab_ref[N_BUCKETS - 1, h]
    acc = jnp.where(b < 0, NEG_BIG, 0.0)
    for n in range(N_BUCKETS - 1):
        acc = jnp.where(b == n, (tab_ref[n, h] - far) * LOG2E, acc)
    out_ref[0] = acc


def _bias_tiles(rel_bias, tb):
    buckets = jnp.asarray(_bucket_tiles(tb))
    nh = rel_bias.shape[1]
    return pl.pallas_call(
        _bias_kernel,
        grid=(nh,),
        in_specs=[pl.BlockSpec(memory_space=pltpu.SMEM),
                  pl.BlockSpec((2, tb, 2 * tb), lambda h: (0, 0, 0))],
        out_specs=pl.BlockSpec((1, 2, tb, 2 * tb), lambda h: (h, 0, 0, 0)),
        out_shape=jax.ShapeDtypeStruct((nh, 2, tb, 2 * tb), F32),
        compiler_params=pltpu.CompilerParams(dimension_semantics=("arbitrary",)),
        name="bias",
    )(rel_bias, buckets)


def _attn_kernel(q_ref, k_ref, vt_ref, bias_ref, lam_ref, g_ref, o_ref, m_ref, l_ref, acc_ref,
                 *, tb, nhead, out_scale, lambda_init, bounded):
    qi = pl.program_id(2)
    heads = range(nhead)
    lane = lax.broadcasted_iota(jnp.int32, (1, LANES), 1)

    def head_lanes(h):
        return slice(h * LANES, (h + 1) * LANES)

    def stacked_q(h):
        q = q_ref[:, head_lanes(h)]
        zero = jnp.zeros_like(q)
        return jnp.concatenate([jnp.where(lane < DIFF_HEAD, q, zero),
                                jnp.where(lane < DIFF_HEAD, zero, q)], axis=0)

    qs = [stacked_q(h) for h in heads]
    m_ref[...] = jnp.full_like(m_ref, NEG_BIG)
    l_ref[...] = jnp.zeros_like(l_ref)
    acc_ref[...] = jnp.zeros_like(acc_ref)

    def advance(j, biases):
        nblk = len(biases)
        rows = pl.ds(pl.multiple_of(j * tb, tb), nblk * tb)
        logits = lambda h: _dot_nt(k_ref[rows, head_lanes(h)], qs[h])
        ahead = min(2, nhead)
        st = {h: logits(h) for h in range(ahead)}
        spans = []
        for i, b in enumerate(biases):
            if b is None and spans and spans[-1][2] is None:
                spans[-1] = (spans[-1][0], i + 1, None)
            else:
                spans.append((i, i + 1, b))
        for h in heads:
            if h + ahead < nhead:
                st[h + ahead] = logits(h + ahead)
            s = st.pop(h)
            parts = [s[i * tb:e * tb] if b is None else s[i * tb:e * tb] + bias_ref[h, b] for i, e, b in spans]
            if bounded:
                ps = [jnp.exp2(x) for x in parts]
                l_new = l_ref[h]
            else:
                m_old = m_ref[h]
                m_new = m_old
                for x in parts:
                    m_new = jnp.maximum(m_new, jnp.max(x, axis=0, keepdims=True))
                alpha = jnp.exp2(m_old - m_new)
                ps = [jnp.exp2(x - m_new) for x in parts]
                l_new = alpha * l_ref[h]
                m_ref[h] = m_new
            for p in ps:
                l_new = l_new + jnp.sum(p, axis=0, keepdims=True)
            l_ref[h] = l_new
            pv = None
            for (i, e, _), p in zip(spans, ps):
                p = p.astype(BF16)
                for b in range(i, e):
                    term = _dot(vt_ref[j + b, head_lanes(h), :], p[(b - i) * tb:(b - i + 1) * tb])
                    pv = term if pv is None else pv + term
            acc_ref[h] = acc_ref[h] + pv if bounded else alpha * acc_ref[h] + pv

    def far_pair(t, carry):
        advance(2 * t, (None, None))
        return carry

    nfar = jnp.maximum(qi - 1, 0)
    lax.fori_loop(0, nfar // 2, far_pair, 0)

    @pl.when(qi == 0)
    def _():
        advance(0, (0,))

    @pl.when((qi >= 1) & (nfar % 2 == 0))
    def _():
        advance(qi - 1, (1, 0))

    @pl.when(nfar % 2 == 1)
    def _():
        advance(qi - 2, (None, 1, 0))

    lam = lam_ref[...]
    lam_full = (jnp.exp(jnp.sum(lam[0:1] * lam[1:2], axis=1, keepdims=True))
                - jnp.exp(jnp.sum(lam[2:3] * lam[3:4], axis=1, keepdims=True)) + lambda_init)
    for h in heads:
        o = acc_ref[h] * (1.0 / l_ref[h])
        o = o[:, :tb] - lam_full * o[:, tb:]
        o = o * lax.rsqrt(jnp.mean(o * o, axis=0, keepdims=True) + NORM_EPS)
        o_ref[:, head_lanes(h)] = (o.T * (g_ref[...] * out_scale)).astype(o_ref.dtype)


def _attn(q, k, vt, bias, lam, subln_g, batch, seq, tb, nhead, layer_idx, bounded):
    n, d = q.shape
    ngroup = d // (LANES * nhead)
    width = LANES * nhead
    nq = seq // tb
    lambda_init = 0.8 - 0.6 * math.exp(-0.3 * layer_idx)
    q_spec = pl.BlockSpec((tb, width), lambda b, h, i: (b * nq + i, h))
    return pl.pallas_call(
        functools.partial(_attn_kernel, tb=tb, nhead=nhead, out_scale=1.0 - lambda_init,
                          lambda_init=lambda_init, bounded=bounded),
        grid=(batch, ngroup, nq),
        in_specs=[q_spec,
                  pl.BlockSpec((seq, width), lambda b, h, i: (b, h)),
                  pl.BlockSpec((nq, width, tb), lambda b, h, i: (b, h, 0)),
                  pl.BlockSpec((nhead, 2, tb, 2 * tb), lambda b, h, i: (h, 0, 0, 0)),
                  pl.BlockSpec(lam.shape, lambda b, h, i: (0, 0)),
                  pl.BlockSpec(subln_g.shape, lambda b, h, i: (0, 0))],
        out_specs=q_spec,
        out_shape=jax.ShapeDtypeStruct((n, d), BF16),
        scratch_shapes=[pltpu.VMEM((nhead, 1, 2 * tb), F32), pltpu.VMEM((nhead, 1, 2 * tb), F32),
                        pltpu.VMEM((nhead, LANES, 2 * tb), F32)],
        compiler_params=pltpu.CompilerParams(
            dimension_semantics=("arbitrary", "arbitrary", "arbitrary"),
            vmem_limit_bytes=VMEM_LIMIT),
        name="attn",
    )(q, k, vt, bias, lam, subln_g)


def kernel(x, p, norm_g, mlp_w1, mlp_w2, ple_w_up, ple_w_gate, rwkv_mix, rwkv_w_rkvo, rwkv_w0, rwkv_w1, rwkv_w2, rwkv_a0, rwkv_a1, rwkv_a2, rwkv_g1, rwkv_g2, rwkv_k_k, rwkv_k_a, rwkv_r_k, rwkv_ln_w, rwkv_ln_b, kv_norm_g, w_k_shared, w_v_shared, k_norm_g, diff_w_q, diff_q_norm_g, diff_lam, diff_subln_g, diff_w_o, rel_bias):
    batch, seq, d = x.shape
    assert d == D_MODEL and norm_g.shape[0] == 2
    n = batch * seq
    tm = min(256, seq)
    tb_rec = min(256, seq)
    tb_attn = min(256, seq)
    assert seq % tm == 0 and seq % tb_rec == 0 and seq % tb_attn == 0 and tb_attn >= MAX_DIST

    bf = lambda w: w.astype(BF16)
    x2d = x.reshape(n, d)
    p3d = p.reshape(p.shape[0], n, p.shape[-1])
    head_id = jnp.arange(d, dtype=jnp.int32) // RWKV_HEAD
    slot = jnp.arange(LANES, dtype=jnp.int32)
    bc = ((head_id[:, None] == slot[None, :] % N_HEADS) & (slot[None, :] < 2 * N_HEADS)).astype(BF16)
    be = bc.T

    vecs = jnp.stack([norm_g[0, 0], rwkv_w0[0], rwkv_a0[0], rwkv_k_k[0], rwkv_k_a[0],
                      rwkv_r_k[0].reshape(d), rwkv_ln_w[0], rwkv_ln_b[0]])
    y, g, bonus = _rwkv(
        x2d, vecs, rwkv_mix[0], bf(rwkv_w_rkvo[0, 0]), bf(rwkv_w_rkvo[0, 1]), bf(rwkv_w_rkvo[0, 2]),
        bf(rwkv_w1[0]), bf(rwkv_w2[0]), bf(rwkv_a1[0]), bf(rwkv_a2[0]), bf(rwkv_g1[0]), bf(rwkv_g2[0]),
        bc, be, batch, seq, tb_rec)
    x2d = _tail(_tail_rwkv_kernel, (x2d, y, bonus, g), p3d, 0, (vecs, bc, be), norm_g[0, 1:3],
                bf(rwkv_w_rkvo[0, 3]), bf(mlp_w1[0]), bf(mlp_w2[0]), bf(ple_w_up[0]), bf(ple_w_gate[0]), tm)

    reps = d // DIFF_HEAD
    gn = jnp.stack([norm_g[1, 0], kv_norm_g,
                    jnp.tile(diff_q_norm_g[0], reps) * (DIFF_HEAD ** -0.5 * LOG2E), jnp.tile(k_norm_g, reps)])
    q, kq, vq = _qkv(x2d, gn, bf(diff_w_q[0]), bf(w_k_shared), bf(w_v_shared), bc, be, tb_attn)
    bias = _bias_tiles(rel_bias, tb_attn)
    logit_bound = 1.02 * LOG2E * (DIFF_HEAD ** 0.5 * jnp.max(jnp.abs(diff_q_norm_g[0] * k_norm_g))
                                  + jnp.max(jnp.abs(rel_bias - rel_bias[-1:])))
    attn = functools.partial(_attn, q, kq, vq, bias, diff_lam[0], diff_subln_g[0].reshape(1, LANES),
                             batch, seq, tb_attn, 4, 1)
    o = lax.cond(logit_bound <= LOGIT_BOUND, lambda: attn(True), lambda: attn(False))
    x2d = _tail(_tail_attn_kernel, (x2d, o), p3d, 1, (), norm_g[1, 1:3],
                bf(diff_w_o[0]), bf(mlp_w1[1]), bf(mlp_w2[1]), bf(ple_w_up[1]), bf(ple_w_gate[1]), tm)
    return x2d.reshape(batch, seq, d)
```

```python
import functools
import math

import numpy as np
import jax
import jax.numpy as jnp
from jax import lax
from jax.experimental import pallas as pl
from jax.experimental.pallas import tpu as pltpu

F32 = jnp.float32
BF16 = jnp.bfloat16

D_MODEL = 1024
RWKV_HEAD = 64
N_HEADS = D_MODEL // RWKV_HEAD
DIFF_HEAD = 64
DIFF_HEADS = 8
LANES = 128
CHUNK = 64
SUB = 16
PRE_STAGE_EVERY = 5
FF_CHUNK = 1024
NORM_EPS = 1e-6
GN_EPS = 64e-5
N_BUCKETS = 32
MAX_DIST = 128
NEG_BIG = -1e30
LOG2E = math.log2(math.e)
LOGIT_BOUND = 60.0
VMEM_LIMIT = 56 * 1024 * 1024


def _dot(a, b):
    return jnp.dot(a.astype(BF16), b.astype(BF16), preferred_element_type=F32)


def _dot_nt(a, b):
    return lax.dot_general(a.astype(BF16), b.astype(BF16), (((1,), (1,)), ((), ())),
                           preferred_element_type=F32)


def _dot_tn(a, b):
    return lax.dot_general(a.astype(BF16), b.astype(BF16), (((0,), (0,)), ((), ())),
                           preferred_element_type=F32)


def _rms(x, g):
    return x * lax.rsqrt(jnp.mean(x * x, axis=-1, keepdims=True) + NORM_EPS) * g


def _sigmoid(x):
    return 1.0 / (1.0 + jnp.exp(-x))


def _head_reduce(x, bc_ref):
    return _dot(x, bc_ref[...])


def _head_expand(c, be_ref):
    lane = lax.broadcasted_iota(jnp.int32, (1, LANES), 1)
    hi = c.astype(BF16).astype(F32)
    return _dot(jnp.where(lane < N_HEADS, hi, c - hi), be_ref[...])


def _const_spec(shape):
    nd = len(shape)
    return pl.BlockSpec(shape, lambda *_: (0,) * nd, pipeline_mode=pl.Buffered(1))


def _row_spec(tm, width):
    return pl.BlockSpec((tm, width), lambda i: (i, 0))


def _pre_stages(x_ref, xp_ref, vec_ref, mix_ref, wr_ref, wk_ref, wv_ref,
                w1_ref, w2_ref, a1_ref, a2_ref, g1_ref, g2_ref, bc_ref, be_ref,
                r_out, lw_out, k_out, v_out, kk_out, a_out, g_out, bonus_out, *, first_block):
    g0 = vec_ref[0:1, :]
    h = _rms(x_ref[...], g0)
    prev = _rms(xp_ref[...], g0)[7:8, :]
    prev = jnp.where(first_block, 0.0, prev)
    row = lax.broadcasted_iota(jnp.int32, h.shape, 0)
    hprev = jnp.where(row == 0, prev, pltpu.roll(h, 1, 0))
    dx = hprev - h

    def mixed(j):
        return (h + dx * mix_ref[j:j + 1, :]).astype(BF16)

    yield
    r = _dot(mixed(0), wr_ref[...])
    r_out[...] = r
    yield
    k = _dot(mixed(2), wk_ref[...])
    yield
    v = _dot(mixed(3), wv_ref[...])
    v_out[...] = v
    yield
    t_w = jnp.tanh(_dot(mixed(1), w1_ref[...]))
    yield
    t_a = _dot(mixed(4), a1_ref[...])
    t_g = _sigmoid(_dot(mixed(5), g1_ref[...]))
    yield
    lw_out[...] = -math.exp(-0.5) * _sigmoid(vec_ref[1:2, :] + _dot(t_w, w2_ref[...]))
    yield
    a = _sigmoid(vec_ref[2:3, :] + _dot(t_a, a2_ref[...]))
    a_out[...] = a
    yield
    g_out[...] = _dot(t_g, g2_ref[...]).astype(g_out.dtype)
    yield
    kk = k * vec_ref[3:4, :]
    ss = _head_reduce(kk * kk, bc_ref)
    k = k * (1.0 + (a - 1.0) * vec_ref[4:5, :])
    k_out[...] = k
    bsum = _head_reduce(r * k * vec_ref[5:6, :], bc_ref)
    yield
    kk_out[...] = kk * _head_expand(1.0 / jnp.maximum(jnp.sqrt(ss), 1e-12), be_ref)
    bonus_out[...] = (_head_expand(bsum, be_ref) * v).astype(bonus_out.dtype)


def _rec_stages(r_ref, lw_ref, k_ref, v_ref, kk_ref, a_ref, y_ref, s_ref, *, nchunk, npair, first_block):
    c2 = 2 * CHUNK
    width = npair * LANES

    @pl.when(first_block)
    def _():
        s_ref[...] = jnp.zeros_like(s_ref)

    ri = lax.broadcasted_iota(jnp.int32, (c2, c2), 0)
    ci = lax.broadcasted_iota(jnp.int32, (c2, c2), 1)
    sub_diag = (ri // SUB) == (ci // SUB)
    eye = (ri == ci).astype(F32)
    gi = lax.broadcasted_iota(jnp.int32, (2 * c2, 2 * c2), 0)
    gj = lax.broadcasted_iota(jnp.int32, (2 * c2, 2 * c2), 1)
    gram_mask = (gj % CHUNK) < (gi % CHUNK) + jnp.where(gi < c2, 0, 1)
    lo_lane = (lax.broadcasted_iota(jnp.int32, (1, width), 1) % LANES) < CHUNK
    tri = (lax.broadcasted_iota(jnp.int32, (CHUNK, CHUNK), 1)
           <= lax.broadcasted_iota(jnp.int32, (CHUNK, CHUNK), 0)).astype(BF16)
    pairs = range(npair)
    nsub = c2 // SUB
    eye_p = (lax.broadcasted_iota(jnp.int32, (SUB, c2), 0)
             == lax.broadcasted_iota(jnp.int32, (SUB, c2), 1) % SUB).astype(F32)

    def pack(m):
        return m.reshape(nsub, SUB, c2).sum(axis=0)

    def unpack(p):
        return jnp.where(sub_diag, jnp.tile(p, (nsub, 1)), 0.0)

    def per_pair(x):
        return [x[:, p * LANES:(p + 1) * LANES] for p in pairs]

    def stacked(x):
        return [jnp.concatenate([a, b], axis=0)
                for a, b in zip(per_pair(jnp.where(lo_lane, x, 0.0)), per_pair(jnp.where(lo_lane, 0.0, x)))]

    def setup(c, out):
        sl = slice(c * CHUNK, (c + 1) * CHUNK)
        lw = lw_ref[sl, :]
        l_hi = lw.astype(BF16)
        l_mid = (lw - l_hi.astype(F32)).astype(BF16)
        l_lo = (lw - l_hi.astype(F32) - l_mid.astype(F32)).astype(BF16)
        cum = _dot(tri, l_hi) + _dot(tri, l_mid) + _dot(tri, l_lo)
        tot = cum[CHUNK - 1:CHUNK, :]
        e_neg = jnp.exp(-cum)
        e_rest = jnp.exp(tot - cum)
        k = k_ref[sl, :]
        kk = kk_ref[sl, :]
        kb = kk * a_ref[sl, :]
        at = stacked(-kk * jnp.exp(cum - lw))
        rt = stacked(r_ref[sl, :] * jnp.exp(cum))
        bt = stacked(kb * e_neg)
        kt = stacked(k * e_neg)
        out["bkw"] = [jnp.concatenate([b, kq], axis=0).astype(BF16)
                      for b, kq in zip(stacked(kb * e_rest), stacked(k * e_rest))]
        out["vm"] = stacked(v_ref[sl, :])
        out["decay"] = per_pair(jnp.exp(tot))
        lhs = out["lhs"] = [jnp.concatenate([a, r], axis=0).astype(BF16) for a, r in zip(at, rt)]
        yield
        gram = out["gram"] = [jnp.where(gram_mask, _dot_nt(l, jnp.concatenate([b, kq], axis=0)), 0.0)
                              for l, b, kq in zip(lhs, bt, kt)]
        abd = [g[:c2, :c2] for g in gram]
        dg = [jnp.where(sub_diag, x, 0.0) for x in abd]
        off = [x - d for x, d in zip(abd, dg)]
        yield
        tp = [eye_p + pack(d) for d in dg]
        xp = [_dot(pack(d), d) for d in dg]
        for _ in range(2):
            yield
            res = [_dot(jnp.concatenate([ti, xi], axis=0), unpack(xi)) for ti, xi in zip(tp, xp)]
            tp = [ti + r[:SUB] for ti, r in zip(tp, res)]
            xp = [r[SUB:] for r in res]
        yield
        t = [unpack(ti + _dot(ti, unpack(xi))) for ti, xi in zip(tp, xp)]
        yield
        nn = [_dot(ti, o) for ti, o in zip(t, off)]
        yield
        res = [_dot(n, jnp.concatenate([n, ti], axis=1)) for n, ti in zip(nn, t)]
        q = [ti + r[:, c2:] for ti, r in zip(t, res)]
        yield
        out["tinv"] = [qi + _dot(r[:, :c2], qi) for qi, r in zip(q, res)]

    def carry(c, pre):
        sl = slice(c * CHUNK, (c + 1) * CHUNK)
        gram, vm = pre["gram"], pre["vm"]
        hs = [_dot_nt(l, s_ref[p]) for p, l in zip(pairs, pre["lhs"])]
        yield
        w = [h[:c2] + _dot(g[:c2, c2:], v) for h, g, v in zip(hs, gram, vm)]
        yield
        u = [_dot(ti, wi) for ti, wi in zip(pre["tinv"], w)]
        uv = [jnp.concatenate([ui, v], axis=0).astype(BF16) for ui, v in zip(u, vm)]
        yield
        ym = [h[c2:] + _dot(g[c2:], x2) for h, g, x2 in zip(hs, gram, uv)]
        y_ref[sl, :] = jnp.concatenate([yi[:CHUNK] + yi[CHUNK:] for yi in ym], axis=1)
        yield
        for p in pairs:
            s_ref[p] = s_ref[p] * pre["decay"][p] + _dot_tn(uv[p], pre["bkw"][p])

    pre = [dict() for _ in range(nchunk)]
    yield from setup(0, pre[0])
    for c in range(nchunk):
        active = [carry(c, pre[c])] + ([setup(c + 1, pre[c + 1])] if c + 1 < nchunk else [])
        while active:
            for gen in list(active):
                if next(gen, StopIteration) is StopIteration:
                    active.remove(gen)
                else:
                    yield


def _drain(gen):
    for _ in gen:
        pass


def _rwkv_kernel(*refs, nchunk, npair, nt):
    ins, (y_out, g_out, bonus_out), (r_s, lw_s, k_s, v_s, kk_s, a_s, s_ref) = refs[:15], refs[15:18], refs[18:]
    s = pl.program_id(1)
    bufs = (r_s, lw_s, k_s, v_s, kk_s, a_s)
    pre = lambda: _pre_stages(*ins, *[b.at[s % 2] for b in bufs], g_out, bonus_out, first_block=s == 0)
    rec = lambda: _rec_stages(*[b.at[(s + 1) % 2] for b in bufs], y_out, s_ref, nchunk=nchunk, npair=npair,
                              first_block=s == 1)

    @pl.when(s == 0)
    def _():
        _drain(pre())

    @pl.when((s > 0) & (s < nt))
    def _():
        side = pre()
        for i, _ in enumerate(rec()):
            if i % PRE_STAGE_EVERY == 0:
                next(side, None)
        _drain(side)

    @pl.when(s == nt)
    def _():
        _drain(rec())


def _rwkv(x2d, vecs, mix, wr, wk, wv, w1, w2, a1, a2, g1, g2, bc, be, batch, seq, tb):
    n, d = x2d.shape
    nt = seq // tb
    npair = d // LANES
    consts = (vecs, mix, wr, wk, wv, w1, w2, a1, a2, g1, g2, bc, be)
    pre_blk = lambda b, s: b * nt + jnp.minimum(s, nt - 1)
    rec_blk = lambda b, s: b * nt + jnp.maximum(s - 1, 0)
    in_spec = pl.BlockSpec((tb, d), lambda b, s: (pre_blk(b, s), 0))
    prev_spec = pl.BlockSpec((8, d), lambda b, s: (jnp.maximum(pre_blk(b, s) * (tb // 8) - 1, 0), 0))
    return pl.pallas_call(
        functools.partial(_rwkv_kernel, nchunk=tb // CHUNK, npair=npair, nt=nt),
        grid=(batch, nt + 1),
        in_specs=[in_spec, prev_spec] + [_const_spec(c.shape) for c in consts],
        out_specs=[pl.BlockSpec((tb, d), lambda b, s: (rec_blk(b, s), 0)), in_spec, in_spec],
        out_shape=[jax.ShapeDtypeStruct((n, d), F32)] + [jax.ShapeDtypeStruct((n, d), BF16)] * 2,
        scratch_shapes=[pltpu.VMEM((2, tb, d), F32)] * 6 + [pltpu.VMEM((npair, LANES, LANES), F32)],
        compiler_params=pltpu.CompilerParams(dimension_semantics=("arbitrary", "arbitrary"),
                                             vmem_limit_bytes=VMEM_LIMIT),
        name="rwkv",
    )(x2d, x2d, *consts)


def _tail_body(x, pre, p_ref, gn_ref, wo_ref, w1_ref, w2_ref, wup_ref, wgate_ref, out_ref):
    x = x + _dot(pre, wo_ref[...])
    hn = _rms(x, gn_ref[0:1, :]).astype(BF16)
    d_ff = w1_ref.shape[1]
    acc = x
    for c in range(0, d_ff, FF_CHUNK):
        mid = jnp.maximum(_dot(hn, w1_ref[:, c:c + FF_CHUNK]), 0.0)
        acc = acc + _dot(mid * mid, w2_ref[c:c + FF_CHUNK, :])
    gate = _sigmoid(_dot(_rms(acc, gn_ref[1:2, :]), wgate_ref[...]))
    out_ref[...] = acc + _dot(p_ref[...], wup_ref[...]) * gate


def _tail_attn_kernel(x_ref, o_ref, p_ref, *rest):
    _tail_body(x_ref[...], o_ref[...], p_ref, *rest)


def _tail_rwkv_kernel(x_ref, y_ref, bonus_ref, g_ref, p_ref, vec_ref, bc_ref, be_ref, *rest):
    y = y_ref[...]
    inv_n = 1.0 / RWKV_HEAD
    yc = y - _head_expand(_head_reduce(y, bc_ref) * inv_n, be_ref)
    var = _head_reduce(yc * yc, bc_ref) * inv_n
    yn = yc * _head_expand(lax.rsqrt(var + GN_EPS), be_ref) * vec_ref[6:7, :] + vec_ref[7:8, :]
    _tail_body(x_ref[...], (yn + bonus_ref[...]) * g_ref[...], p_ref, *rest)


def _tail(kernel_fn, rows, p3d, layer, extra_consts, gn, wo, w1, w2, wup, wgate, tm):
    n, d = rows[0].shape
    consts = tuple(extra_consts) + (gn, wo, w1, w2, wup, wgate)
    return pl.pallas_call(
        kernel_fn,
        grid=(n // tm,),
        in_specs=[_row_spec(tm, d)] * len(rows)
                 + [pl.BlockSpec((None, tm, p3d.shape[2]), lambda i: (layer, i, 0))]
                 + [_const_spec(c.shape) for c in consts],
        out_specs=_row_spec(tm, d),
        out_shape=jax.ShapeDtypeStruct((n, d), F32),
        compiler_params=pltpu.CompilerParams(dimension_semantics=("arbitrary",),
                                             vmem_limit_bytes=VMEM_LIMIT),
        name="tail",
    )(*rows, p3d, *consts)


def _qkv_kernel(x_ref, gn_ref, wq_ref, wk_ref, wv_ref, bc_ref, be_ref, q_out, k_out, v_out):
    x = x_ref[...]
    inv_n = 1.0 / DIFF_HEAD

    def head_rms(t, g):
        ms = _head_reduce(t * t, bc_ref) * inv_n
        return t * _head_expand(lax.rsqrt(ms + NORM_EPS), be_ref) * g

    hq = _rms(x, gn_ref[0:1, :])
    hk = _rms(x, gn_ref[1:2, :]).astype(BF16)
    q = head_rms(_dot(hq, wq_ref[...]), gn_ref[2:3, :])
    q_out[...] = q.astype(q_out.dtype)
    k_out[...] = head_rms(_dot(hk, wk_ref[...]), gn_ref[3:4, :]).astype(k_out.dtype)
    v_out[0] = _dot(hk, wv_ref[...]).T.astype(v_out.dtype)


def _qkv(x2d, gn, wq, wk, wv, bc, be, tm):
    n, d = x2d.shape
    consts = (gn, wq, wk, wv, bc, be)
    out_sd = jax.ShapeDtypeStruct((n, d), BF16)
    return pl.pallas_call(
        _qkv_kernel,
        grid=(n // tm,),
        in_specs=[_row_spec(tm, d)] + [_const_spec(c.shape) for c in consts],
        out_specs=[_row_spec(tm, d)] * 2 + [pl.BlockSpec((1, d, tm), lambda i: (i, 0, 0))],
        out_shape=[out_sd] * 2 + [jax.ShapeDtypeStruct((n // tm, d, tm), BF16)],
        compiler_params=pltpu.CompilerParams(dimension_semantics=("arbitrary",),
                                             vmem_limit_bytes=VMEM_LIMIT),
        name="qkv",
    )(x2d, *consts)


def _bucket_tiles(tb):
    i = np.arange(tb, dtype=np.int64)[None, :]
    j = np.arange(tb, dtype=np.int64)[:, None]

    def bucket(rel):
        n = np.maximum(rel, 0)
        max_exact = N_BUCKETS // 2
        nf = np.maximum(n, 1).astype(np.float32)
        large = max_exact + (np.log(nf / np.float32(max_exact)) / np.float32(math.log(MAX_DIST / max_exact))
                             * np.float32(N_BUCKETS - max_exact)).astype(np.int32)
        large = np.minimum(large, N_BUCKETS - 1)
        return np.where(n < max_exact, n, large).astype(np.int32)

    diag = np.where(i - j >= 0, bucket(i - j), -1)
    near = bucket(tb + i - j)
    tiles = np.stack([diag, near]).astype(np.int32)
    return np.concatenate([tiles, tiles], axis=2)


def _bias_kernel(tab_ref, bucket_ref, out_ref):
    h = pl.program_id(0)
    b = bucket_ref[...]
    far = tab_ref[N_BUCKETS - 1, h]
    acc = jnp.where(b < 0, NEG_BIG, 0.0)
    for n in range(N_BUCKETS - 1):
        acc = jnp.where(b == n, (tab_ref[n, h] - far) * LOG2E, acc)
    out_ref[0] = acc


def _bias_tiles(rel_bias, tb):
    buckets = jnp.asarray(_bucket_tiles(tb))
    nh = rel_bias.shape[1]
    return pl.pallas_call(
        _bias_kernel,
        grid=(nh,),
        in_specs=[pl.BlockSpec(memory_space=pltpu.SMEM),
                  pl.BlockSpec((2, tb, 2 * tb), lambda h: (0, 0, 0))],
        out_specs=pl.BlockSpec((1, 2, tb, 2 * tb), lambda h: (h, 0, 0, 0)),
        out_shape=jax.ShapeDtypeStruct((nh, 2, tb, 2 * tb), F32),
        compiler_params=pltpu.CompilerParams(dimension_semantics=("arbitrary",)),
        name="bias",
    )(rel_bias, buckets)


def _attn_kernel(q_ref, k_ref, vt_ref, bias_ref, lam_ref, g_ref, o_ref, m_ref, l_ref, acc_ref,
                 *, tb, nhead, out_scale, lambda_init, bounded):
    qi = pl.program_id(2)
    heads = range(nhead)
    lane = lax.broadcasted_iota(jnp.int32, (1, LANES), 1)

    def head_lanes(h):
        return slice(h * LANES, (h + 1) * LANES)

    def stacked_q(h):
        q = q_ref[:, head_lanes(h)]
        zero = jnp.zeros_like(q)
        return jnp.concatenate([jnp.where(lane < DIFF_HEAD, q, zero),
                                jnp.where(lane < DIFF_HEAD, zero, q)], axis=0)

    qs = [stacked_q(h) for h in heads]
    m_ref[...] = jnp.full_like(m_ref, NEG_BIG)
    l_ref[...] = jnp.zeros_like(l_ref)
    acc_ref[...] = jnp.zeros_like(acc_ref)

    def advance(j, biases):
        nblk = len(biases)
        rows = pl.ds(pl.multiple_of(j * tb, tb), nblk * tb)
        logits = lambda h: _dot_nt(k_ref[rows, head_lanes(h)], qs[h])
        ahead = min(2, nhead)
        st = {h: logits(h) for h in range(ahead)}
        spans = []
        for i, b in enumerate(biases):
            if b is None and spans and spans[-1][2] is None:
                spans[-1] = (spans[-1][0], i + 1, None)
            else:
                spans.append((i, i + 1, b))
        for h in heads:
            if h + ahead < nhead:
                st[h + ahead] = logits(h + ahead)
            s = st.pop(h)
            parts = [s[i * tb:e * tb] if b is None else s[i * tb:e * tb] + bias_ref[h, b] for i, e, b in spans]
            if bounded:
                ps = [jnp.exp2(x) for x in parts]
                l_new = l_ref[h]
            else:
                m_old = m_ref[h]
                m_new = m_old
                for x in parts:
                    m_new = jnp.maximum(m_new, jnp.max(x, axis=0, keepdims=True))
                alpha = jnp.exp2(m_old - m_new)
                ps = [jnp.exp2(x - m_new) for x in parts]
                l_new = alpha * l_ref[h]
                m_ref[h] = m_new
            for p in ps:
                l_new = l_new + jnp.sum(p, axis=0, keepdims=True)
            l_ref[h] = l_new
            pv = None
            for (i, e, _), p in zip(spans, ps):
                p = p.astype(BF16)
                for b in range(i, e):
                    term = _dot(vt_ref[j + b, head_lanes(h), :], p[(b - i) * tb:(b - i + 1) * tb])
                    pv = term if pv is None else pv + term
            acc_ref[h] = acc_ref[h] + pv if bounded else alpha * acc_ref[h] + pv

    def far_pair(t, carry):
        advance(2 * t, (None, None))
        return carry

    nfar = jnp.maximum(qi - 1, 0)
    lax.fori_loop(0, nfar // 2, far_pair, 0)

    @pl.when(qi == 0)
    def _():
        advance(0, (0,))

    @pl.when((qi >= 1) & (nfar % 2 == 0))
    def _():
        advance(qi - 1, (1, 0))

    @pl.when(nfar % 2 == 1)
    def _():
        advance(qi - 2, (None, 1, 0))

    lam = lam_ref[...]
    lam_full = (jnp.exp(jnp.sum(lam[0:1] * lam[1:2], axis=1, keepdims=True))
                - jnp.exp(jnp.sum(lam[2:3] * lam[3:4], axis=1, keepdims=True)) + lambda_init)
    for h in heads:
        o = acc_ref[h] * (1.0 / l_ref[h])
        o = o[:, :tb] - lam_full * o[:, tb:]
        o = o * lax.rsqrt(jnp.mean(o * o, axis=0, keepdims=True) + NORM_EPS)
        o_ref[:, head_lanes(h)] = (o.T * (g_ref[...] * out_scale)).astype(o_ref.dtype)


def _attn(q, k, vt, bias, lam, subln_g, batch, seq, tb, nhead, layer_idx, bounded):
    n, d = q.shape
    ngroup = d // (LANES * nhead)
    width = LANES * nhead
    nq = seq // tb
    lambda_init = 0.8 - 0.6 * math.exp(-0.3 * layer_idx)
    q_spec = pl.BlockSpec((tb, width), lambda b, h, i: (b * nq + i, h))
    return pl.pallas_call(
        functools.partial(_attn_kernel, tb=tb, nhead=nhead, out_scale=1.0 - lambda_init,
                          lambda_init=lambda_init, bounded=bounded),
        grid=(batch, ngroup, nq),
        in_specs=[q_spec,
                  pl.BlockSpec((seq, width), lambda b, h, i: (b, h)),
                  pl.BlockSpec((nq, width, tb), lambda b, h, i: (b, h, 0)),
                  pl.BlockSpec((nhead, 2, tb, 2 * tb), lambda b, h, i: (h, 0, 0, 0)),
                  pl.BlockSpec(lam.shape, lambda b, h, i: (0, 0)),
                  pl.BlockSpec(subln_g.shape, lambda b, h, i: (0, 0))],
        out_specs=q_spec,
        out_shape=jax.ShapeDtypeStruct((n, d), BF16),
        scratch_shapes=[pltpu.VMEM((nhead, 1, 2 * tb), F32), pltpu.VMEM((nhead, 1, 2 * tb), F32),
                        pltpu.VMEM((nhead, LANES, 2 * tb), F32)],
        compiler_params=pltpu.CompilerParams(
            dimension_semantics=("arbitrary", "arbitrary", "arbitrary"),
            vmem_limit_bytes=VMEM_LIMIT),
        name="attn",
    )(q, k, vt, bias, lam, subln_g)


def kernel(x, p, norm_g, mlp_w1, mlp_w2, ple_w_up, ple_w_gate, rwkv_mix, rwkv_w_rkvo, rwkv_w0, rwkv_w1, rwkv_w2, rwkv_a0, rwkv_a1, rwkv_a2, rwkv_g1, rwkv_g2, rwkv_k_k, rwkv_k_a, rwkv_r_k, rwkv_ln_w, rwkv_ln_b, kv_norm_g, w_k_shared, w_v_shared, k_norm_g, diff_w_q, diff_q_norm_g, diff_lam, diff_subln_g, diff_w_o, rel_bias):
    batch, seq, d = x.shape
    assert d == D_MODEL and norm_g.shape[0] == 2
    n = batch * seq
    tm = min(256, seq)
    tb_rec = min(256, seq)
    tb_attn = min(256, seq)
    assert seq % tm == 0 and seq % tb_rec == 0 and seq % tb_attn == 0 and tb_attn >= MAX_DIST

    bf = lambda w: w.astype(BF16)
    x2d = x.reshape(n, d)
    p3d = p.reshape(p.shape[0], n, p.shape[-1])
    head_id = jnp.arange(d, dtype=jnp.int32) // RWKV_HEAD
    slot = jnp.arange(LANES, dtype=jnp.int32)
    bc = ((head_id[:, None] == slot[None, :] % N_HEADS) & (slot[None, :] < 2 * N_HEADS)).astype(BF16)
    be = bc.T

    vecs = jnp.stack([norm_g[0, 0], rwkv_w0[0], rwkv_a0[0], rwkv_k_k[0], rwkv_k_a[0],
                      rwkv_r_k[0].reshape(d), rwkv_ln_w[0], rwkv_ln_b[0]])
    y, g, bonus = _rwkv(
        x2d, vecs, rwkv_mix[0], bf(rwkv_w_rkvo[0, 0]), bf(rwkv_w_rkvo[0, 1]), bf(rwkv_w_rkvo[0, 2]),
        bf(rwkv_w1[0]), bf(rwkv_w2[0]), bf(rwkv_a1[0]), bf(rwkv_a2[0]), bf(rwkv_g1[0]), bf(rwkv_g2[0]),
        bc, be, batch, seq, tb_rec)
    x2d = _tail(_tail_rwkv_kernel, (x2d, y, bonus, g), p3d, 0, (vecs, bc, be), norm_g[0, 1:3],
                bf(rwkv_w_rkvo[0, 3]), bf(mlp_w1[0]), bf(mlp_w2[0]), bf(ple_w_up[0]), bf(ple_w_gate[0]), tm)

    reps = d // DIFF_HEAD
    gn = jnp.stack([norm_g[1, 0], kv_norm_g,
                    jnp.tile(diff_q_norm_g[0], reps) * (DIFF_HEAD ** -0.5 * LOG2E), jnp.tile(k_norm_g, reps)])
    q, kq, vq = _qkv(x2d, gn, bf(diff_w_q[0]), bf(w_k_shared), bf(w_v_shared), bc, be, tb_attn)
    bias = _bias_tiles(rel_bias, tb_attn)
    logit_bound = 1.02 * LOG2E * (DIFF_HEAD ** 0.5 * jnp.max(jnp.abs(diff_q_norm_g[0] * k_norm_g))
                                  + jnp.max(jnp.abs(rel_bias - rel_bias[-1:])))
    attn = functools.partial(_attn, q, kq, vq, bias, diff_lam[0], diff_subln_g[0].reshape(1, LANES),
                             batch, seq, tb_attn, 4, 1)
    o = lax.cond(logit_bound <= LOGIT_BOUND, lambda: attn(True), lambda: attn(False))
    x2d = _tail(_tail_attn_kernel, (x2d, o), p3d, 1, (), norm_g[1, 1:3],
                bf(diff_w_o[0]), bf(mlp_w1[1]), bf(mlp_w2[1]), bf(ple_w_up[1]), bf(ple_w_gate[1]), tm)
    return x2d.reshape(batch, seq, d)
```

```python
import functools
import math

import numpy as np
import jax
import jax.numpy as jnp
from jax import lax
from jax.experimental import pallas as pl
from jax.experimental.pallas import tpu as pltpu

F32 = jnp.float32
BF16 = jnp.bfloat16

D_MODEL = 1024
RWKV_HEAD = 64
N_HEADS = D_MODEL // RWKV_HEAD
DIFF_HEAD = 64
DIFF_HEADS = 8
LANES = 128
CHUNK = 64
SUB = 16
PRE_STAGE_EVERY = 5
FF_CHUNK = 1024
NORM_EPS = 1e-6
GN_EPS = 64e-5
N_BUCKETS = 32
MAX_DIST = 128
NEG_BIG = -1e30
LOG2E = math.log2(math.e)
LOGIT_BOUND = 60.0
VMEM_LIMIT = 56 * 1024 * 1024


def _dot(a, b):
    return jnp.dot(a.astype(BF16), b.astype(BF16), preferred_element_type=F32)


def _dot_nt(a, b):
    return lax.dot_general(a.astype(BF16), b.astype(BF16), (((1,), (1,)), ((), ())),
                           preferred_element_type=F32)


def _dot_tn(a, b):
    return lax.dot_general(a.astype(BF16), b.astype(BF16), (((0,), (0,)), ((), ())),
                           preferred_element_type=F32)


def _rms(x, g):
    return x * lax.rsqrt(jnp.mean(x * x, axis=-1, keepdims=True) + NORM_EPS) * g


def _sigmoid(x):
    return 1.0 / (1.0 + jnp.exp(-x))


def _head_reduce(x, bc_ref):
    return _dot(x, bc_ref[...])


def _head_expand(c, be_ref):
    lane = lax.broadcasted_iota(jnp.int32, (1, LANES), 1)
    hi = c.astype(BF16).astype(F32)
    return _dot(jnp.where(lane < N_HEADS, hi, c - hi), be_ref[...])


def _const_spec(shape):
    nd = len(shape)
    return pl.BlockSpec(shape, lambda *_: (0,) * nd, pipeline_mode=pl.Buffered(1))


def _row_spec(tm, width):
    return pl.BlockSpec((tm, width), lambda i: (i, 0))


def _pre_stages(x_ref, xp_ref, vec_ref, mix_ref, wr_ref, wk_ref, wv_ref,
                w1_ref, w2_ref, a1_ref, a2_ref, g1_ref, g2_ref, bc_ref, be_ref,
                r_out, lw_out, k_out, v_out, kk_out, a_out, g_out, bonus_out, *, first_block):
    g0 = vec_ref[0:1, :]
    h = _rms(x_ref[...], g0)
    prev = _rms(xp_ref[...], g0)[7:8, :]
    prev = jnp.where(first_block, 0.0, prev)
    row = lax.broadcasted_iota(jnp.int32, h.shape, 0)
    hprev = jnp.where(row == 0, prev, pltpu.roll(h, 1, 0))
    dx = hprev - h

    def mixed(j):
        return (h + dx * mix_ref[j:j + 1, :]).astype(BF16)

    yield
    r = _dot(mixed(0), wr_ref[...])
    r_out[...] = r
    yield
    k = _dot(mixed(2), wk_ref[...])
    yield
    v = _dot(mixed(3), wv_ref[...])
    v_out[...] = v
    yield
    t_w = jnp.tanh(_dot(mixed(1), w1_ref[...]))
    yield
    t_a = _dot(mixed(4), a1_ref[...])
    t_g = _sigmoid(_dot(mixed(5), g1_ref[...]))
    yield
    lw_out[...] = -math.exp(-0.5) * _sigmoid(vec_ref[1:2, :] + _dot(t_w, w2_ref[...]))
    yield
    a = _sigmoid(vec_ref[2:3, :] + _dot(t_a, a2_ref[...]))
    a_out[...] = a
    yield
    g_out[...] = _dot(t_g, g2_ref[...]).astype(g_out.dtype)
    yield
    kk = k * vec_ref[3:4, :]
    ss = _head_reduce(kk * kk, bc_ref)
    k = k * (1.0 + (a - 1.0) * vec_ref[4:5, :])
    k_out[...] = k
    bsum = _head_reduce(r * k * vec_ref[5:6, :], bc_ref)
    yield
    kk_out[...] = kk * _head_expand(1.0 / jnp.maximum(jnp.sqrt(ss), 1e-12), be_ref)
    bonus_out[...] = (_head_expand(bsum, be_ref) * v).astype(bonus_out.dtype)


def _rec_stages(r_ref, lw_ref, k_ref, v_ref, kk_ref, a_ref, y_ref, s_ref, *, nchunk, npair, first_block):
    c2 = 2 * CHUNK
    width = npair * LANES

    @pl.when(first_block)
    def _():
        s_ref[...] = jnp.zeros_like(s_ref)

    ri = lax.broadcasted_iota(jnp.int32, (c2, c2), 0)
    ci = lax.broadcasted_iota(jnp.int32, (c2, c2), 1)
    sub_diag = (ri // SUB) == (ci // SUB)
    eye = (ri == ci).astype(F32)
    gi = lax.broadcasted_iota(jnp.int32, (2 * c2, 2 * c2), 0)
    gj = lax.broadcasted_iota(jnp.int32, (2 * c2, 2 * c2), 1)
    gram_mask = (gj % CHUNK) < (gi % CHUNK) + jnp.where(gi < c2, 0, 1)
    lo_lane = (lax.broadcasted_iota(jnp.int32, (1, width), 1) % LANES) < CHUNK
    tri = (lax.broadcasted_iota(jnp.int32, (CHUNK, CHUNK), 1)
           <= lax.broadcasted_iota(jnp.int32, (CHUNK, CHUNK), 0)).astype(BF16)
    pairs = range(npair)
    nsub = c2 // SUB
    eye_p = (lax.broadcasted_iota(jnp.int32, (SUB, c2), 0)
             == lax.broadcasted_iota(jnp.int32, (SUB, c2), 1) % SUB).astype(F32)

    def pack(m):
        return m.reshape(nsub, SUB, c2).sum(axis=0)

    def unpack(p):
        return jnp.where(sub_diag, jnp.tile(p, (nsub, 1)), 0.0)

    def per_pair(x):
        return [x[:, p * LANES:(p + 1) * LANES] for p in pairs]

    def stacked(x):
        return [jnp.concatenate([a, b], axis=0)
                for a, b in zip(per_pair(jnp.where(lo_lane, x, 0.0)), per_pair(jnp.where(lo_lane, 0.0, x)))]

    def setup(c, out):
        sl = slice(c * CHUNK, (c + 1) * CHUNK)
        lw = lw_ref[sl, :]
        l_hi = lw.astype(BF16)
        l_mid = (lw - l_hi.astype(F32)).astype(BF16)
        l_lo = (lw - l_hi.astype(F32) - l_mid.astype(F32)).astype(BF16)
        cum = _dot(tri, l_hi) + _dot(tri, l_mid) + _dot(tri, l_lo)
        tot = cum[CHUNK - 1:CHUNK, :]
        e_neg = jnp.exp(-cum)
        e_rest = jnp.exp(tot - cum)
        k = k_ref[sl, :]
        kk = kk_ref[sl, :]
        kb = kk * a_ref[sl, :]
        at = stacked(-kk * jnp.exp(cum - lw))
        rt = stacked(r_ref[sl, :] * jnp.exp(cum))
        bt = stacked(kb * e_neg)
        kt = stacked(k * e_neg)
        out["bkw"] = [jnp.concatenate([b, kq], axis=0).astype(BF16)
                      for b, kq in zip(stacked(kb * e_rest), stacked(k * e_rest))]
        out["vm"] = stacked(v_ref[sl, :])
        out["decay"] = per_pair(jnp.exp(tot))
        lhs = out["lhs"] = [jnp.concatenate([a, r], axis=0).astype(BF16) for a, r in zip(at, rt)]
        yield
        gram = out["gram"] = [jnp.where(gram_mask, _dot_nt(l, jnp.concatenate([b, kq], axis=0)), 0.0)
                              for l, b, kq in zip(lhs, bt, kt)]
        abd = [g[:c2, :c2] for g in gram]
        dg = [jnp.where(sub_diag, x, 0.0) for x in abd]
        off = [x - d for x, d in zip(abd, dg)]
        yield
        tp = [eye_p + pack(d) for d in dg]
        xp = [_dot(pack(d), d) for d in dg]
        for _ in range(2):
            yield
            res = [_dot(jnp.concatenate([ti, xi], axis=0), unpack(xi)) for ti, xi in zip(tp, xp)]
            tp = [ti + r[:SUB] for ti, r in zip(tp, res)]
            xp = [r[SUB:] for r in res]
        yield
        t = [unpack(ti + _dot(ti, unpack(xi))) for ti, xi in zip(tp, xp)]
        yield
        nn = [_dot(ti, o) for ti, o in zip(t, off)]
        yield
        res = [_dot(n, jnp.concatenate([n, ti], axis=1)) for n, ti in zip(nn, t)]
        q = [ti + r[:, c2:] for ti, r in zip(t, res)]
        yield
        out["tinv"] = [qi + _dot(r[:, :c2], qi) for qi, r in zip(q, res)]

    def carry(c, pre):
        sl = slice(c * CHUNK, (c + 1) * CHUNK)
        gram, vm = pre["gram"], pre["vm"]
        hs = [_dot_nt(l, s_ref[p]) for p, l in zip(pairs, pre["lhs"])]
        yield
        w = [h[:c2] + _dot(g[:c2, c2:], v) for h, g, v in zip(hs, gram, vm)]
        yield
        u = [_dot(ti, wi) for ti, wi in zip(pre["tinv"], w)]
        uv = [jnp.concatenate([ui, v], axis=0).astype(BF16) for ui, v in zip(u, vm)]
        yield
        ym = [h[c2:] + _dot(g[c2:], x2) for h, g, x2 in zip(hs, gram, uv)]
        y_ref[sl, :] = jnp.concatenate([yi[:CHUNK] + yi[CHUNK:] for yi in ym], axis=1)
        yield
        for p in pairs:
            s_ref[p] = s_ref[p] * pre["decay"][p] + _dot_tn(uv[p], pre["bkw"][p])

    pre = [dict() for _ in range(nchunk)]
    yield from setup(0, pre[0])
    for c in range(nchunk):
        active = [carry(c, pre[c])] + ([setup(c + 1, pre[c + 1])] if c + 1 < nchunk else [])
        while active:
            for gen in list(active):
                if next(gen, StopIteration) is StopIteration:
                    active.remove(gen)
                else:
                    yield


def _drain(gen):
    for _ in gen:
        pass


def _rwkv_kernel(*refs, nchunk, npair, nt):
    ins, (y_out, g_out, bonus_out), (r_s, lw_s, k_s, v_s, kk_s, a_s, s_ref) = refs[:15], refs[15:18], refs[18:]
    s = pl.program_id(1)
    bufs = (r_s, lw_s, k_s, v_s, kk_s, a_s)
    pre = lambda: _pre_stages(*ins, *[b.at[s % 2] for b in bufs], g_out, bonus_out, first_block=s == 0)
    rec = lambda: _rec_stages(*[b.at[(s + 1) % 2] for b in bufs], y_out, s_ref, nchunk=nchunk, npair=npair,
                              first_block=s == 1)

    @pl.when(s == 0)
    def _():
        _drain(pre())

    @pl.when((s > 0) & (s < nt))
    def _():
        side = pre()
        for i, _ in enumerate(rec()):
            if i % PRE_STAGE_EVERY == 0:
                next(side, None)
        _drain(side)

    @pl.when(s == nt)
    def _():
        _drain(rec())


def _rwkv(x2d, vecs, mix, wr, wk, wv, w1, w2, a1, a2, g1, g2, bc, be, batch, seq, tb):
    n, d = x2d.shape
    nt = seq // tb
    npair = d // LANES
    consts = (vecs, mix, wr, wk, wv, w1, w2, a1, a2, g1, g2, bc, be)
    pre_blk = lambda b, s: b * nt + jnp.minimum(s, nt - 1)
    rec_blk = lambda b, s: b * nt + jnp.maximum(s - 1, 0)
    in_spec = pl.BlockSpec((tb, d), lambda b, s: (pre_blk(b, s), 0))
    prev_spec = pl.BlockSpec((8, d), lambda b, s: (jnp.maximum(pre_blk(b, s) * (tb // 8) - 1, 0), 0))
    return pl.pallas_call(
        functools.partial(_rwkv_kernel, nchunk=tb // CHUNK, npair=npair, nt=nt),
        grid=(batch, nt + 1),
        in_specs=[in_spec, prev_spec] + [_const_spec(c.shape) for c in consts],
        out_specs=[pl.BlockSpec((tb, d), lambda b, s: (rec_blk(b, s), 0)), in_spec, in_spec],
        out_shape=[jax.ShapeDtypeStruct((n, d), F32)] + [jax.ShapeDtypeStruct((n, d), BF16)] * 2,
        scratch_shapes=[pltpu.VMEM((2, tb, d), F32)] * 6 + [pltpu.VMEM((npair, LANES, LANES), F32)],
        compiler_params=pltpu.CompilerParams(dimension_semantics=("arbitrary", "arbitrary"),
                                             vmem_limit_bytes=VMEM_LIMIT),
        name="rwkv",
    )(x2d, x2d, *consts)


def _tail_body(x, pre, p_ref, gn_ref, wo_ref, w1_ref, w2_ref, wup_ref, wgate_ref, out_ref):
    x = x + _dot(pre, wo_ref[...])
    hn = _rms(x, gn_ref[0:1, :]).astype(BF16)
    d_ff = w1_ref.shape[1]
    acc = x
    for c in range(0, d_ff, FF_CHUNK):
        mid = jnp.maximum(_dot(hn, w1_ref[:, c:c + FF_CHUNK]), 0.0)
        acc = acc + _dot(mid * mid, w2_ref[c:c + FF_CHUNK, :])
    gate = _sigmoid(_dot(_rms(acc, gn_ref[1:2, :]), wgate_ref[...]))
    out = acc + _dot(p_ref[...], wup_ref[...]) * gate
    out_ref[...] = out
    return out


def _tail_attn_kernel(x_ref, o_ref, p_ref, *rest):
    _tail_body(x_ref[...], o_ref[...], p_ref, *rest)


def _tail_rwkv_kernel(x_ref, y_ref, bonus_ref, g_ref, p_ref, vec_ref, bc_ref, be_ref, gnq_ref, wq_ref, wk_ref,
                      wv_ref, gn_ref, wo_ref, w1_ref, w2_ref, wup_ref, wgate_ref, out_ref, q_out, k_out, v_out):
    y = y_ref[...]
    inv_n = 1.0 / RWKV_HEAD
    yc = y - _head_expand(_head_reduce(y, bc_ref) * inv_n, be_ref)
    var = _head_reduce(yc * yc, bc_ref) * inv_n
    yn = yc * _head_expand(lax.rsqrt(var + GN_EPS), be_ref) * vec_ref[6:7, :] + vec_ref[7:8, :]
    x = _tail_body(x_ref[...], (yn + bonus_ref[...]) * g_ref[...], p_ref, gn_ref, wo_ref, w1_ref, w2_ref,
                   wup_ref, wgate_ref, out_ref)
    _qkv_body(x, gnq_ref, wq_ref, wk_ref, wv_ref, bc_ref, be_ref, q_out, k_out, v_out)


def _tail(kernel_fn, rows, p3d, layer, extra_consts, gn, wo, w1, w2, wup, wgate, tm, with_qkv=False):
    n, d = rows[0].shape
    consts = tuple(extra_consts) + (gn, wo, w1, w2, wup, wgate)
    out_specs = [_row_spec(tm, d)]
    out_shape = [jax.ShapeDtypeStruct((n, d), F32)]
    if with_qkv:
        out_specs += [_row_spec(tm, d)] * 2 + [pl.BlockSpec((1, d, tm), lambda i: (i, 0, 0))]
        out_shape += [jax.ShapeDtypeStruct((n, d), BF16)] * 2 + [jax.ShapeDtypeStruct((n // tm, d, tm), BF16)]
    outs = pl.pallas_call(
        kernel_fn,
        grid=(n // tm,),
        in_specs=[_row_spec(tm, d)] * len(rows)
                 + [pl.BlockSpec((None, tm, p3d.shape[2]), lambda i: (layer, i, 0))]
                 + [_const_spec(c.shape) for c in consts],
        out_specs=out_specs,
        out_shape=out_shape,
        compiler_params=pltpu.CompilerParams(dimension_semantics=("arbitrary",),
                                             vmem_limit_bytes=VMEM_LIMIT),
        name="tail",
    )(*rows, p3d, *consts)
    return outs if with_qkv else outs[0]


def _qkv_body(x, gn_ref, wq_ref, wk_ref, wv_ref, bc_ref, be_ref, q_out, k_out, v_out):
    inv_n = 1.0 / DIFF_HEAD

    def head_rms(t, g):
        ms = _head_reduce(t * t, bc_ref) * inv_n
        return t * _head_expand(lax.rsqrt(ms + NORM_EPS), be_ref) * g

    hq = _rms(x, gn_ref[0:1, :])
    hk = _rms(x, gn_ref[1:2, :]).astype(BF16)
    q = head_rms(_dot(hq, wq_ref[...]), gn_ref[2:3, :])
    q_out[...] = q.astype(q_out.dtype)
    k_out[...] = head_rms(_dot(hk, wk_ref[...]), gn_ref[3:4, :]).astype(k_out.dtype)
    v_out[0] = _dot(hk, wv_ref[...]).T.astype(v_out.dtype)


def _bucket_tiles(tb):
    i = np.arange(tb, dtype=np.int64)[None, :]
    j = np.arange(tb, dtype=np.int64)[:, None]

    def bucket(rel):
        n = np.maximum(rel, 0)
        max_exact = N_BUCKETS // 2
        nf = np.maximum(n, 1).astype(np.float32)
        large = max_exact + (np.log(nf / np.float32(max_exact)) / np.float32(math.log(MAX_DIST / max_exact))
                             * np.float32(N_BUCKETS - max_exact)).astype(np.int32)
        large = np.minimum(large, N_BUCKETS - 1)
        return np.where(n < max_exact, n, large).astype(np.int32)

    diag = np.where(i - j >= 0, bucket(i - j), -1)
    near = bucket(tb + i - j)
    tiles = np.stack([diag, near]).astype(np.int32)
    return np.concatenate([tiles, tiles], axis=2)


def _bias_kernel(tab_ref, bucket_ref, out_ref):
    h = pl.program_id(0)
    b = bucket_ref[...]
    far = tab_ref[N_BUCKETS - 1, h]
    acc = jnp.where(b < 0, NEG_BIG, 0.0)
    for n in range(N_BUCKETS - 1):
        acc = jnp.where(b == n, (tab_ref[n, h] - far) * LOG2E, acc)
    out_ref[0] = acc


def _bias_tiles(rel_bias, tb):
    buckets = jnp.asarray(_bucket_tiles(tb))
    nh = rel_bias.shape[1]
    return pl.pallas_call(
        _bias_kernel,
        grid=(nh,),
        in_specs=[pl.BlockSpec(memory_space=pltpu.SMEM),
                  pl.BlockSpec((2, tb, 2 * tb), lambda h: (0, 0, 0))],
        out_specs=pl.BlockSpec((1, 2, tb, 2 * tb), lambda h: (h, 0, 0, 0)),
        out_shape=jax.ShapeDtypeStruct((nh, 2, tb, 2 * tb), F32),
        compiler_params=pltpu.CompilerParams(dimension_semantics=("arbitrary",)),
        name="bias",
    )(rel_bias, buckets)


def _attn_kernel(q_ref, k_ref, vt_ref, bias_ref, lam_ref, g_ref, o_ref, m_ref, l_ref, acc_ref,
                 *, tb, nhead, out_scale, lambda_init, bounded):
    qi = pl.program_id(2)
    heads = range(nhead)
    lane = lax.broadcasted_iota(jnp.int32, (1, LANES), 1)

    def head_lanes(h):
        return slice(h * LANES, (h + 1) * LANES)

    def stacked_q(h):
        q = q_ref[:, head_lanes(h)]
        zero = jnp.zeros_like(q)
        return jnp.concatenate([jnp.where(lane < DIFF_HEAD, q, zero),
                                jnp.where(lane < DIFF_HEAD, zero, q)], axis=0)

    qs = [stacked_q(h) for h in heads]
    m_ref[...] = jnp.full_like(m_ref, NEG_BIG)
    l_ref[...] = jnp.zeros_like(l_ref)
    acc_ref[...] = jnp.zeros_like(acc_ref)

    def advance(j, biases):
        nblk = len(biases)
        rows = pl.ds(pl.multiple_of(j * tb, tb), nblk * tb)
        logits = lambda h: _dot_nt(k_ref[rows, head_lanes(h)], qs[h])
        ahead = min(3, nhead)
        st = {h: logits(h) for h in range(ahead)}
        spans = []
        for i, b in enumerate(biases):
            if b is None and spans and spans[-1][2] is None:
                spans[-1] = (spans[-1][0], i + 1, None)
            else:
                spans.append((i, i + 1, b))
        for h in heads:
            if h + ahead < nhead:
                st[h + ahead] = logits(h + ahead)
            s = st.pop(h)
            parts = [s[i * tb:e * tb] if b is None else s[i * tb:e * tb] + bias_ref[h, b] for i, e, b in spans]
            if bounded:
                ps = [jnp.exp2(x) for x in parts]
                l_new = l_ref[h]
            else:
                m_old = m_ref[h]
                m_new = m_old
                for x in parts:
                    m_new = jnp.maximum(m_new, jnp.max(x, axis=0, keepdims=True))
                alpha = jnp.exp2(m_old - m_new)
                ps = [jnp.exp2(x - m_new) for x in parts]
                l_new = alpha * l_ref[h]
                m_ref[h] = m_new
            for p in ps:
                l_new = l_new + jnp.sum(p, axis=0, keepdims=True)
            l_ref[h] = l_new
            pv = None
            for (i, e, _), p in zip(spans, ps):
                p = p.astype(BF16)
                for b in range(i, e):
                    term = _dot(vt_ref[j + b, head_lanes(h), :], p[(b - i) * tb:(b - i + 1) * tb])
                    pv = term if pv is None else pv + term
            acc_ref[h] = acc_ref[h] + pv if bounded else alpha * acc_ref[h] + pv

    def far_pair(t, carry):
        advance(2 * t, (None, None))
        return carry

    nfar = jnp.maximum(qi - 1, 0)
    lax.fori_loop(0, nfar // 2, far_pair, 0)

    @pl.when(qi == 0)
    def _():
        advance(0, (0,))

    @pl.when((qi >= 1) & (nfar % 2 == 0))
    def _():
        advance(qi - 1, (1, 0))

    @pl.when(nfar % 2 == 1)
    def _():
        advance(qi - 2, (None, 1, 0))

    lam = lam_ref[...]
    lam_full = (jnp.exp(jnp.sum(lam[0:1] * lam[1:2], axis=1, keepdims=True))
                - jnp.exp(jnp.sum(lam[2:3] * lam[3:4], axis=1, keepdims=True)) + lambda_init)
    for h in heads:
        o = acc_ref[h] * (1.0 / l_ref[h])
        o = o[:, :tb] - lam_full * o[:, tb:]
        o = o * lax.rsqrt(jnp.mean(o * o, axis=0, keepdims=True) + NORM_EPS)
        o_ref[:, head_lanes(h)] = (o.T * (g_ref[...] * out_scale)).astype(o_ref.dtype)


def _attn(q, k, vt, bias, lam, subln_g, batch, seq, tb, nhead, layer_idx, bounded):
    n, d = q.shape
    ngroup = d // (LANES * nhead)
    width = LANES * nhead
    nq = seq // tb
    lambda_init = 0.8 - 0.6 * math.exp(-0.3 * layer_idx)
    q_spec = pl.BlockSpec((tb, width), lambda b, h, i: (b * nq + i, h))
    return pl.pallas_call(
        functools.partial(_attn_kernel, tb=tb, nhead=nhead, out_scale=1.0 - lambda_init,
                          lambda_init=lambda_init, bounded=bounded),
        grid=(batch, ngroup, nq),
        in_specs=[q_spec,
                  pl.BlockSpec((seq, width), lambda b, h, i: (b, h)),
                  pl.BlockSpec((nq, width, tb), lambda b, h, i: (b, h, 0)),
                  pl.BlockSpec((nhead, 2, tb, 2 * tb), lambda b, h, i: (h, 0, 0, 0)),
                  pl.BlockSpec(lam.shape, lambda b, h, i: (0, 0)),
                  pl.BlockSpec(subln_g.shape, lambda b, h, i: (0, 0))],
        out_specs=q_spec,
        out_shape=jax.ShapeDtypeStruct((n, d), BF16),
        scratch_shapes=[pltpu.VMEM((nhead, 1, 2 * tb), F32), pltpu.VMEM((nhead, 1, 2 * tb), F32),
                        pltpu.VMEM((nhead, LANES, 2 * tb), F32)],
        compiler_params=pltpu.CompilerParams(
            dimension_semantics=("arbitrary", "arbitrary", "arbitrary"),
            vmem_limit_bytes=VMEM_LIMIT),
        name="attn",
    )(q, k, vt, bias, lam, subln_g)


def kernel(x, p, norm_g, mlp_w1, mlp_w2, ple_w_up, ple_w_gate, rwkv_mix, rwkv_w_rkvo, rwkv_w0, rwkv_w1, rwkv_w2, rwkv_a0, rwkv_a1, rwkv_a2, rwkv_g1, rwkv_g2, rwkv_k_k, rwkv_k_a, rwkv_r_k, rwkv_ln_w, rwkv_ln_b, kv_norm_g, w_k_shared, w_v_shared, k_norm_g, diff_w_q, diff_q_norm_g, diff_lam, diff_subln_g, diff_w_o, rel_bias):
    batch, seq, d = x.shape
    assert d == D_MODEL and norm_g.shape[0] == 2
    n = batch * seq
    tm = min(256, seq)
    tb_rec = min(256, seq)
    tb_attn = min(256, seq)
    assert seq % tm == 0 and seq % tb_rec == 0 and seq % tb_attn == 0 and tb_attn >= MAX_DIST

    bf = lambda w: w.astype(BF16)
    x2d = x.reshape(n, d)
    p3d = p.reshape(p.shape[0], n, p.shape[-1])
    head_id = jnp.arange(d, dtype=jnp.int32) // RWKV_HEAD
    slot = jnp.arange(LANES, dtype=jnp.int32)
    bc = ((head_id[:, None] == slot[None, :] % N_HEADS) & (slot[None, :] < 2 * N_HEADS)).astype(BF16)
    be = bc.T

    vecs = jnp.stack([norm_g[0, 0], rwkv_w0[0], rwkv_a0[0], rwkv_k_k[0], rwkv_k_a[0],
                      rwkv_r_k[0].reshape(d), rwkv_ln_w[0], rwkv_ln_b[0]])
    y, g, bonus = _rwkv(
        x2d, vecs, rwkv_mix[0], bf(rwkv_w_rkvo[0, 0]), bf(rwkv_w_rkvo[0, 1]), bf(rwkv_w_rkvo[0, 2]),
        bf(rwkv_w1[0]), bf(rwkv_w2[0]), bf(rwkv_a1[0]), bf(rwkv_a2[0]), bf(rwkv_g1[0]), bf(rwkv_g2[0]),
        bc, be, batch, seq, tb_rec)
    reps = d // DIFF_HEAD
    gnq = jnp.stack([norm_g[1, 0], kv_norm_g,
                     jnp.tile(diff_q_norm_g[0], reps) * (DIFF_HEAD ** -0.5 * LOG2E), jnp.tile(k_norm_g, reps)])
    x2d, q, kq, vq = _tail(_tail_rwkv_kernel, (x2d, y, bonus, g), p3d, 0,
                           (vecs, bc, be, gnq, bf(diff_w_q[0]), bf(w_k_shared), bf(w_v_shared)), norm_g[0, 1:3],
                           bf(rwkv_w_rkvo[0, 3]), bf(mlp_w1[0]), bf(mlp_w2[0]), bf(ple_w_up[0]),
                           bf(ple_w_gate[0]), tb_attn, with_qkv=True)

    bias = _bias_tiles(rel_bias, tb_attn)
    logit_bound = 1.02 * LOG2E * (DIFF_HEAD ** 0.5 * jnp.max(jnp.abs(diff_q_norm_g[0] * k_norm_g))
                                  + jnp.max(jnp.abs(rel_bias - rel_bias[-1:])))
    attn = functools.partial(_attn, q, kq, vq, bias, diff_lam[0], diff_subln_g[0].reshape(1, LANES),
                             batch, seq, tb_attn, 4, 1)
    o = lax.cond(logit_bound <= LOGIT_BOUND, lambda: attn(True), lambda: attn(False))
    x2d = _tail(_tail_attn_kernel, (x2d, o), p3d, 1, (), norm_g[1, 1:3],
                bf(diff_w_o[0]), bf(mlp_w1[1]), bf(mlp_w2[1]), bf(ple_w_up[1]), bf(ple_w_gate[1]), tm)
    return x2d.reshape(batch, seq, d)
```

```python
import functools
import math

import numpy as np
import jax
import jax.numpy as jnp
from jax import lax
from jax.experimental import pallas as pl
from jax.experimental.pallas import tpu as pltpu

F32 = jnp.float32
BF16 = jnp.bfloat16

D_MODEL = 1024
RWKV_HEAD = 64
N_HEADS = D_MODEL // RWKV_HEAD
DIFF_HEAD = 64
DIFF_HEADS = 8
LANES = 128
CHUNK = 64
SUB = 16
PRE_STAGE_EVERY = 5
FF_CHUNK = 1024
NORM_EPS = 1e-6
GN_EPS = 64e-5
N_BUCKETS = 32
MAX_DIST = 128
NEG_BIG = -1e30
LOG2E = math.log2(math.e)
LOGIT_BOUND = 60.0
VMEM_LIMIT = 60 * 1024 * 1024


def _dot(a, b):
    return jnp.dot(a.astype(BF16), b.astype(BF16), preferred_element_type=F32)


def _dot_nt(a, b):
    return lax.dot_general(a.astype(BF16), b.astype(BF16), (((1,), (1,)), ((), ())),
                           preferred_element_type=F32)


def _dot_tn(a, b):
    return lax.dot_general(a.astype(BF16), b.astype(BF16), (((0,), (0,)), ((), ())),
                           preferred_element_type=F32)


def _rms(x, g):
    return x * lax.rsqrt(jnp.mean(x * x, axis=-1, keepdims=True) + NORM_EPS) * g


def _sigmoid(x):
    return 1.0 / (1.0 + jnp.exp(-x))


def _head_reduce(x, bc_ref):
    return _dot(x, bc_ref[...])


def _head_expand(c, be_ref):
    lane = lax.broadcasted_iota(jnp.int32, (1, LANES), 1)
    hi = c.astype(BF16).astype(F32)
    return _dot(jnp.where(lane < N_HEADS, hi, c - hi), be_ref[...])


def _const_spec(shape):
    nd = len(shape)
    return pl.BlockSpec(shape, lambda *_: (0,) * nd, pipeline_mode=pl.Buffered(1))


def _row_spec(tm, width):
    return pl.BlockSpec((tm, width), lambda i: (i, 0))


def _pre_stages(x_ref, xp_ref, vec_ref, mix_ref, wr_ref, wk_ref, wv_ref,
                w1_ref, w2_ref, a1_ref, a2_ref, g1_ref, g2_ref, bc_ref, be_ref,
                r_out, lw_out, k_out, v_out, kk_out, a_out, g_out, bonus_out, *, first_block):
    g0 = vec_ref[0:1, :]
    h = _rms(x_ref[...], g0)
    prev = _rms(xp_ref[...], g0)[7:8, :]
    prev = jnp.where(first_block, 0.0, prev)
    row = lax.broadcasted_iota(jnp.int32, h.shape, 0)
    hprev = jnp.where(row == 0, prev, pltpu.roll(h, 1, 0))
    dx = hprev - h

    def mixed(j):
        return (h + dx * mix_ref[j:j + 1, :]).astype(BF16)

    yield
    r = _dot(mixed(0), wr_ref[...])
    r_out[...] = r
    yield
    k = _dot(mixed(2), wk_ref[...])
    yield
    v = _dot(mixed(3), wv_ref[...])
    v_out[...] = v
    yield
    t_w = jnp.tanh(_dot(mixed(1), w1_ref[...]))
    yield
    t_a = _dot(mixed(4), a1_ref[...])
    t_g = _sigmoid(_dot(mixed(5), g1_ref[...]))
    yield
    lw_out[...] = -math.exp(-0.5) * _sigmoid(vec_ref[1:2, :] + _dot(t_w, w2_ref[...]))
    yield
    a = _sigmoid(vec_ref[2:3, :] + _dot(t_a, a2_ref[...]))
    a_out[...] = a
    yield
    g_out[...] = _dot(t_g, g2_ref[...]).astype(g_out.dtype)
    yield
    kk = k * vec_ref[3:4, :]
    ss = _head_reduce(kk * kk, bc_ref)
    k = k * (1.0 + (a - 1.0) * vec_ref[4:5, :])
    k_out[...] = k
    bsum = _head_reduce(r * k * vec_ref[5:6, :], bc_ref)
    yield
    kk_out[...] = kk * _head_expand(1.0 / jnp.maximum(jnp.sqrt(ss), 1e-12), be_ref)
    bonus_out[...] = (_head_expand(bsum, be_ref) * v).astype(bonus_out.dtype)


def _rec_stages(r_ref, lw_ref, k_ref, v_ref, kk_ref, a_ref, y_ref, s_ref, *, nchunk, npair, first_block):
    c2 = 2 * CHUNK
    width = npair * LANES

    @pl.when(first_block)
    def _():
        s_ref[...] = jnp.zeros_like(s_ref)

    ri = lax.broadcasted_iota(jnp.int32, (c2, c2), 0)
    ci = lax.broadcasted_iota(jnp.int32, (c2, c2), 1)
    sub_diag = (ri // SUB) == (ci // SUB)
    eye = (ri == ci).astype(F32)
    gi = lax.broadcasted_iota(jnp.int32, (2 * c2, 2 * c2), 0)
    gj = lax.broadcasted_iota(jnp.int32, (2 * c2, 2 * c2), 1)
    gram_mask = (gj % CHUNK) < (gi % CHUNK) + jnp.where(gi < c2, 0, 1)
    lo_lane = (lax.broadcasted_iota(jnp.int32, (1, width), 1) % LANES) < CHUNK
    tri = (lax.broadcasted_iota(jnp.int32, (CHUNK, CHUNK), 1)
           <= lax.broadcasted_iota(jnp.int32, (CHUNK, CHUNK), 0)).astype(BF16)
    pairs = range(npair)
    nsub = c2 // SUB
    eye_p = (lax.broadcasted_iota(jnp.int32, (SUB, c2), 0)
             == lax.broadcasted_iota(jnp.int32, (SUB, c2), 1) % SUB).astype(F32)

    def pack(m):
        return m.reshape(nsub, SUB, c2).sum(axis=0)

    def unpack(p):
        return jnp.where(sub_diag, jnp.tile(p, (nsub, 1)), 0.0)

    def per_pair(x):
        return [x[:, p * LANES:(p + 1) * LANES] for p in pairs]

    def stacked(x):
        return [jnp.concatenate([a, b], axis=0)
                for a, b in zip(per_pair(jnp.where(lo_lane, x, 0.0)), per_pair(jnp.where(lo_lane, 0.0, x)))]

    def setup(c, out):
        sl = slice(c * CHUNK, (c + 1) * CHUNK)
        lw = lw_ref[sl, :]
        l_hi = lw.astype(BF16)
        l_mid = (lw - l_hi.astype(F32)).astype(BF16)
        l_lo = (lw - l_hi.astype(F32) - l_mid.astype(F32)).astype(BF16)
        cum = _dot(tri, l_hi) + _dot(tri, l_mid) + _dot(tri, l_lo)
        tot = cum[CHUNK - 1:CHUNK, :]
        e_neg = jnp.exp(-cum)
        e_rest = jnp.exp(tot - cum)
        k = k_ref[sl, :]
        kk = kk_ref[sl, :]
        kb = kk * a_ref[sl, :]
        at = stacked(-kk * jnp.exp(cum - lw))
        rt = stacked(r_ref[sl, :] * jnp.exp(cum))
        bt = stacked(kb * e_neg)
        kt = stacked(k * e_neg)
        out["bkw"] = [jnp.concatenate([b, kq], axis=0).astype(BF16)
                      for b, kq in zip(stacked(kb * e_rest), stacked(k * e_rest))]
        out["vm"] = stacked(v_ref[sl, :])
        out["decay"] = per_pair(jnp.exp(tot))
        lhs = out["lhs"] = [jnp.concatenate([a, r], axis=0).astype(BF16) for a, r in zip(at, rt)]
        yield
        gram = out["gram"] = [jnp.where(gram_mask, _dot_nt(l, jnp.concatenate([b, kq], axis=0)), 0.0)
                              for l, b, kq in zip(lhs, bt, kt)]
        abd = [g[:c2, :c2] for g in gram]
        dg = [jnp.where(sub_diag, x, 0.0) for x in abd]
        off = [x - d for x, d in zip(abd, dg)]
        yield
        tp = [eye_p + pack(d) for d in dg]
        xp = [_dot(pack(d), d) for d in dg]
        for _ in range(2):
            yield
            res = [_dot(jnp.concatenate([ti, xi], axis=0), unpack(xi)) for ti, xi in zip(tp, xp)]
            tp = [ti + r[:SUB] for ti, r in zip(tp, res)]
            xp = [r[SUB:] for r in res]
        yield
        t = [unpack(ti + _dot(ti, unpack(xi))) for ti, xi in zip(tp, xp)]
        yield
        nn = [_dot(ti, o) for ti, o in zip(t, off)]
        yield
        res = [_dot(n, jnp.concatenate([n, ti], axis=1)) for n, ti in zip(nn, t)]
        q = [ti + r[:, c2:] for ti, r in zip(t, res)]
        yield
        out["tinv"] = [qi + _dot(r[:, :c2], qi) for qi, r in zip(q, res)]

    def carry(c, pre):
        sl = slice(c * CHUNK, (c + 1) * CHUNK)
        gram, vm = pre["gram"], pre["vm"]
        hs = [_dot_nt(l, s_ref[p]) for p, l in zip(pairs, pre["lhs"])]
        yield
        w = [h[:c2] + _dot(g[:c2, c2:], v) for h, g, v in zip(hs, gram, vm)]
        yield
        u = [_dot(ti, wi) for ti, wi in zip(pre["tinv"], w)]
        uv = [jnp.concatenate([ui, v], axis=0).astype(BF16) for ui, v in zip(u, vm)]
        yield
        ym = [h[c2:] + _dot(g[c2:], x2) for h, g, x2 in zip(hs, gram, uv)]
        y_ref[sl, :] = jnp.concatenate([yi[:CHUNK] + yi[CHUNK:] for yi in ym], axis=1)
        yield
        for p in pairs:
            s_ref[p] = s_ref[p] * pre["decay"][p] + _dot_tn(uv[p], pre["bkw"][p])

    pre = [dict() for _ in range(nchunk)]
    yield from setup(0, pre[0])
    for c in range(nchunk):
        active = [carry(c, pre[c])] + ([setup(c + 1, pre[c + 1])] if c + 1 < nchunk else [])
        while active:
            for gen in list(active):
                if next(gen, StopIteration) is StopIteration:
                    active.remove(gen)
                else:
                    yield


def _drain(gen):
    for _ in gen:
        pass


def _rwkv_kernel(*refs, nchunk, npair, nt):
    ins, (y_out, g_out, bonus_out), (r_s, lw_s, k_s, v_s, kk_s, a_s, s_ref) = refs[:15], refs[15:18], refs[18:]
    s = pl.program_id(1)
    bufs = (r_s, lw_s, k_s, v_s, kk_s, a_s)
    pre = lambda: _pre_stages(*ins, *[b.at[s % 2] for b in bufs], g_out, bonus_out, first_block=s == 0)
    rec = lambda: _rec_stages(*[b.at[(s + 1) % 2] for b in bufs], y_out, s_ref, nchunk=nchunk, npair=npair,
                              first_block=s == 1)

    @pl.when(s == 0)
    def _():
        _drain(pre())

    @pl.when((s > 0) & (s < nt))
    def _():
        side = pre()
        for i, _ in enumerate(rec()):
            if i % PRE_STAGE_EVERY == 0:
                next(side, None)
        _drain(side)

    @pl.when(s == nt)
    def _():
        _drain(rec())


def _rwkv(x2d, vecs, mix, wr, wk, wv, w1, w2, a1, a2, g1, g2, bc, be, batch, seq, tb):
    n, d = x2d.shape
    nt = seq // tb
    npair = d // LANES
    consts = (vecs, mix, wr, wk, wv, w1, w2, a1, a2, g1, g2, bc, be)
    pre_blk = lambda b, s: b * nt + jnp.minimum(s, nt - 1)
    rec_blk = lambda b, s: b * nt + jnp.maximum(s - 1, 0)
    in_spec = pl.BlockSpec((tb, d), lambda b, s: (pre_blk(b, s), 0))
    prev_spec = pl.BlockSpec((8, d), lambda b, s: (jnp.maximum(pre_blk(b, s) * (tb // 8) - 1, 0), 0))
    return pl.pallas_call(
        functools.partial(_rwkv_kernel, nchunk=tb // CHUNK, npair=npair, nt=nt),
        grid=(batch, nt + 1),
        in_specs=[in_spec, prev_spec] + [_const_spec(c.shape) for c in consts],
        out_specs=[pl.BlockSpec((tb, d), lambda b, s: (rec_blk(b, s), 0)), in_spec, in_spec],
        out_shape=[jax.ShapeDtypeStruct((n, d), F32)] + [jax.ShapeDtypeStruct((n, d), BF16)] * 2,
        scratch_shapes=[pltpu.VMEM((2, tb, d), F32)] * 6 + [pltpu.VMEM((npair, LANES, LANES), F32)],
        compiler_params=pltpu.CompilerParams(dimension_semantics=("arbitrary", "arbitrary"),
                                             vmem_limit_bytes=VMEM_LIMIT),
        name="rwkv",
    )(x2d, x2d, *consts)


def _tail_stages(x, pre, p, gn_ref, wo_ref, w1_ref, w2_ref, wup_ref, wgate_ref, store):
    x = x + _dot(pre, wo_ref[...])
    yield
    hn = _rms(x, gn_ref[0:1, :]).astype(BF16)
    d_ff = w1_ref.shape[1]
    acc = x
    for c in range(0, d_ff, FF_CHUNK):
        mid = jnp.maximum(_dot(hn, w1_ref[:, c:c + FF_CHUNK]), 0.0)
        yield
        acc = acc + _dot(mid * mid, w2_ref[c:c + FF_CHUNK, :])
        yield
    gate = _sigmoid(_dot(_rms(acc, gn_ref[1:2, :]), wgate_ref[...]))
    yield
    store(acc + _dot(p, wup_ref[...]) * gate)


def _round_robin(gens):
    gens = list(gens)
    while gens:
        gens = [g for g in gens if next(g, StopIteration) is not StopIteration]


def _tail_attn_kernel(x_ref, o_ref, p_ref, gn_ref, wo_ref, w1_ref, w2_ref, wup_ref, wgate_ref, out_ref, *, nslab):
    rows_per = x_ref.shape[0] // nslab

    def slab(i):
        rows = slice(i * rows_per, (i + 1) * rows_per)

        def store(v):
            out_ref[rows, :] = v

        return _tail_stages(x_ref[rows, :], o_ref[rows, :], p_ref[rows, :], gn_ref, wo_ref, w1_ref, w2_ref,
                            wup_ref, wgate_ref, store)

    _round_robin(slab(i) for i in range(nslab))


def _tail_rwkv_kernel(x_ref, y_ref, bonus_ref, g_ref, p_ref, vec_ref, bc_ref, be_ref, gnq_ref, wq_ref, wk_ref,
                      wv_ref, gn_ref, wo_ref, w1_ref, w2_ref, wup_ref, wgate_ref, out_ref, q_out, k_out, v_out,
                      *, nslab):
    rows_per = x_ref.shape[0] // nslab
    assert v_out.shape == (nslab, x_ref.shape[1], rows_per)
    inv_n = 1.0 / RWKV_HEAD

    def slab(i):
        rows = slice(i * rows_per, (i + 1) * rows_per)
        y = y_ref[rows, :]
        mean = _head_reduce(y, bc_ref) * inv_n
        yield
        yc = y - _head_expand(mean, be_ref)
        yield
        var = _head_reduce(yc * yc, bc_ref) * inv_n
        yield
        yn = yc * _head_expand(lax.rsqrt(var + GN_EPS), be_ref) * vec_ref[6:7, :] + vec_ref[7:8, :]
        yield
        fresh = []

        def store(v):
            out_ref[rows, :] = v
            fresh.append(v)

        yield from _tail_stages(x_ref[rows, :], (yn + bonus_ref[rows, :]) * g_ref[rows, :], p_ref[rows, :],
                                gn_ref, wo_ref, w1_ref, w2_ref, wup_ref, wgate_ref, store)
        yield
        yield from _qkv_stages(fresh[0], gnq_ref, wq_ref, wk_ref, wv_ref, bc_ref, be_ref, q_out, k_out, v_out,
                               rows, i)

    _round_robin(slab(i) for i in range(nslab))


def _tail(kernel_fn, rows, p3d, layer, extra_consts, gn, wo, w1, w2, wup, wgate, tm, key_block=None):
    n, d = rows[0].shape
    consts = tuple(extra_consts) + (gn, wo, w1, w2, wup, wgate)
    out_specs = [_row_spec(tm, d)]
    out_shape = [jax.ShapeDtypeStruct((n, d), F32)]
    if key_block:
        out_specs += [_row_spec(tm, d)] * 2 + [pl.BlockSpec((tm // key_block, d, key_block), lambda i: (i, 0, 0))]
        out_shape += [jax.ShapeDtypeStruct((n, d), BF16)] * 2
        out_shape += [jax.ShapeDtypeStruct((n // key_block, d, key_block), BF16)]
    outs = pl.pallas_call(
        kernel_fn,
        grid=(n // tm,),
        in_specs=[_row_spec(tm, d)] * len(rows)
                 + [pl.BlockSpec((None, tm, p3d.shape[2]), lambda i: (layer, i, 0))]
                 + [_const_spec(c.shape) for c in consts],
        out_specs=out_specs,
        out_shape=out_shape,
        compiler_params=pltpu.CompilerParams(dimension_semantics=("arbitrary",),
                                             vmem_limit_bytes=VMEM_LIMIT),
        name="tail",
    )(*rows, p3d, *consts)
    return outs if key_block else outs[0]


def _qkv_stages(x, gn_ref, wq_ref, wk_ref, wv_ref, bc_ref, be_ref, q_out, k_out, v_out, rows, slab):
    inv_n = 1.0 / DIFF_HEAD
    hk = _rms(x, gn_ref[1:2, :]).astype(BF16)
    tq = _dot(_rms(x, gn_ref[0:1, :]), wq_ref[...])
    yield
    ms = _head_reduce(tq * tq, bc_ref) * inv_n
    yield
    q = tq * _head_expand(lax.rsqrt(ms + NORM_EPS), be_ref) * gn_ref[2:3, :]
    q_out[rows, :] = q.astype(q_out.dtype)
    yield
    tk = _dot(hk, wk_ref[...])
    yield
    ms = _head_reduce(tk * tk, bc_ref) * inv_n
    yield
    k_out[rows, :] = (tk * _head_expand(lax.rsqrt(ms + NORM_EPS), be_ref) * gn_ref[3:4, :]).astype(k_out.dtype)
    yield
    v_out[slab] = _dot(hk, wv_ref[...]).T.astype(v_out.dtype)


def _bucket_tiles(tb):
    i = np.arange(tb, dtype=np.int64)[None, :]
    j = np.arange(tb, dtype=np.int64)[:, None]

    def bucket(rel):
        n = np.maximum(rel, 0)
        max_exact = N_BUCKETS // 2
        nf = np.maximum(n, 1).astype(np.float32)
        large = max_exact + (np.log(nf / np.float32(max_exact)) / np.float32(math.log(MAX_DIST / max_exact))
                             * np.float32(N_BUCKETS - max_exact)).astype(np.int32)
        large = np.minimum(large, N_BUCKETS - 1)
        return np.where(n < max_exact, n, large).astype(np.int32)

    diag = np.where(i - j >= 0, bucket(i - j), -1)
    near = bucket(tb + i - j)
    tiles = np.stack([diag, near]).astype(np.int32)
    return np.concatenate([tiles, tiles], axis=2)


def _bias_kernel(tab_ref, bucket_ref, out_ref):
    h = pl.program_id(0)
    b = bucket_ref[...]
    far = tab_ref[N_BUCKETS - 1, h]
    acc = jnp.where(b < 0, NEG_BIG, 0.0)
    for n in range(N_BUCKETS - 1):
        acc = jnp.where(b == n, (tab_ref[n, h] - far) * LOG2E, acc)
    out_ref[0] = acc


def _bias_tiles(rel_bias, tb):
    buckets = jnp.asarray(_bucket_tiles(tb))
    nh = rel_bias.shape[1]
    return pl.pallas_call(
        _bias_kernel,
        grid=(nh,),
        in_specs=[pl.BlockSpec(memory_space=pltpu.SMEM),
                  pl.BlockSpec((2, tb, 2 * tb), lambda h: (0, 0, 0))],
        out_specs=pl.BlockSpec((1, 2, tb, 2 * tb), lambda h: (h, 0, 0, 0)),
        out_shape=jax.ShapeDtypeStruct((nh, 2, tb, 2 * tb), F32),
        compiler_params=pltpu.CompilerParams(dimension_semantics=("arbitrary",)),
        name="bias",
    )(rel_bias, buckets)


def _attn_kernel(q_ref, k_ref, vt_ref, bias_ref, lam_ref, g_ref, o_ref, m_ref, l_ref, acc_ref,
                 *, tb, nhead, out_scale, lambda_init, bounded):
    qi = pl.program_id(2)
    heads = range(nhead)
    lane = lax.broadcasted_iota(jnp.int32, (1, LANES), 1)

    def head_lanes(h):
        return slice(h * LANES, (h + 1) * LANES)

    def stacked_q(h):
        q = q_ref[:, head_lanes(h)]
        zero = jnp.zeros_like(q)
        return jnp.concatenate([jnp.where(lane < DIFF_HEAD, q, zero),
                                jnp.where(lane < DIFF_HEAD, zero, q)], axis=0)

    qs = [stacked_q(h) for h in heads]
    m_ref[...] = jnp.full_like(m_ref, NEG_BIG)
    l_ref[...] = jnp.zeros_like(l_ref)
    acc_ref[...] = jnp.zeros_like(acc_ref)

    def advance(j, biases):
        nblk = len(biases)
        rows = pl.ds(pl.multiple_of(j * tb, tb), nblk * tb)
        logits = lambda h: _dot_nt(k_ref[rows, head_lanes(h)], qs[h])
        ahead = min(3, nhead)
        st = {h: logits(h) for h in range(ahead)}
        spans = []
        for i, b in enumerate(biases):
            if b is None and spans and spans[-1][2] is None:
                spans[-1] = (spans[-1][0], i + 1, None)
            else:
                spans.append((i, i + 1, b))
        for h in heads:
            if h + ahead < nhead:
                st[h + ahead] = logits(h + ahead)
            s = st.pop(h)
            parts = [s[i * tb:e * tb] if b is None else s[i * tb:e * tb] + bias_ref[h, b] for i, e, b in spans]
            if bounded:
                ps = [jnp.exp2(x) for x in parts]
                l_new = l_ref[h]
            else:
                m_old = m_ref[h]
                m_new = m_old
                for x in parts:
                    m_new = jnp.maximum(m_new, jnp.max(x, axis=0, keepdims=True))
                alpha = jnp.exp2(m_old - m_new)
                ps = [jnp.exp2(x - m_new) for x in parts]
                l_new = alpha * l_ref[h]
                m_ref[h] = m_new
            for p in ps:
                l_new = l_new + jnp.sum(p, axis=0, keepdims=True)
            l_ref[h] = l_new
            pv = None
            for (i, e, _), p in zip(spans, ps):
                p = p.astype(BF16)
                for b in range(i, e):
                    term = _dot(vt_ref[j + b, head_lanes(h), :], p[(b - i) * tb:(b - i + 1) * tb])
                    pv = term if pv is None else pv + term
            acc_ref[h] = acc_ref[h] + pv if bounded else alpha * acc_ref[h] + pv

    def far_pair(t, carry):
        advance(2 * t, (None, None))
        return carry

    nfar = jnp.maximum(qi - 1, 0)
    lax.fori_loop(0, nfar // 2, far_pair, 0)

    @pl.when(qi == 0)
    def _():
        advance(0, (0,))

    @pl.when((qi >= 1) & (nfar % 2 == 0))
    def _():
        advance(qi - 1, (1, 0))

    @pl.when(nfar % 2 == 1)
    def _():
        advance(qi - 2, (None, 1, 0))

    lam = lam_ref[...]
    lam_full = (jnp.exp(jnp.sum(lam[0:1] * lam[1:2], axis=1, keepdims=True))
                - jnp.exp(jnp.sum(lam[2:3] * lam[3:4], axis=1, keepdims=True)) + lambda_init)
    for h in heads:
        o = acc_ref[h] * (1.0 / l_ref[h])
        o = o[:, :tb] - lam_full * o[:, tb:]
        o = o * lax.rsqrt(jnp.mean(o * o, axis=0, keepdims=True) + NORM_EPS)
        o_ref[:, head_lanes(h)] = (o.T * (g_ref[...] * out_scale)).astype(o_ref.dtype)


def _attn(q, k, vt, bias, lam, subln_g, batch, seq, tb, nhead, layer_idx, bounded):
    n, d = q.shape
    ngroup = d // (LANES * nhead)
    width = LANES * nhead
    nq = seq // tb
    lambda_init = 0.8 - 0.6 * math.exp(-0.3 * layer_idx)
    q_spec = pl.BlockSpec((tb, width), lambda b, h, i: (b * nq + i, h))
    return pl.pallas_call(
        functools.partial(_attn_kernel, tb=tb, nhead=nhead, out_scale=1.0 - lambda_init,
                          lambda_init=lambda_init, bounded=bounded),
        grid=(batch, ngroup, nq),
        in_specs=[q_spec,
                  pl.BlockSpec((seq, width), lambda b, h, i: (b, h)),
                  pl.BlockSpec((nq, width, tb), lambda b, h, i: (b, h, 0)),
                  pl.BlockSpec((nhead, 2, tb, 2 * tb), lambda b, h, i: (h, 0, 0, 0)),
                  pl.BlockSpec(lam.shape, lambda b, h, i: (0, 0)),
                  pl.BlockSpec(subln_g.shape, lambda b, h, i: (0, 0))],
        out_specs=q_spec,
        out_shape=jax.ShapeDtypeStruct((n, d), BF16),
        scratch_shapes=[pltpu.VMEM((nhead, 1, 2 * tb), F32), pltpu.VMEM((nhead, 1, 2 * tb), F32),
                        pltpu.VMEM((nhead, LANES, 2 * tb), F32)],
        compiler_params=pltpu.CompilerParams(
            dimension_semantics=("arbitrary", "arbitrary", "arbitrary"),
            vmem_limit_bytes=VMEM_LIMIT),
        name="attn",
    )(q, k, vt, bias, lam, subln_g)


def kernel(x, p, norm_g, mlp_w1, mlp_w2, ple_w_up, ple_w_gate, rwkv_mix, rwkv_w_rkvo, rwkv_w0, rwkv_w1, rwkv_w2, rwkv_a0, rwkv_a1, rwkv_a2, rwkv_g1, rwkv_g2, rwkv_k_k, rwkv_k_a, rwkv_r_k, rwkv_ln_w, rwkv_ln_b, kv_norm_g, w_k_shared, w_v_shared, k_norm_g, diff_w_q, diff_q_norm_g, diff_lam, diff_subln_g, diff_w_o, rel_bias):
    batch, seq, d = x.shape
    assert d == D_MODEL and norm_g.shape[0] == 2
    n = batch * seq
    tm = min(256, seq)
    tb_rec = min(256, seq)
    tb_attn = min(256, seq)
    assert seq % tm == 0 and seq % tb_rec == 0 and seq % tb_attn == 0 and tb_attn >= MAX_DIST

    bf = lambda w: w.astype(BF16)
    x2d = x.reshape(n, d)
    p3d = p.reshape(p.shape[0], n, p.shape[-1])
    head_id = jnp.arange(d, dtype=jnp.int32) // RWKV_HEAD
    slot = jnp.arange(LANES, dtype=jnp.int32)
    bc = ((head_id[:, None] == slot[None, :] % N_HEADS) & (slot[None, :] < 2 * N_HEADS)).astype(BF16)
    be = bc.T

    vecs = jnp.stack([norm_g[0, 0], rwkv_w0[0], rwkv_a0[0], rwkv_k_k[0], rwkv_k_a[0],
                      rwkv_r_k[0].reshape(d), rwkv_ln_w[0], rwkv_ln_b[0]])
    y, g, bonus = _rwkv(
        x2d, vecs, rwkv_mix[0], bf(rwkv_w_rkvo[0, 0]), bf(rwkv_w_rkvo[0, 1]), bf(rwkv_w_rkvo[0, 2]),
        bf(rwkv_w1[0]), bf(rwkv_w2[0]), bf(rwkv_a1[0]), bf(rwkv_a2[0]), bf(rwkv_g1[0]), bf(rwkv_g2[0]),
        bc, be, batch, seq, tb_rec)
    reps = d // DIFF_HEAD
    gnq = jnp.stack([norm_g[1, 0], kv_norm_g,
                     jnp.tile(diff_q_norm_g[0], reps) * (DIFF_HEAD ** -0.5 * LOG2E), jnp.tile(k_norm_g, reps)])
    x2d, q, kq, vq = _tail(functools.partial(_tail_rwkv_kernel, nslab=2), (x2d, y, bonus, g), p3d, 0,
                           (vecs, bc, be, gnq, bf(diff_w_q[0]), bf(w_k_shared), bf(w_v_shared)), norm_g[0, 1:3],
                           bf(rwkv_w_rkvo[0, 3]), bf(mlp_w1[0]), bf(mlp_w2[0]), bf(ple_w_up[0]),
                           bf(ple_w_gate[0]), 2 * tb_attn, key_block=tb_attn)

    bias = _bias_tiles(rel_bias, tb_attn)
    logit_bound = 1.02 * LOG2E * (DIFF_HEAD ** 0.5 * jnp.max(jnp.abs(diff_q_norm_g[0] * k_norm_g))
                                  + jnp.max(jnp.abs(rel_bias - rel_bias[-1:])))
    attn = functools.partial(_attn, q, kq, vq, bias, diff_lam[0], diff_subln_g[0].reshape(1, LANES),
                             batch, seq, tb_attn, 4, 1)
    o = lax.cond(logit_bound <= LOGIT_BOUND, lambda: attn(True), lambda: attn(False))
    x2d = _tail(functools.partial(_tail_attn_kernel, nslab=2), (x2d, o), p3d, 1, (), norm_g[1, 1:3],
                bf(diff_w_o[0]), bf(mlp_w1[1]), bf(mlp_w2[1]), bf(ple_w_up[1]), bf(ple_w_gate[1]), 2 * tm)
    return x2d.reshape(batch, seq, d)
```

```python
import functools
import math

import numpy as np
import jax
import jax.numpy as jnp
from jax import lax
from jax.experimental import pallas as pl
from jax.experimental.pallas import tpu as pltpu

F32 = jnp.float32
BF16 = jnp.bfloat16

D_MODEL = 1024
RWKV_HEAD = 64
N_HEADS = D_MODEL // RWKV_HEAD
DIFF_HEAD = 64
DIFF_HEADS = 8
LANES = 128
CHUNK = 64
SUB = 16
SETUP_AHEAD = 3
PRE_STAGE_EVERY = 5
FF_CHUNK = 1024
NORM_EPS = 1e-6
GN_EPS = 64e-5
N_BUCKETS = 32
MAX_DIST = 128
NEG_BIG = -1e30
LOG2E = math.log2(math.e)
LOGIT_BOUND = 60.0
VMEM_LIMIT = 60 * 1024 * 1024


def _dot(a, b):
    return jnp.dot(a.astype(BF16), b.astype(BF16), preferred_element_type=F32)


def _dot_nt(a, b):
    return lax.dot_general(a.astype(BF16), b.astype(BF16), (((1,), (1,)), ((), ())),
                           preferred_element_type=F32)


def _dot_tn(a, b):
    return lax.dot_general(a.astype(BF16), b.astype(BF16), (((0,), (0,)), ((), ())),
                           preferred_element_type=F32)


def _rms(x, g):
    return x * lax.rsqrt(jnp.mean(x * x, axis=-1, keepdims=True) + NORM_EPS) * g


def _sigmoid(x):
    return 1.0 / (1.0 + jnp.exp(-x))


def _head_reduce(x, bc_ref):
    return _dot(x, bc_ref[...])


def _head_expand(c, be_ref):
    lane = lax.broadcasted_iota(jnp.int32, (1, LANES), 1)
    hi = c.astype(BF16).astype(F32)
    return _dot(jnp.where(lane < N_HEADS, hi, c - hi), be_ref[...])


def _const_spec(shape):
    nd = len(shape)
    return pl.BlockSpec(shape, lambda *_: (0,) * nd, pipeline_mode=pl.Buffered(1))


def _row_spec(tm, width):
    return pl.BlockSpec((tm, width), lambda i: (i, 0))


def _pre_stages(x_ref, xp_ref, vec_ref, mix_ref, wr_ref, wk_ref, wv_ref,
                w1_ref, w2_ref, a1_ref, a2_ref, g1_ref, g2_ref, bc_ref, be_ref,
                r_out, lw_out, k_out, v_out, kk_out, a_out, g_out, bonus_out, *, first_block):
    g0 = vec_ref[0:1, :]
    h = _rms(x_ref[...], g0)
    prev = _rms(xp_ref[...], g0)[7:8, :]
    prev = jnp.where(first_block, 0.0, prev)
    row = lax.broadcasted_iota(jnp.int32, h.shape, 0)
    hprev = jnp.where(row == 0, prev, pltpu.roll(h, 1, 0))
    dx = hprev - h

    def mixed(j):
        return (h + dx * mix_ref[j:j + 1, :]).astype(BF16)

    yield
    r = _dot(mixed(0), wr_ref[...])
    r_out[...] = r
    yield
    k = _dot(mixed(2), wk_ref[...])
    yield
    v = _dot(mixed(3), wv_ref[...])
    v_out[...] = v
    yield
    t_w = jnp.tanh(_dot(mixed(1), w1_ref[...]))
    yield
    t_a = _dot(mixed(4), a1_ref[...])
    t_g = _sigmoid(_dot(mixed(5), g1_ref[...]))
    yield
    lw_out[...] = -math.exp(-0.5) * _sigmoid(vec_ref[1:2, :] + _dot(t_w, w2_ref[...]))
    yield
    a = _sigmoid(vec_ref[2:3, :] + _dot(t_a, a2_ref[...]))
    a_out[...] = a
    yield
    g_out[...] = _dot(t_g, g2_ref[...]).astype(g_out.dtype)
    yield
    kk = k * vec_ref[3:4, :]
    ss = _head_reduce(kk * kk, bc_ref)
    k = k * (1.0 + (a - 1.0) * vec_ref[4:5, :])
    k_out[...] = k
    bsum = _head_reduce(r * k * vec_ref[5:6, :], bc_ref)
    yield
    kk_out[...] = kk * _head_expand(1.0 / jnp.maximum(jnp.sqrt(ss), 1e-12), be_ref)
    bonus_out[...] = (_head_expand(bsum, be_ref) * v).astype(bonus_out.dtype)


def _rec_stages(r_ref, lw_ref, k_ref, v_ref, kk_ref, a_ref, y_ref, s_ref, *, nchunk, npair, first_block):
    c2 = 2 * CHUNK
    width = npair * LANES

    @pl.when(first_block)
    def _():
        s_ref[...] = jnp.zeros_like(s_ref)

    ri = lax.broadcasted_iota(jnp.int32, (c2, c2), 0)
    ci = lax.broadcasted_iota(jnp.int32, (c2, c2), 1)
    sub_diag = (ri // SUB) == (ci // SUB)
    eye = (ri == ci).astype(F32)
    gi = lax.broadcasted_iota(jnp.int32, (2 * c2, 2 * c2), 0)
    gj = lax.broadcasted_iota(jnp.int32, (2 * c2, 2 * c2), 1)
    gram_mask = (gj % CHUNK) < (gi % CHUNK) + jnp.where(gi < c2, 0, 1)
    lo_lane = (lax.broadcasted_iota(jnp.int32, (1, width), 1) % LANES) < CHUNK
    tri = (lax.broadcasted_iota(jnp.int32, (CHUNK, CHUNK), 1)
           <= lax.broadcasted_iota(jnp.int32, (CHUNK, CHUNK), 0)).astype(BF16)
    pairs = range(npair)
    nsub = c2 // SUB
    eye_p = (lax.broadcasted_iota(jnp.int32, (SUB, c2), 0)
             == lax.broadcasted_iota(jnp.int32, (SUB, c2), 1) % SUB).astype(F32)

    def pack(m):
        return m.reshape(nsub, SUB, c2).sum(axis=0)

    def unpack(p):
        return jnp.where(sub_diag, jnp.tile(p, (nsub, 1)), 0.0)

    def per_pair(x):
        return [x[:, p * LANES:(p + 1) * LANES] for p in pairs]

    def stacked(x):
        return [jnp.concatenate([a, b], axis=0)
                for a, b in zip(per_pair(jnp.where(lo_lane, x, 0.0)), per_pair(jnp.where(lo_lane, 0.0, x)))]

    def setup(c, out):
        sl = slice(c * CHUNK, (c + 1) * CHUNK)
        lw = lw_ref[sl, :]
        l_hi = lw.astype(BF16)
        l_mid = (lw - l_hi.astype(F32)).astype(BF16)
        l_lo = (lw - l_hi.astype(F32) - l_mid.astype(F32)).astype(BF16)
        cum = _dot(tri, l_hi) + _dot(tri, l_mid) + _dot(tri, l_lo)
        tot = cum[CHUNK - 1:CHUNK, :]
        e_neg = jnp.exp(-cum)
        e_rest = jnp.exp(tot - cum)
        k = k_ref[sl, :]
        kk = kk_ref[sl, :]
        kb = kk * a_ref[sl, :]
        at = stacked(-kk * jnp.exp(cum - lw))
        rt = stacked(r_ref[sl, :] * jnp.exp(cum))
        bt = stacked(kb * e_neg)
        kt = stacked(k * e_neg)
        out["bkw"] = [jnp.concatenate([b, kq], axis=0).astype(BF16)
                      for b, kq in zip(stacked(kb * e_rest), stacked(k * e_rest))]
        out["vm"] = [v.astype(BF16) for v in stacked(v_ref[sl, :])]
        out["decay"] = per_pair(jnp.exp(tot))
        lhs = out["lhs"] = [jnp.concatenate([a, r], axis=0).astype(BF16) for a, r in zip(at, rt)]
        yield
        gram = [jnp.where(gram_mask, _dot_nt(l, jnp.concatenate([b, kq], axis=0)), 0.0)
                for l, b, kq in zip(lhs, bt, kt)]
        out["a_k"] = [g[:c2, c2:].astype(BF16) for g in gram]
        out["r_bk"] = [g[c2:].astype(BF16) for g in gram]
        abd = [g[:c2, :c2] for g in gram]
        dg = [jnp.where(sub_diag, x, 0.0) for x in abd]
        off = [x - d for x, d in zip(abd, dg)]
        yield
        tp = [eye_p + pack(d) for d in dg]
        xp = [_dot(pack(d), d) for d in dg]
        for _ in range(2):
            yield
            res = [_dot(jnp.concatenate([ti, xi], axis=0), unpack(xi)) for ti, xi in zip(tp, xp)]
            tp = [ti + r[:SUB] for ti, r in zip(tp, res)]
            xp = [r[SUB:] for r in res]
        yield
        t = [unpack(ti + _dot(ti, unpack(xi))) for ti, xi in zip(tp, xp)]
        yield
        nn = [_dot(ti, o) for ti, o in zip(t, off)]
        yield
        res = [_dot(n, jnp.concatenate([n, ti], axis=1)) for n, ti in zip(nn, t)]
        q = [ti + r[:, c2:] for ti, r in zip(t, res)]
        yield
        out["tinv"] = [(qi + _dot(r[:, :c2], qi)).astype(BF16) for qi, r in zip(q, res)]

    def carry(c, pre):
        sl = slice(c * CHUNK, (c + 1) * CHUNK)
        vm = pre["vm"]
        hs = [_dot_nt(l, s_ref[p]) for p, l in zip(pairs, pre["lhs"])]
        yield
        w = [h[:c2] + _dot(g, v) for h, g, v in zip(hs, pre["a_k"], vm)]
        yield
        u = [_dot(ti, wi) for ti, wi in zip(pre["tinv"], w)]
        uv = [jnp.concatenate([ui.astype(BF16), v], axis=0) for ui, v in zip(u, vm)]
        yield
        ym = [h[c2:] + _dot(g, x2) for h, g, x2 in zip(hs, pre["r_bk"], uv)]
        y_ref[sl, :] = jnp.concatenate([yi[:CHUNK] + yi[CHUNK:] for yi in ym], axis=1)
        yield
        for p in pairs:
            s_ref[p] = s_ref[p] * pre["decay"][p] + _dot_tn(uv[p], pre["bkw"][p])

    pre = [dict() for _ in range(nchunk)]

    def take_turns(active):
        while active:
            for gen in list(active):
                if next(gen, StopIteration) is StopIteration:
                    active.remove(gen)
                else:
                    yield

    yield from take_turns([setup(c, pre[c]) for c in range(min(SETUP_AHEAD, nchunk))])
    for c in range(nchunk):
        ahead = c + SETUP_AHEAD
        yield from take_turns([carry(c, pre[c])] + ([setup(ahead, pre[ahead])] if ahead < nchunk else []))


def _drain(gen):
    for _ in gen:
        pass


def _rwkv_kernel(*refs, nchunk, npair, nt):
    ins, (y_out, g_out, bonus_out), (r_s, lw_s, k_s, v_s, kk_s, a_s, s_ref) = refs[:15], refs[15:18], refs[18:]
    s = pl.program_id(1)
    bufs = (r_s, lw_s, k_s, v_s, kk_s, a_s)
    pre = lambda: _pre_stages(*ins, *[b.at[s % 2] for b in bufs], g_out, bonus_out, first_block=s == 0)
    rec = lambda: _rec_stages(*[b.at[(s + 1) % 2] for b in bufs], y_out, s_ref, nchunk=nchunk, npair=npair,
                              first_block=s == 1)

    @pl.when(s == 0)
    def _():
        _drain(pre())

    @pl.when((s > 0) & (s < nt))
    def _():
        side = pre()
        for i, _ in enumerate(rec()):
            if i % PRE_STAGE_EVERY == 0:
                next(side, None)
        _drain(side)

    @pl.when(s == nt)
    def _():
        _drain(rec())


def _rwkv(x2d, vecs, mix, wr, wk, wv, w1, w2, a1, a2, g1, g2, bc, be, batch, seq, tb):
    n, d = x2d.shape
    nt = seq // tb
    npair = d // LANES
    consts = (vecs, mix, wr, wk, wv, w1, w2, a1, a2, g1, g2, bc, be)
    pre_blk = lambda b, s: b * nt + jnp.minimum(s, nt - 1)
    rec_blk = lambda b, s: b * nt + jnp.maximum(s - 1, 0)
    in_spec = pl.BlockSpec((tb, d), lambda b, s: (pre_blk(b, s), 0))
    prev_spec = pl.BlockSpec((8, d), lambda b, s: (jnp.maximum(pre_blk(b, s) * (tb // 8) - 1, 0), 0))
    return pl.pallas_call(
        functools.partial(_rwkv_kernel, nchunk=tb // CHUNK, npair=npair, nt=nt),
        grid=(batch, nt + 1),
        in_specs=[in_spec, prev_spec] + [_const_spec(c.shape) for c in consts],
        out_specs=[pl.BlockSpec((tb, d), lambda b, s: (rec_blk(b, s), 0)), in_spec, in_spec],
        out_shape=[jax.ShapeDtypeStruct((n, d), F32)] + [jax.ShapeDtypeStruct((n, d), BF16)] * 2,
        scratch_shapes=[pltpu.VMEM((2, tb, d), F32)] * 6 + [pltpu.VMEM((npair, LANES, LANES), F32)],
        compiler_params=pltpu.CompilerParams(dimension_semantics=("arbitrary", "arbitrary"),
                                             vmem_limit_bytes=VMEM_LIMIT),
        name="rwkv",
    )(x2d, x2d, *consts)


def _tail_stages(x, mixer_out, p, gn_ref, w1_ref, w2_ref, wup_ref, wgate_ref, store):
    x = x + mixer_out()
    yield
    hn = _rms(x, gn_ref[0:1, :]).astype(BF16)
    d_ff = w1_ref.shape[1]
    acc = x
    for c in range(0, d_ff, FF_CHUNK):
        mid = jnp.maximum(_dot(hn, w1_ref[:, c:c + FF_CHUNK]), 0.0)
        yield
        acc = acc + _dot(mid * mid, w2_ref[c:c + FF_CHUNK, :])
        yield
    gate = _sigmoid(_dot(_rms(acc, gn_ref[1:2, :]), wgate_ref[...]))
    yield
    store(acc + _dot(p, wup_ref[...]) * gate)


def _round_robin(gens):
    gens = list(gens)
    while gens:
        gens = [g for g in gens if next(g, StopIteration) is not StopIteration]


def _tail_attn_kernel(x_ref, o_ref, p_ref, gn_ref, wo_ref, w1_ref, w2_ref, wup_ref, wgate_ref, out_ref, *, nslab):
    rows_per = x_ref.shape[0] // nslab

    def slab(i):
        rows = slice(i * rows_per, (i + 1) * rows_per)

        def store(v):
            out_ref[rows, :] = v

        return _tail_stages(x_ref[rows, :], lambda: _dot_tn(o_ref[:, rows], wo_ref[...]), p_ref[rows, :],
                            gn_ref, w1_ref, w2_ref, wup_ref, wgate_ref, store)

    _round_robin(slab(i) for i in range(nslab))


def _tail_rwkv_kernel(x_ref, y_ref, bonus_ref, g_ref, p_ref, vec_ref, bc_ref, be_ref, gnq_ref, wq_ref, wk_ref,
                      wv_ref, gn_ref, wo_ref, w1_ref, w2_ref, wup_ref, wgate_ref, out_ref, q_out, k_out, v_out,
                      *, nslab):
    rows_per = x_ref.shape[0] // nslab
    assert v_out.shape == (nslab, x_ref.shape[1], rows_per)
    inv_n = 1.0 / RWKV_HEAD

    def slab(i):
        rows = slice(i * rows_per, (i + 1) * rows_per)
        y = y_ref[rows, :]
        mean = _head_reduce(y, bc_ref) * inv_n
        yield
        yc = y - _head_expand(mean, be_ref)
        yield
        var = _head_reduce(yc * yc, bc_ref) * inv_n
        yield
        yn = yc * _head_expand(lax.rsqrt(var + GN_EPS), be_ref) * vec_ref[6:7, :] + vec_ref[7:8, :]
        yield
        fresh = []

        def store(v):
            out_ref[rows, :] = v
            fresh.append(v)

        gated = (yn + bonus_ref[rows, :]) * g_ref[rows, :]
        yield from _tail_stages(x_ref[rows, :], lambda: _dot(gated, wo_ref[...]), p_ref[rows, :],
                                gn_ref, w1_ref, w2_ref, wup_ref, wgate_ref, store)
        yield
        yield from _qkv_stages(fresh[0], gnq_ref, wq_ref, wk_ref, wv_ref, bc_ref, be_ref, q_out, k_out, v_out,
                               rows, i)

    _round_robin(slab(i) for i in range(nslab))


def _tail(kernel_fn, rows, cols, p3d, layer, extra_consts, gn, wo, w1, w2, wup, wgate, tm, key_block=None):
    n, d = rows[0].shape
    row_specs = [_row_spec(tm, d)] * len(rows) + [pl.BlockSpec((d, tm), lambda i: (0, i))] * len(cols)
    consts = tuple(extra_consts) + (gn, wo, w1, w2, wup, wgate)
    out_specs = [_row_spec(tm, d)]
    out_shape = [jax.ShapeDtypeStruct((n, d), F32)]
    if key_block:
        out_specs += [_row_spec(tm, d)] * 2 + [pl.BlockSpec((tm // key_block, d, key_block), lambda i: (i, 0, 0))]
        out_shape += [jax.ShapeDtypeStruct((n, d), BF16)] * 2
        out_shape += [jax.ShapeDtypeStruct((n // key_block, d, key_block), BF16)]
    outs = pl.pallas_call(
        kernel_fn,
        grid=(n // tm,),
        in_specs=row_specs
                 + [pl.BlockSpec((None, tm, p3d.shape[2]), lambda i: (layer, i, 0))]
                 + [_const_spec(c.shape) for c in consts],
        out_specs=out_specs,
        out_shape=out_shape,
        compiler_params=pltpu.CompilerParams(dimension_semantics=("arbitrary",),
                                             vmem_limit_bytes=VMEM_LIMIT),
        name="tail",
    )(*rows, *cols, p3d, *consts)
    return outs if key_block else outs[0]


def _qkv_stages(x, gn_ref, wq_ref, wk_ref, wv_ref, bc_ref, be_ref, q_out, k_out, v_out, rows, slab):
    inv_n = 1.0 / DIFF_HEAD
    hk = _rms(x, gn_ref[1:2, :]).astype(BF16)
    tq = _dot(_rms(x, gn_ref[0:1, :]), wq_ref[...])
    yield
    ms = _head_reduce(tq * tq, bc_ref) * inv_n
    yield
    q = tq * _head_expand(lax.rsqrt(ms + NORM_EPS), be_ref) * gn_ref[2:3, :]
    q_out[rows, :] = q.astype(q_out.dtype)
    yield
    tk = _dot(hk, wk_ref[...])
    yield
    ms = _head_reduce(tk * tk, bc_ref) * inv_n
    yield
    k_out[rows, :] = (tk * _head_expand(lax.rsqrt(ms + NORM_EPS), be_ref) * gn_ref[3:4, :]).astype(k_out.dtype)
    yield
    v_out[slab] = _dot(hk, wv_ref[...]).T.astype(v_out.dtype)


def _bucket_tiles(tb):
    i = np.arange(tb, dtype=np.int64)[None, :]
    j = np.arange(tb, dtype=np.int64)[:, None]

    def bucket(rel):
        n = np.maximum(rel, 0)
        max_exact = N_BUCKETS // 2
        nf = np.maximum(n, 1).astype(np.float32)
        large = max_exact + (np.log(nf / np.float32(max_exact)) / np.float32(math.log(MAX_DIST / max_exact))
                             * np.float32(N_BUCKETS - max_exact)).astype(np.int32)
        large = np.minimum(large, N_BUCKETS - 1)
        return np.where(n < max_exact, n, large).astype(np.int32)

    diag = np.where(i - j >= 0, bucket(i - j), -1)
    near = bucket(tb + i - j)
    tiles = np.stack([diag, near]).astype(np.int32)
    return np.concatenate([tiles, tiles], axis=2)


def _bias_kernel(tab_ref, bucket_ref, out_ref):
    h = pl.program_id(0)
    b = bucket_ref[...]
    far = tab_ref[N_BUCKETS - 1, h]
    acc = jnp.where(b < 0, NEG_BIG, 0.0)
    for n in range(N_BUCKETS - 1):
        acc = jnp.where(b == n, (tab_ref[n, h] - far) * LOG2E, acc)
    out_ref[0] = acc


def _bias_tiles(rel_bias, tb):
    buckets = jnp.asarray(_bucket_tiles(tb))
    nh = rel_bias.shape[1]
    return pl.pallas_call(
        _bias_kernel,
        grid=(nh,),
        in_specs=[pl.BlockSpec(memory_space=pltpu.SMEM),
                  pl.BlockSpec((2, tb, 2 * tb), lambda h: (0, 0, 0))],
        out_specs=pl.BlockSpec((1, 2, tb, 2 * tb), lambda h: (h, 0, 0, 0)),
        out_shape=jax.ShapeDtypeStruct((nh, 2, tb, 2 * tb), F32),
        compiler_params=pltpu.CompilerParams(dimension_semantics=("arbitrary",)),
        name="bias",
    )(rel_bias, buckets)


def _attn_kernel(q_ref, k_ref, vt_ref, bias_ref, lam_ref, g_ref, o_ref, m_ref, l_ref, acc_ref,
                 *, tb, nhead, out_scale, lambda_init, bounded):
    qi = pl.program_id(2)
    heads = range(nhead)
    lane = lax.broadcasted_iota(jnp.int32, (1, LANES), 1)

    def head_lanes(h):
        return slice(h * LANES, (h + 1) * LANES)

    def stacked_q(h):
        q = q_ref[:, head_lanes(h)]
        zero = jnp.zeros_like(q)
        return jnp.concatenate([jnp.where(lane < DIFF_HEAD, q, zero),
                                jnp.where(lane < DIFF_HEAD, zero, q)], axis=0)

    qs = [stacked_q(h) for h in heads]
    m_ref[...] = jnp.full_like(m_ref, NEG_BIG)
    l_ref[...] = jnp.zeros_like(l_ref)
    acc_ref[...] = jnp.zeros_like(acc_ref)

    def advance(j, biases):
        nblk = len(biases)
        rows = pl.ds(pl.multiple_of(j * tb, tb), nblk * tb)
        logits = lambda h: _dot_nt(k_ref[rows, head_lanes(h)], qs[h])
        ahead = min(3, nhead)
        st = {h: logits(h) for h in range(ahead)}
        spans = []
        for i, b in enumerate(biases):
            if b is None and spans and spans[-1][2] is None:
                spans[-1] = (spans[-1][0], i + 1, None)
            else:
                spans.append((i, i + 1, b))
        for h in heads:
            if h + ahead < nhead:
                st[h + ahead] = logits(h + ahead)
            s = st.pop(h)
            parts = [s[i * tb:e * tb] if b is None else s[i * tb:e * tb] + bias_ref[h, b] for i, e, b in spans]
            if bounded:
                ps = [jnp.exp2(x) for x in parts]
                l_new = l_ref[h]
            else:
                m_old = m_ref[h]
                m_new = m_old
                for x in parts:
                    m_new = jnp.maximum(m_new, jnp.max(x, axis=0, keepdims=True))
                alpha = jnp.exp2(m_old - m_new)
                ps = [jnp.exp2(x - m_new) for x in parts]
                l_new = alpha * l_ref[h]
                m_ref[h] = m_new
            for p in ps:
                l_new = l_new + jnp.sum(p, axis=0, keepdims=True)
            l_ref[h] = l_new
            pv = None
            for (i, e, _), p in zip(spans, ps):
                p = p.astype(BF16)
                for b in range(i, e):
                    term = _dot(vt_ref[j + b, head_lanes(h), :], p[(b - i) * tb:(b - i + 1) * tb])
                    pv = term if pv is None else pv + term
            acc_ref[h] = acc_ref[h] + pv if bounded else alpha * acc_ref[h] + pv

    def far_pair(t, carry):
        advance(2 * t, (None, None))
        return carry

    nfar = jnp.maximum(qi - 1, 0)
    lax.fori_loop(0, nfar // 2, far_pair, 0)

    @pl.when(qi == 0)
    def _():
        advance(0, (0,))

    @pl.when((qi >= 1) & (nfar % 2 == 0))
    def _():
        advance(qi - 1, (1, 0))

    @pl.when(nfar % 2 == 1)
    def _():
        advance(qi - 2, (None, 1, 0))

    lam = lam_ref[...]
    lam_full = (jnp.exp(jnp.sum(lam[0:1] * lam[1:2], axis=1, keepdims=True))
                - jnp.exp(jnp.sum(lam[2:3] * lam[3:4], axis=1, keepdims=True)) + lambda_init)
    for h in heads:
        o = acc_ref[h] * (1.0 / l_ref[h])
        o = o[:, :tb] - lam_full * o[:, tb:]
        o = o * lax.rsqrt(jnp.mean(o * o, axis=0, keepdims=True) + NORM_EPS)
        o_ref[head_lanes(h), :] = (o * (g_ref[...] * out_scale)).astype(o_ref.dtype)


def _attn(q, k, vt, bias, lam, subln_g, batch, seq, tb, nhead, layer_idx, bounded):
    n, d = q.shape
    ngroup = d // (LANES * nhead)
    width = LANES * nhead
    nq = seq // tb
    lambda_init = 0.8 - 0.6 * math.exp(-0.3 * layer_idx)
    q_spec = pl.BlockSpec((tb, width), lambda b, h, i: (b * nq + i, h))
    return pl.pallas_call(
        functools.partial(_attn_kernel, tb=tb, nhead=nhead, out_scale=1.0 - lambda_init,
                          lambda_init=lambda_init, bounded=bounded),
        grid=(batch, ngroup, nq),
        in_specs=[q_spec,
                  pl.BlockSpec((seq, width), lambda b, h, i: (b, h)),
                  pl.BlockSpec((nq, width, tb), lambda b, h, i: (b, h, 0)),
                  pl.BlockSpec((nhead, 2, tb, 2 * tb), lambda b, h, i: (h, 0, 0, 0)),
                  pl.BlockSpec(lam.shape, lambda b, h, i: (0, 0)),
                  pl.BlockSpec(subln_g.shape, lambda b, h, i: (0, 0))],
        out_specs=pl.BlockSpec((width, tb), lambda b, h, i: (h, b * nq + i)),
        out_shape=jax.ShapeDtypeStruct((d, n), BF16),
        scratch_shapes=[pltpu.VMEM((nhead, 1, 2 * tb), F32), pltpu.VMEM((nhead, 1, 2 * tb), F32),
                        pltpu.VMEM((nhead, LANES, 2 * tb), F32)],
        compiler_params=pltpu.CompilerParams(
            dimension_semantics=("arbitrary", "arbitrary", "arbitrary"),
            vmem_limit_bytes=VMEM_LIMIT),
        name="attn",
    )(q, k, vt, bias, lam, subln_g)


def kernel(x, p, norm_g, mlp_w1, mlp_w2, ple_w_up, ple_w_gate, rwkv_mix, rwkv_w_rkvo, rwkv_w0, rwkv_w1, rwkv_w2, rwkv_a0, rwkv_a1, rwkv_a2, rwkv_g1, rwkv_g2, rwkv_k_k, rwkv_k_a, rwkv_r_k, rwkv_ln_w, rwkv_ln_b, kv_norm_g, w_k_shared, w_v_shared, k_norm_g, diff_w_q, diff_q_norm_g, diff_lam, diff_subln_g, diff_w_o, rel_bias):
    batch, seq, d = x.shape
    assert d == D_MODEL and norm_g.shape[0] == 2
    n = batch * seq
    tm = min(256, seq)
    tb_rec = min(256, seq)
    tb_attn = min(256, seq)
    assert seq % tm == 0 and seq % tb_rec == 0 and seq % tb_attn == 0 and tb_attn >= MAX_DIST

    bf = lambda w: w.astype(BF16)
    x2d = x.reshape(n, d)
    p3d = p.reshape(p.shape[0], n, p.shape[-1])
    head_id = jnp.arange(d, dtype=jnp.int32) // RWKV_HEAD
    slot = jnp.arange(LANES, dtype=jnp.int32)
    bc = ((head_id[:, None] == slot[None, :] % N_HEADS) & (slot[None, :] < 2 * N_HEADS)).astype(BF16)
    be = bc.T

    vecs = jnp.stack([norm_g[0, 0], rwkv_w0[0], rwkv_a0[0], rwkv_k_k[0], rwkv_k_a[0],
                      rwkv_r_k[0].reshape(d), rwkv_ln_w[0], rwkv_ln_b[0]])
    y, g, bonus = _rwkv(
        x2d, vecs, rwkv_mix[0], bf(rwkv_w_rkvo[0, 0]), bf(rwkv_w_rkvo[0, 1]), bf(rwkv_w_rkvo[0, 2]),
        bf(rwkv_w1[0]), bf(rwkv_w2[0]), bf(rwkv_a1[0]), bf(rwkv_a2[0]), bf(rwkv_g1[0]), bf(rwkv_g2[0]),
        bc, be, batch, seq, tb_rec)
    reps = d // DIFF_HEAD
    gnq = jnp.stack([norm_g[1, 0], kv_norm_g,
                     jnp.tile(diff_q_norm_g[0], reps) * (DIFF_HEAD ** -0.5 * LOG2E), jnp.tile(k_norm_g, reps)])
    x2d, q, kq, vq = _tail(functools.partial(_tail_rwkv_kernel, nslab=2), (x2d, y, bonus, g), (), p3d, 0,
                           (vecs, bc, be, gnq, bf(diff_w_q[0]), bf(w_k_shared), bf(w_v_shared)), norm_g[0, 1:3],
                           bf(rwkv_w_rkvo[0, 3]), bf(mlp_w1[0]), bf(mlp_w2[0]), bf(ple_w_up[0]),
                           bf(ple_w_gate[0]), 2 * tb_attn, key_block=tb_attn)

    bias = _bias_tiles(rel_bias, tb_attn)
    logit_bound = 1.02 * LOG2E * (DIFF_HEAD ** 0.5 * jnp.max(jnp.abs(diff_q_norm_g[0] * k_norm_g))
                                  + jnp.max(jnp.abs(rel_bias - rel_bias[-1:])))
    attn = functools.partial(_attn, q, kq, vq, bias, diff_lam[0], diff_subln_g[0].reshape(LANES, 1),
                             batch, seq, tb_attn, 4, 1)
    o = lax.cond(logit_bound <= LOGIT_BOUND, lambda: attn(True), lambda: attn(False))
    x2d = _tail(functools.partial(_tail_attn_kernel, nslab=2), (x2d,), (o,), p3d, 1, (), norm_g[1, 1:3],
                bf(diff_w_o[0]), bf(mlp_w1[1]), bf(mlp_w2[1]), bf(ple_w_up[1]), bf(ple_w_gate[1]), 2 * tm)
    return x2d.reshape(batch, seq, d)
```

```python
import functools
import math

import numpy as np
import jax
import jax.numpy as jnp
from jax import lax
from jax.experimental import pallas as pl
from jax.experimental.pallas import tpu as pltpu

F32 = jnp.float32
BF16 = jnp.bfloat16

D_MODEL = 1024
RWKV_HEAD = 64
N_HEADS = D_MODEL // RWKV_HEAD
DIFF_HEAD = 64
DIFF_HEADS = 8
LANES = 128
CHUNK = 64
SUB = 16
SETUP_AHEAD = 3
PRE_STAGE_EVERY = 5
FF_CHUNK = 1024
NORM_EPS = 1e-6
GN_EPS = 64e-5
N_BUCKETS = 32
MAX_DIST = 128
NEG_BIG = -1e30
LOG2E = math.log2(math.e)
FAR_BLOCKS = 4
LOGIT_BOUND = 60.0
VMEM_LIMIT = 60 * 1024 * 1024


def _dot(a, b):
    return jnp.dot(a.astype(BF16), b.astype(BF16), preferred_element_type=F32)


def _dot_nt(a, b):
    return lax.dot_general(a.astype(BF16), b.astype(BF16), (((1,), (1,)), ((), ())),
                           preferred_element_type=F32)


def _dot_tn(a, b):
    return lax.dot_general(a.astype(BF16), b.astype(BF16), (((0,), (0,)), ((), ())),
                           preferred_element_type=F32)


def _rms(x, g):
    return x * lax.rsqrt(jnp.mean(x * x, axis=-1, keepdims=True) + NORM_EPS) * g


def _sigmoid(x):
    return 1.0 / (1.0 + jnp.exp(-x))


def _head_reduce(x, bc_ref):
    return _dot(x, bc_ref[...])


def _head_expand(c, be_ref):
    lane = lax.broadcasted_iota(jnp.int32, (1, LANES), 1)
    hi = c.astype(BF16).astype(F32)
    return _dot(jnp.where(lane < N_HEADS, hi, c - hi), be_ref[...])


def _const_spec(shape):
    nd = len(shape)
    return pl.BlockSpec(shape, lambda *_: (0,) * nd, pipeline_mode=pl.Buffered(1))


def _row_spec(tm, width):
    return pl.BlockSpec((tm, width), lambda i: (i, 0))


def _pre_stages(x_ref, xp_ref, vec_ref, mix_ref, wr_ref, wk_ref, wv_ref,
                w1_ref, w2_ref, a1_ref, a2_ref, g1_ref, g2_ref, bc_ref, be_ref,
                r_out, lw_out, k_out, v_out, kk_out, a_out, g_out, bonus_out, *, first_block):
    g0 = vec_ref[0:1, :]
    h = _rms(x_ref[...], g0)
    prev = _rms(xp_ref[...], g0)[7:8, :]
    prev = jnp.where(first_block, 0.0, prev)
    row = lax.broadcasted_iota(jnp.int32, h.shape, 0)
    hprev = jnp.where(row == 0, prev, pltpu.roll(h, 1, 0))
    dx = hprev - h

    def mixed(j):
        return (h + dx * mix_ref[j:j + 1, :]).astype(BF16)

    yield
    r = _dot(mixed(0), wr_ref[...])
    r_out[...] = r
    yield
    k = _dot(mixed(2), wk_ref[...])
    yield
    v = _dot(mixed(3), wv_ref[...])
    v_out[...] = v
    yield
    t_w = jnp.tanh(_dot(mixed(1), w1_ref[...]))
    yield
    t_a = _dot(mixed(4), a1_ref[...])
    t_g = _sigmoid(_dot(mixed(5), g1_ref[...]))
    yield
    lw_out[...] = -math.exp(-0.5) * _sigmoid(vec_ref[1:2, :] + _dot(t_w, w2_ref[...]))
    yield
    a = _sigmoid(vec_ref[2:3, :] + _dot(t_a, a2_ref[...]))
    a_out[...] = a
    yield
    g_out[...] = _dot(t_g, g2_ref[...]).astype(g_out.dtype)
    yield
    kk = k * vec_ref[3:4, :]
    ss = _head_reduce(kk * kk, bc_ref)
    k = k * (1.0 + (a - 1.0) * vec_ref[4:5, :])
    k_out[...] = k
    bsum = _head_reduce(r * k * vec_ref[5:6, :], bc_ref)
    yield
    kk_out[...] = kk * _head_expand(1.0 / jnp.maximum(jnp.sqrt(ss), 1e-12), be_ref)
    bonus_out[...] = (_head_expand(bsum, be_ref) * v).astype(bonus_out.dtype)


def _rec_stages(r_ref, lw_ref, k_ref, v_ref, kk_ref, a_ref, y_ref, s_ref, *, nchunk, npair, first_block):
    c2 = 2 * CHUNK
    width = npair * LANES

    @pl.when(first_block)
    def _():
        s_ref[...] = jnp.zeros_like(s_ref)

    ri = lax.broadcasted_iota(jnp.int32, (c2, c2), 0)
    ci = lax.broadcasted_iota(jnp.int32, (c2, c2), 1)
    sub_diag = (ri // SUB) == (ci // SUB)
    eye = (ri == ci).astype(F32)
    gi = lax.broadcasted_iota(jnp.int32, (2 * c2, 2 * c2), 0)
    gj = lax.broadcasted_iota(jnp.int32, (2 * c2, 2 * c2), 1)
    gram_mask = (gj % CHUNK) < (gi % CHUNK) + jnp.where(gi < c2, 0, 1)
    lo_lane = (lax.broadcasted_iota(jnp.int32, (1, width), 1) % LANES) < CHUNK
    tri = (lax.broadcasted_iota(jnp.int32, (CHUNK, CHUNK), 1)
           <= lax.broadcasted_iota(jnp.int32, (CHUNK, CHUNK), 0)).astype(BF16)
    pairs = range(npair)
    nsub = c2 // SUB
    eye_p = (lax.broadcasted_iota(jnp.int32, (SUB, c2), 0)
             == lax.broadcasted_iota(jnp.int32, (SUB, c2), 1) % SUB).astype(F32)

    def pack(m):
        return m.reshape(nsub, SUB, c2).sum(axis=0)

    def unpack(p):
        return jnp.where(sub_diag, jnp.tile(p, (nsub, 1)), 0.0)

    def per_pair(x):
        return [x[:, p * LANES:(p + 1) * LANES] for p in pairs]

    def stacked(x):
        return [jnp.concatenate([a, b], axis=0)
                for a, b in zip(per_pair(jnp.where(lo_lane, x, 0.0)), per_pair(jnp.where(lo_lane, 0.0, x)))]

    def setup(c, out):
        sl = slice(c * CHUNK, (c + 1) * CHUNK)
        lw = lw_ref[sl, :]
        l_hi = lw.astype(BF16)
        l_mid = (lw - l_hi.astype(F32)).astype(BF16)
        l_lo = (lw - l_hi.astype(F32) - l_mid.astype(F32)).astype(BF16)
        cum = _dot(tri, l_hi) + _dot(tri, l_mid) + _dot(tri, l_lo)
        tot = cum[CHUNK - 1:CHUNK, :]
        e_neg = jnp.exp(-cum)
        e_rest = jnp.exp(tot - cum)
        k = k_ref[sl, :]
        kk = kk_ref[sl, :]
        kb = kk * a_ref[sl, :]
        at = stacked(-kk * jnp.exp(cum - lw))
        rt = stacked(r_ref[sl, :] * jnp.exp(cum))
        bt = stacked(kb * e_neg)
        kt = stacked(k * e_neg)
        out["bkw"] = [jnp.concatenate([b, kq], axis=0).astype(BF16)
                      for b, kq in zip(stacked(kb * e_rest), stacked(k * e_rest))]
        out["vm"] = [v.astype(BF16) for v in stacked(v_ref[sl, :])]
        out["decay"] = per_pair(jnp.exp(tot))
        lhs = out["lhs"] = [jnp.concatenate([a, r], axis=0).astype(BF16) for a, r in zip(at, rt)]
        yield
        gram = [jnp.where(gram_mask, _dot_nt(l, jnp.concatenate([b, kq], axis=0)), 0.0)
                for l, b, kq in zip(lhs, bt, kt)]
        out["a_k"] = [g[:c2, c2:].astype(BF16) for g in gram]
        out["r_bk"] = [g[c2:].astype(BF16) for g in gram]
        abd = [g[:c2, :c2] for g in gram]
        dg = [jnp.where(sub_diag, x, 0.0) for x in abd]
        off = [x - d for x, d in zip(abd, dg)]
        yield
        tp = [eye_p + pack(d) for d in dg]
        xp = [_dot(pack(d), d) for d in dg]
        for _ in range(2):
            yield
            res = [_dot(jnp.concatenate([ti, xi], axis=0), unpack(xi)) for ti, xi in zip(tp, xp)]
            tp = [ti + r[:SUB] for ti, r in zip(tp, res)]
            xp = [r[SUB:] for r in res]
        yield
        t = [unpack(ti + _dot(ti, unpack(xi))) for ti, xi in zip(tp, xp)]
        yield
        nn = [_dot(ti, o) for ti, o in zip(t, off)]
        yield
        res = [_dot(n, jnp.concatenate([n, ti], axis=1)) for n, ti in zip(nn, t)]
        q = [ti + r[:, c2:] for ti, r in zip(t, res)]
        yield
        out["tinv"] = [(qi + _dot(r[:, :c2], qi)).astype(BF16) for qi, r in zip(q, res)]

    def carry(c, pre):
        sl = slice(c * CHUNK, (c + 1) * CHUNK)
        vm = pre["vm"]
        hs = [_dot_nt(l, s_ref[p]) for p, l in zip(pairs, pre["lhs"])]
        yield
        w = [h[:c2] + _dot(g, v) for h, g, v in zip(hs, pre["a_k"], vm)]
        yield
        u = [_dot(ti, wi) for ti, wi in zip(pre["tinv"], w)]
        uv = [jnp.concatenate([ui.astype(BF16), v], axis=0) for ui, v in zip(u, vm)]
        yield
        ym = [h[c2:] + _dot(g, x2) for h, g, x2 in zip(hs, pre["r_bk"], uv)]
        y_ref[sl, :] = jnp.concatenate([yi[:CHUNK] + yi[CHUNK:] for yi in ym], axis=1)
        yield
        for p in pairs:
            s_ref[p] = s_ref[p] * pre["decay"][p] + _dot_tn(uv[p], pre["bkw"][p])

    pre = [dict() for _ in range(nchunk)]

    def take_turns(active):
        while active:
            for gen in list(active):
                if next(gen, StopIteration) is StopIteration:
                    active.remove(gen)
                else:
                    yield

    yield from take_turns([setup(c, pre[c]) for c in range(min(SETUP_AHEAD, nchunk))])
    for c in range(nchunk):
        ahead = c + SETUP_AHEAD
        yield from take_turns([carry(c, pre[c])] + ([setup(ahead, pre[ahead])] if ahead < nchunk else []))


def _drain(gen):
    for _ in gen:
        pass


def _rwkv_kernel(*refs, nchunk, npair, nt):
    ins, (y_out, g_out, bonus_out), (r_s, lw_s, k_s, v_s, kk_s, a_s, s_ref) = refs[:15], refs[15:18], refs[18:]
    s = pl.program_id(1)
    bufs = (r_s, lw_s, k_s, v_s, kk_s, a_s)
    pre = lambda: _pre_stages(*ins, *[b.at[s % 2] for b in bufs], g_out, bonus_out, first_block=s == 0)
    rec = lambda: _rec_stages(*[b.at[(s + 1) % 2] for b in bufs], y_out, s_ref, nchunk=nchunk, npair=npair,
                              first_block=s == 1)

    @pl.when(s == 0)
    def _():
        _drain(pre())

    @pl.when((s > 0) & (s < nt))
    def _():
        side = pre()
        for i, _ in enumerate(rec()):
            if i % PRE_STAGE_EVERY == 0:
                next(side, None)
        _drain(side)

    @pl.when(s == nt)
    def _():
        _drain(rec())


def _rwkv(x2d, vecs, mix, wr, wk, wv, w1, w2, a1, a2, g1, g2, bc, be, batch, seq, tb):
    n, d = x2d.shape
    nt = seq // tb
    npair = d // LANES
    consts = (vecs, mix, wr, wk, wv, w1, w2, a1, a2, g1, g2, bc, be)
    pre_blk = lambda b, s: b * nt + jnp.minimum(s, nt - 1)
    rec_blk = lambda b, s: b * nt + jnp.maximum(s - 1, 0)
    in_spec = pl.BlockSpec((tb, d), lambda b, s: (pre_blk(b, s), 0))
    prev_spec = pl.BlockSpec((8, d), lambda b, s: (jnp.maximum(pre_blk(b, s) * (tb // 8) - 1, 0), 0))
    return pl.pallas_call(
        functools.partial(_rwkv_kernel, nchunk=tb // CHUNK, npair=npair, nt=nt),
        grid=(batch, nt + 1),
        in_specs=[in_spec, prev_spec] + [_const_spec(c.shape) for c in consts],
        out_specs=[pl.BlockSpec((tb, d), lambda b, s: (rec_blk(b, s), 0)), in_spec, in_spec],
        out_shape=[jax.ShapeDtypeStruct((n, d), F32)] + [jax.ShapeDtypeStruct((n, d), BF16)] * 2,
        scratch_shapes=[pltpu.VMEM((2, tb, d), F32)] * 6 + [pltpu.VMEM((npair, LANES, LANES), F32)],
        compiler_params=pltpu.CompilerParams(dimension_semantics=("arbitrary", "arbitrary"),
                                             vmem_limit_bytes=VMEM_LIMIT),
        name="rwkv",
    )(x2d, x2d, *consts)


def _tail_stages(x, mixer_out, p, gn_ref, w1_ref, w2_ref, wup_ref, wgate_ref, store):
    x = x + mixer_out()
    yield
    hn = _rms(x, gn_ref[0:1, :]).astype(BF16)
    d_ff = w1_ref.shape[1]
    acc = x
    for c in range(0, d_ff, FF_CHUNK):
        mid = jnp.maximum(_dot(hn, w1_ref[:, c:c + FF_CHUNK]), 0.0)
        yield
        acc = acc + _dot(mid * mid, w2_ref[c:c + FF_CHUNK, :])
        yield
    gate = _sigmoid(_dot(_rms(acc, gn_ref[1:2, :]), wgate_ref[...]))
    yield
    store(acc + _dot(p, wup_ref[...]) * gate)


def _round_robin(gens):
    gens = list(gens)
    while gens:
        gens = [g for g in gens if next(g, StopIteration) is not StopIteration]


def _tail_attn_kernel(x_ref, o_ref, p_ref, gn_ref, wo_ref, w1_ref, w2_ref, wup_ref, wgate_ref, out_ref, *, nslab):
    rows_per = x_ref.shape[0] // nslab

    def slab(i):
        rows = slice(i * rows_per, (i + 1) * rows_per)

        def store(v):
            out_ref[rows, :] = v

        return _tail_stages(x_ref[rows, :], lambda: _dot_tn(o_ref[:, rows], wo_ref[...]), p_ref[rows, :],
                            gn_ref, w1_ref, w2_ref, wup_ref, wgate_ref, store)

    _round_robin(slab(i) for i in range(nslab))


def _tail_rwkv_kernel(x_ref, y_ref, bonus_ref, g_ref, p_ref, vec_ref, bc_ref, be_ref, gnq_ref, wq_ref, wk_ref,
                      wv_ref, gn_ref, wo_ref, w1_ref, w2_ref, wup_ref, wgate_ref, out_ref, q_out, k_out, v_out,
                      *, nslab):
    rows_per = x_ref.shape[0] // nslab
    assert v_out.shape == (nslab, x_ref.shape[1], rows_per)
    inv_n = 1.0 / RWKV_HEAD

    def slab(i):
        rows = slice(i * rows_per, (i + 1) * rows_per)
        y = y_ref[rows, :]
        mean = _head_reduce(y, bc_ref) * inv_n
        yield
        yc = y - _head_expand(mean, be_ref)
        yield
        var = _head_reduce(yc * yc, bc_ref) * inv_n
        yield
        yn = yc * _head_expand(lax.rsqrt(var + GN_EPS), be_ref) * vec_ref[6:7, :] + vec_ref[7:8, :]
        yield
        fresh = []

        def store(v):
            out_ref[rows, :] = v
            fresh.append(v)

        gated = (yn + bonus_ref[rows, :]) * g_ref[rows, :]
        yield from _tail_stages(x_ref[rows, :], lambda: _dot(gated, wo_ref[...]), p_ref[rows, :],
                                gn_ref, w1_ref, w2_ref, wup_ref, wgate_ref, store)
        yield
        yield from _qkv_stages(fresh[0], gnq_ref, wq_ref, wk_ref, wv_ref, bc_ref, be_ref, q_out, k_out, v_out,
                               rows, i)

    _round_robin(slab(i) for i in range(nslab))


def _tail(kernel_fn, rows, cols, p3d, layer, extra_consts, gn, wo, w1, w2, wup, wgate, tm, key_block=None):
    n, d = rows[0].shape
    row_specs = [_row_spec(tm, d)] * len(rows) + [pl.BlockSpec((d, tm), lambda i: (0, i))] * len(cols)
    consts = tuple(extra_consts) + (gn, wo, w1, w2, wup, wgate)
    out_specs = [_row_spec(tm, d)]
    out_shape = [jax.ShapeDtypeStruct((n, d), F32)]
    if key_block:
        out_specs += [_row_spec(tm, d)] * 2 + [pl.BlockSpec((tm // key_block, d, key_block), lambda i: (i, 0, 0))]
        out_shape += [jax.ShapeDtypeStruct((n, d), BF16)] * 2
        out_shape += [jax.ShapeDtypeStruct((n // key_block, d, key_block), BF16)]
    outs = pl.pallas_call(
        kernel_fn,
        grid=(n // tm,),
        in_specs=row_specs
                 + [pl.BlockSpec((None, tm, p3d.shape[2]), lambda i: (layer, i, 0))]
                 + [_const_spec(c.shape) for c in consts],
        out_specs=out_specs,
        out_shape=out_shape,
        compiler_params=pltpu.CompilerParams(dimension_semantics=("arbitrary",),
                                             vmem_limit_bytes=VMEM_LIMIT),
        name="tail",
    )(*rows, *cols, p3d, *consts)
    return outs if key_block else outs[0]


def _qkv_stages(x, gn_ref, wq_ref, wk_ref, wv_ref, bc_ref, be_ref, q_out, k_out, v_out, rows, slab):
    inv_n = 1.0 / DIFF_HEAD
    hk = _rms(x, gn_ref[1:2, :]).astype(BF16)
    tq = _dot(_rms(x, gn_ref[0:1, :]), wq_ref[...])
    yield
    ms = _head_reduce(tq * tq, bc_ref) * inv_n
    yield
    q = tq * _head_expand(lax.rsqrt(ms + NORM_EPS), be_ref) * gn_ref[2:3, :]
    q_out[rows, :] = q.astype(q_out.dtype)
    yield
    tk = _dot(hk, wk_ref[...])
    yield
    ms = _head_reduce(tk * tk, bc_ref) * inv_n
    yield
    k_out[rows, :] = (tk * _head_expand(lax.rsqrt(ms + NORM_EPS), be_ref) * gn_ref[3:4, :]).astype(k_out.dtype)
    yield
    v_out[slab] = _dot(hk, wv_ref[...]).T.astype(v_out.dtype)


def _bucket_tiles(tb):
    i = np.arange(tb, dtype=np.int64)[None, :]
    j = np.arange(tb, dtype=np.int64)[:, None]

    def bucket(rel):
        n = np.maximum(rel, 0)
        max_exact = N_BUCKETS // 2
        nf = np.maximum(n, 1).astype(np.float32)
        large = max_exact + (np.log(nf / np.float32(max_exact)) / np.float32(math.log(MAX_DIST / max_exact))
                             * np.float32(N_BUCKETS - max_exact)).astype(np.int32)
        large = np.minimum(large, N_BUCKETS - 1)
        return np.where(n < max_exact, n, large).astype(np.int32)

    diag = np.where(i - j >= 0, bucket(i - j), -1)
    near = bucket(tb + i - j)
    tiles = np.stack([diag, near]).astype(np.int32)
    return np.concatenate([tiles, tiles], axis=2)


def _bias_kernel(tab_ref, bucket_ref, out_ref):
    h = pl.program_id(0)
    b = bucket_ref[...]
    far = tab_ref[N_BUCKETS - 1, h]
    acc = jnp.where(b < 0, NEG_BIG, 0.0)
    for n in range(N_BUCKETS - 1):
        acc = jnp.where(b == n, (tab_ref[n, h] - far) * LOG2E, acc)
    out_ref[0] = acc


def _bias_tiles(rel_bias, tb):
    buckets = jnp.asarray(_bucket_tiles(tb))
    nh = rel_bias.shape[1]
    return pl.pallas_call(
        _bias_kernel,
        grid=(nh,),
        in_specs=[pl.BlockSpec(memory_space=pltpu.SMEM),
                  pl.BlockSpec((2, tb, 2 * tb), lambda h: (0, 0, 0))],
        out_specs=pl.BlockSpec((1, 2, tb, 2 * tb), lambda h: (h, 0, 0, 0)),
        out_shape=jax.ShapeDtypeStruct((nh, 2, tb, 2 * tb), F32),
        compiler_params=pltpu.CompilerParams(dimension_semantics=("arbitrary",)),
        name="bias",
    )(rel_bias, buckets)


def _attn_kernel(q_ref, k_ref, vt_ref, bias_ref, lam_ref, g_ref, o_ref, m_ref, l_ref, acc_ref,
                 *, tb, nhead, out_scale, lambda_init, bounded):
    qi = pl.program_id(2)
    heads = range(nhead)
    lane = lax.broadcasted_iota(jnp.int32, (1, LANES), 1)

    def head_lanes(h):
        return slice(h * LANES, (h + 1) * LANES)

    def stacked_q(h):
        q = q_ref[:, head_lanes(h)]
        zero = jnp.zeros_like(q)
        return jnp.concatenate([jnp.where(lane < DIFF_HEAD, q, zero),
                                jnp.where(lane < DIFF_HEAD, zero, q)], axis=0)

    qs = [stacked_q(h) for h in heads]
    m_ref[...] = jnp.full_like(m_ref, NEG_BIG)
    l_ref[...] = jnp.zeros_like(l_ref)
    acc_ref[...] = jnp.zeros_like(acc_ref)

    def advance(j, biases):
        nblk = len(biases)
        rows = pl.ds(pl.multiple_of(j * tb, tb), nblk * tb)
        logits = lambda h: _dot_nt(k_ref[rows, head_lanes(h)], qs[h])
        ahead = min(3, nhead)
        st = {h: logits(h) for h in range(ahead)}
        spans = []
        for i, b in enumerate(biases):
            if b is None and spans and spans[-1][2] is None:
                spans[-1] = (spans[-1][0], i + 1, None)
            else:
                spans.append((i, i + 1, b))
        for h in heads:
            if h + ahead < nhead:
                st[h + ahead] = logits(h + ahead)
            s = st.pop(h)
            parts = [s[i * tb:e * tb] if b is None else s[i * tb:e * tb] + bias_ref[h, b] for i, e, b in spans]
            if bounded:
                ps = [jnp.exp2(x) for x in parts]
                l_new = l_ref[h]
            else:
                m_old = m_ref[h]
                m_new = m_old
                for x in parts:
                    m_new = jnp.maximum(m_new, jnp.max(x, axis=0, keepdims=True))
                alpha = jnp.exp2(m_old - m_new)
                ps = [jnp.exp2(x - m_new) for x in parts]
                l_new = alpha * l_ref[h]
                m_ref[h] = m_new
            for p in ps:
                l_new = l_new + jnp.sum(p, axis=0, keepdims=True)
            l_ref[h] = l_new
            pv = None
            for (i, e, _), p in zip(spans, ps):
                p = p.astype(BF16)
                for b in range(i, e):
                    term = _dot(vt_ref[j + b, head_lanes(h), :], p[(b - i) * tb:(b - i + 1) * tb])
                    pv = term if pv is None else pv + term
            acc_ref[h] = acc_ref[h] + pv if bounded else alpha * acc_ref[h] + pv

    def far_blocks(t, carry):
        for i in range(0, FAR_BLOCKS, 2):
            advance(FAR_BLOCKS * t + i, (None, None))
        return carry

    nfar = jnp.maximum(qi - 1, 0)
    lax.fori_loop(0, nfar // FAR_BLOCKS, far_blocks, 0)

    @pl.when(qi == 0)
    def _():
        advance(0, (0,))

    for left in range(FAR_BLOCKS):
        @pl.when((qi >= 1) & (nfar % FAR_BLOCKS == left))
        def _(left=left):
            for i in range(0, left - 1, 2):
                advance(qi - 1 - left + i, (None, None))
            advance(qi - 1 - left % 2, (None,) * (left % 2) + (1, 0))

    lam = lam_ref[...]
    lam_full = (jnp.exp(jnp.sum(lam[0:1] * lam[1:2], axis=1, keepdims=True))
                - jnp.exp(jnp.sum(lam[2:3] * lam[3:4], axis=1, keepdims=True)) + lambda_init)
    for h in heads:
        o = acc_ref[h] * (1.0 / l_ref[h])
        o = o[:, :tb] - lam_full * o[:, tb:]
        o = o * lax.rsqrt(jnp.mean(o * o, axis=0, keepdims=True) + NORM_EPS)
        o_ref[head_lanes(h), :] = (o * (g_ref[...] * out_scale)).astype(o_ref.dtype)


def _attn(q, k, vt, bias, lam, subln_g, batch, seq, tb, nhead, layer_idx, bounded):
    n, d = q.shape
    ngroup = d // (LANES * nhead)
    width = LANES * nhead
    nq = seq // tb
    lambda_init = 0.8 - 0.6 * math.exp(-0.3 * layer_idx)
    q_spec = pl.BlockSpec((tb, width), lambda b, h, i: (b * nq + i, h))
    return pl.pallas_call(
        functools.partial(_attn_kernel, tb=tb, nhead=nhead, out_scale=1.0 - lambda_init,
                          lambda_init=lambda_init, bounded=bounded),
        grid=(batch, ngroup, nq),
        in_specs=[q_spec,
                  pl.BlockSpec((seq, width), lambda b, h, i: (b, h)),
                  pl.BlockSpec((nq, width, tb), lambda b, h, i: (b, h, 0)),
                  pl.BlockSpec((nhead, 2, tb, 2 * tb), lambda b, h, i: (h, 0, 0, 0)),
                  pl.BlockSpec(lam.shape, lambda b, h, i: (0, 0)),
                  pl.BlockSpec(subln_g.shape, lambda b, h, i: (0, 0))],
        out_specs=pl.BlockSpec((width, tb), lambda b, h, i: (h, b * nq + i)),
        out_shape=jax.ShapeDtypeStruct((d, n), BF16),
        scratch_shapes=[pltpu.VMEM((nhead, 1, 2 * tb), F32), pltpu.VMEM((nhead, 1, 2 * tb), F32),
                        pltpu.VMEM((nhead, LANES, 2 * tb), F32)],
        compiler_params=pltpu.CompilerParams(
            dimension_semantics=("arbitrary", "arbitrary", "arbitrary"),
            vmem_limit_bytes=VMEM_LIMIT),
        name="attn",
    )(q, k, vt, bias, lam, subln_g)


def kernel(x, p, norm_g, mlp_w1, mlp_w2, ple_w_up, ple_w_gate, rwkv_mix, rwkv_w_rkvo, rwkv_w0, rwkv_w1, rwkv_w2, rwkv_a0, rwkv_a1, rwkv_a2, rwkv_g1, rwkv_g2, rwkv_k_k, rwkv_k_a, rwkv_r_k, rwkv_ln_w, rwkv_ln_b, kv_norm_g, w_k_shared, w_v_shared, k_norm_g, diff_w_q, diff_q_norm_g, diff_lam, diff_subln_g, diff_w_o, rel_bias):
    batch, seq, d = x.shape
    assert d == D_MODEL and norm_g.shape[0] == 2
    n = batch * seq
    tm = min(256, seq)
    tb_rec = min(256, seq)
    tb_attn = min(256, seq)
    assert seq % tm == 0 and seq % tb_rec == 0 and seq % tb_attn == 0 and tb_attn >= MAX_DIST

    bf = lambda w: w.astype(BF16)
    x2d = x.reshape(n, d)
    p3d = p.reshape(p.shape[0], n, p.shape[-1])
    head_id = jnp.arange(d, dtype=jnp.int32) // RWKV_HEAD
    slot = jnp.arange(LANES, dtype=jnp.int32)
    bc = ((head_id[:, None] == slot[None, :] % N_HEADS) & (slot[None, :] < 2 * N_HEADS)).astype(BF16)
    be = bc.T

    vecs = jnp.stack([norm_g[0, 0], rwkv_w0[0], rwkv_a0[0], rwkv_k_k[0], rwkv_k_a[0],
                      rwkv_r_k[0].reshape(d), rwkv_ln_w[0], rwkv_ln_b[0]])
    y, g, bonus = _rwkv(
        x2d, vecs, rwkv_mix[0], bf(rwkv_w_rkvo[0, 0]), bf(rwkv_w_rkvo[0, 1]), bf(rwkv_w_rkvo[0, 2]),
        bf(rwkv_w1[0]), bf(rwkv_w2[0]), bf(rwkv_a1[0]), bf(rwkv_a2[0]), bf(rwkv_g1[0]), bf(rwkv_g2[0]),
        bc, be, batch, seq, tb_rec)
    reps = d // DIFF_HEAD
    gnq = jnp.stack([norm_g[1, 0], kv_norm_g,
                     jnp.tile(diff_q_norm_g[0], reps) * (DIFF_HEAD ** -0.5 * LOG2E), jnp.tile(k_norm_g, reps)])
    x2d, q, kq, vq = _tail(functools.partial(_tail_rwkv_kernel, nslab=2), (x2d, y, bonus, g), (), p3d, 0,
                           (vecs, bc, be, gnq, bf(diff_w_q[0]), bf(w_k_shared), bf(w_v_shared)), norm_g[0, 1:3],
                           bf(rwkv_w_rkvo[0, 3]), bf(mlp_w1[0]), bf(mlp_w2[0]), bf(ple_w_up[0]),
                           bf(ple_w_gate[0]), 2 * tb_attn, key_block=tb_attn)

    bias = _bias_tiles(rel_bias, tb_attn)
    logit_bound = 1.02 * LOG2E * (DIFF_HEAD ** 0.5 * jnp.max(jnp.abs(diff_q_norm_g[0] * k_norm_g))
                                  + jnp.max(jnp.abs(rel_bias - rel_bias[-1:])))
    attn = functools.partial(_attn, q, kq, vq, bias, diff_lam[0], diff_subln_g[0].reshape(LANES, 1),
                             batch, seq, tb_attn, 4, 1)
    o = lax.cond(logit_bound <= LOGIT_BOUND, lambda: attn(True), lambda: attn(False))
    x2d = _tail(functools.partial(_tail_attn_kernel, nslab=2), (x2d,), (o,), p3d, 1, (), norm_g[1, 1:3],
                bf(diff_w_o[0]), bf(mlp_w1[1]), bf(mlp_w2[1]), bf(ple_w_up[1]), bf(ple_w_gate[1]), 2 * tm)
    return x2d.reshape(batch, seq, d)
```

```python
import functools
import math

import numpy as np
import jax
import jax.numpy as jnp
from jax import lax
from jax.experimental import pallas as pl
from jax.experimental.pallas import tpu as pltpu

F32 = jnp.float32
BF16 = jnp.bfloat16

D_MODEL = 1024
RWKV_HEAD = 64
N_HEADS = D_MODEL // RWKV_HEAD
DIFF_HEAD = 64
DIFF_HEADS = 8
LANES = 128
CHUNK = 64
MIN_PACK = 16
SETUP_AHEAD = 3
PRE_STAGE_EVERY = 5
FF_CHUNK = 1024
NORM_EPS = 1e-6
GN_EPS = 64e-5
N_BUCKETS = 32
MAX_DIST = 128
NEG_BIG = -1e30
LOG2E = math.log2(math.e)
FAR_BLOCKS = 4
LOGIT_BOUND = 60.0
VMEM_LIMIT = 60 * 1024 * 1024


def _dot(a, b):
    return jnp.dot(a.astype(BF16), b.astype(BF16), preferred_element_type=F32)


def _dot_nt(a, b):
    return lax.dot_general(a.astype(BF16), b.astype(BF16), (((1,), (1,)), ((), ())),
                           preferred_element_type=F32)


def _dot_tn(a, b):
    return lax.dot_general(a.astype(BF16), b.astype(BF16), (((0,), (0,)), ((), ())),
                           preferred_element_type=F32)


def _rms(x, g):
    return x * lax.rsqrt(jnp.mean(x * x, axis=-1, keepdims=True) + NORM_EPS) * g


def _sigmoid(x):
    return 1.0 / (1.0 + jnp.exp(-x))


def _head_reduce(x, bc_ref):
    return _dot(x, bc_ref[...])


def _head_expand(c, be_ref):
    lane = lax.broadcasted_iota(jnp.int32, (1, LANES), 1)
    hi = c.astype(BF16).astype(F32)
    return _dot(jnp.where(lane < N_HEADS, hi, c - hi), be_ref[...])


def _const_spec(shape):
    nd = len(shape)
    return pl.BlockSpec(shape, lambda *_: (0,) * nd, pipeline_mode=pl.Buffered(1))


def _row_spec(tm, width):
    return pl.BlockSpec((tm, width), lambda i: (i, 0))


def _pre_stages(x_ref, xp_ref, vec_ref, mix_ref, wr_ref, wk_ref, wv_ref,
                w1_ref, w2_ref, a1_ref, a2_ref, g1_ref, g2_ref, bc_ref, be_ref,
                r_out, lw_out, k_out, v_out, kk_out, a_out, g_out, bonus_out, *, first_block):
    g0 = vec_ref[0:1, :]
    h = _rms(x_ref[...], g0)
    prev = _rms(xp_ref[...], g0)[7:8, :]
    prev = jnp.where(first_block, 0.0, prev)
    row = lax.broadcasted_iota(jnp.int32, h.shape, 0)
    hprev = jnp.where(row == 0, prev, pltpu.roll(h, 1, 0))
    dx = hprev - h

    def mixed(j):
        return (h + dx * mix_ref[j:j + 1, :]).astype(BF16)

    yield
    r = _dot(mixed(0), wr_ref[...])
    r_out[...] = r
    yield
    k = _dot(mixed(2), wk_ref[...])
    yield
    v = _dot(mixed(3), wv_ref[...])
    v_out[...] = v
    yield
    t_w = jnp.tanh(_dot(mixed(1), w1_ref[...]))
    yield
    t_a = _dot(mixed(4), a1_ref[...])
    t_g = _sigmoid(_dot(mixed(5), g1_ref[...]))
    yield
    lw_out[...] = -math.exp(-0.5) * _sigmoid(vec_ref[1:2, :] + _dot(t_w, w2_ref[...]))
    yield
    a = _sigmoid(vec_ref[2:3, :] + _dot(t_a, a2_ref[...]))
    a_out[...] = a
    yield
    g_out[...] = _dot(t_g, g2_ref[...]).astype(g_out.dtype)
    yield
    kk = k * vec_ref[3:4, :]
    ss = _head_reduce(kk * kk, bc_ref)
    k = k * (1.0 + (a - 1.0) * vec_ref[4:5, :])
    k_out[...] = k
    bsum = _head_reduce(r * k * vec_ref[5:6, :], bc_ref)
    yield
    kk_out[...] = kk * _head_expand(1.0 / jnp.maximum(jnp.sqrt(ss), 1e-12), be_ref)
    bonus_out[...] = (_head_expand(bsum, be_ref) * v).astype(bonus_out.dtype)


def _rec_stages(r_ref, lw_ref, k_ref, v_ref, kk_ref, a_ref, y_ref, s_ref, *, nchunk, npair, first_block):
    c2 = 2 * CHUNK
    width = npair * LANES

    @pl.when(first_block)
    def _():
        s_ref[...] = jnp.zeros_like(s_ref)

    ri = lax.broadcasted_iota(jnp.int32, (c2, c2), 0)
    ci = lax.broadcasted_iota(jnp.int32, (c2, c2), 1)
    same_block = {m: (ri // m) == (ci // m) for m in (2, 16, 32, 64)}
    sibling_block = {m: ((ri // m) ^ (ci // m)) == 1 for m in (2, 4, 8, 16, 32)}
    eye = (ri == ci).astype(F32)
    gi = lax.broadcasted_iota(jnp.int32, (2 * c2, 2 * c2), 0)
    gj = lax.broadcasted_iota(jnp.int32, (2 * c2, 2 * c2), 1)
    gram_mask = (gj % CHUNK) < (gi % CHUNK) + jnp.where(gi < c2, 0, 1)
    lo_lane = (lax.broadcasted_iota(jnp.int32, (1, width), 1) % LANES) < CHUNK
    tri = (lax.broadcasted_iota(jnp.int32, (CHUNK, CHUNK), 1)
           <= lax.broadcasted_iota(jnp.int32, (CHUNK, CHUNK), 0)).astype(BF16)
    pairs = range(npair)

    def pack(m, blk):
        return m.reshape(c2 // blk, blk, c2).sum(axis=0)

    def unpack(p, blk):
        return jnp.where(same_block[blk], jnp.tile(p, (c2 // blk, 1)), 0.0)

    def per_pair(x):
        return [x[:, p * LANES:(p + 1) * LANES] for p in pairs]

    def stacked(x):
        return [jnp.concatenate([a, b], axis=0)
                for a, b in zip(per_pair(jnp.where(lo_lane, x, 0.0)), per_pair(jnp.where(lo_lane, 0.0, x)))]

    def setup(c, out):
        sl = slice(c * CHUNK, (c + 1) * CHUNK)
        lw = lw_ref[sl, :]
        l_hi = lw.astype(BF16)
        l_mid = (lw - l_hi.astype(F32)).astype(BF16)
        l_lo = (lw - l_hi.astype(F32) - l_mid.astype(F32)).astype(BF16)
        cum = _dot(tri, l_hi) + _dot(tri, l_mid) + _dot(tri, l_lo)
        tot = cum[CHUNK - 1:CHUNK, :]
        e_neg = jnp.exp(-cum)
        e_rest = jnp.exp(tot - cum)
        k = k_ref[sl, :]
        kk = kk_ref[sl, :]
        kb = kk * a_ref[sl, :]
        at = stacked(-kk * jnp.exp(cum - lw))
        rt = stacked(r_ref[sl, :] * jnp.exp(cum))
        bt = stacked(kb * e_neg)
        kt = stacked(k * e_neg)
        out["bkw"] = [jnp.concatenate([b, kq], axis=0).astype(BF16)
                      for b, kq in zip(stacked(kb * e_rest), stacked(k * e_rest))]
        out["vm"] = [v.astype(BF16) for v in stacked(v_ref[sl, :])]
        out["decay"] = per_pair(jnp.exp(tot))
        lhs = out["lhs"] = [jnp.concatenate([a, r], axis=0).astype(BF16) for a, r in zip(at, rt)]
        yield
        gram = [jnp.where(gram_mask, _dot_nt(l, jnp.concatenate([b, kq], axis=0)), 0.0)
                for l, b, kq in zip(lhs, bt, kt)]
        out["a_k"] = [g[:c2, c2:].astype(BF16) for g in gram]
        out["r_bk"] = [g[c2:].astype(BF16) for g in gram]
        abd = [g[:c2, :c2] for g in gram]
        yield
        x = [eye + jnp.where(same_block[2], a, 0.0) for a in abd]
        xp, xp_blk = None, None
        for m in (2, 4, 8, 16, 32):
            blk = max(2 * m, MIN_PACK)
            if blk != xp_blk:
                xp, xp_blk = [pack(xi, blk) for xi in x], blk
            below = [jnp.where(sibling_block[m], a, 0.0) for a in abd]
            lx = [_dot(pack(l, blk), xi) for l, xi in zip(below, x)]
            yield
            xlx = [_dot(xpi, unpack(y, blk)) for xpi, y in zip(xp, lx)]
            yield
            x = [xi + unpack(y, blk) for xi, y in zip(x, xlx)]
            xp = [xpi + y for xpi, y in zip(xp, xlx)]
        out["tinv"] = [xi.astype(BF16) for xi in x]

    def carry(c, pre):
        sl = slice(c * CHUNK, (c + 1) * CHUNK)
        vm = pre["vm"]
        hs = [_dot_nt(l, s_ref[p]) for p, l in zip(pairs, pre["lhs"])]
        yield
        w = [h[:c2] + _dot(g, v) for h, g, v in zip(hs, pre["a_k"], vm)]
        yield
        u = [_dot(ti, wi) for ti, wi in zip(pre["tinv"], w)]
        uv = [jnp.concatenate([ui.astype(BF16), v], axis=0) for ui, v in zip(u, vm)]
        yield
        ym = [h[c2:] + _dot(g, x2) for h, g, x2 in zip(hs, pre["r_bk"], uv)]
        y_ref[sl, :] = jnp.concatenate([yi[:CHUNK] + yi[CHUNK:] for yi in ym], axis=1)
        yield
        for p in pairs:
            s_ref[p] = s_ref[p] * pre["decay"][p] + _dot_tn(uv[p], pre["bkw"][p])

    pre = [dict() for _ in range(nchunk)]

    def take_turns(active):
        while active:
            for gen in list(active):
                if next(gen, StopIteration) is StopIteration:
                    active.remove(gen)
                else:
                    yield

    yield from take_turns([setup(c, pre[c]) for c in range(min(SETUP_AHEAD, nchunk))])
    for c in range(nchunk):
        ahead = c + SETUP_AHEAD
        yield from take_turns([carry(c, pre[c])] + ([setup(ahead, pre[ahead])] if ahead < nchunk else []))


def _drain(gen):
    for _ in gen:
        pass


def _rwkv_kernel(*refs, nchunk, npair, nt):
    ins, (y_out, g_out, bonus_out), (r_s, lw_s, k_s, v_s, kk_s, a_s, s_ref) = refs[:15], refs[15:18], refs[18:]
    s = pl.program_id(1)
    bufs = (r_s, lw_s, k_s, v_s, kk_s, a_s)
    pre = lambda: _pre_stages(*ins, *[b.at[s % 2] for b in bufs], g_out, bonus_out, first_block=s == 0)
    rec = lambda: _rec_stages(*[b.at[(s + 1) % 2] for b in bufs], y_out, s_ref, nchunk=nchunk, npair=npair,
                              first_block=s == 1)

    @pl.when(s == 0)
    def _():
        _drain(pre())

    @pl.when((s > 0) & (s < nt))
    def _():
        side = pre()
        for i, _ in enumerate(rec()):
            if i % PRE_STAGE_EVERY == 0:
                next(side, None)
        _drain(side)

    @pl.when(s == nt)
    def _():
        _drain(rec())


def _rwkv(x2d, vecs, mix, wr, wk, wv, w1, w2, a1, a2, g1, g2, bc, be, batch, seq, tb):
    n, d = x2d.shape
    nt = seq // tb
    npair = d // LANES
    consts = (vecs, mix, wr, wk, wv, w1, w2, a1, a2, g1, g2, bc, be)
    pre_blk = lambda b, s: b * nt + jnp.minimum(s, nt - 1)
    rec_blk = lambda b, s: b * nt + jnp.maximum(s - 1, 0)
    in_spec = pl.BlockSpec((tb, d), lambda b, s: (pre_blk(b, s), 0))
    prev_spec = pl.BlockSpec((8, d), lambda b, s: (jnp.maximum(pre_blk(b, s) * (tb // 8) - 1, 0), 0))
    return pl.pallas_call(
        functools.partial(_rwkv_kernel, nchunk=tb // CHUNK, npair=npair, nt=nt),
        grid=(batch, nt + 1),
        in_specs=[in_spec, prev_spec] + [_const_spec(c.shape) for c in consts],
        out_specs=[pl.BlockSpec((tb, d), lambda b, s: (rec_blk(b, s), 0)), in_spec, in_spec],
        out_shape=[jax.ShapeDtypeStruct((n, d), F32)] + [jax.ShapeDtypeStruct((n, d), BF16)] * 2,
        scratch_shapes=[pltpu.VMEM((2, tb, d), F32)] * 6 + [pltpu.VMEM((npair, LANES, LANES), F32)],
        compiler_params=pltpu.CompilerParams(dimension_semantics=("arbitrary", "arbitrary"),
                                             vmem_limit_bytes=VMEM_LIMIT),
        name="rwkv",
    )(x2d, x2d, *consts)


def _tail_stages(x, mixer_out, p, gn_ref, w1_ref, w2_ref, wup_ref, wgate_ref, store):
    x = x + mixer_out()
    yield
    hn = _rms(x, gn_ref[0:1, :]).astype(BF16)
    d_ff = w1_ref.shape[1]
    acc = x
    for c in range(0, d_ff, FF_CHUNK):
        mid = jnp.maximum(_dot(hn, w1_ref[:, c:c + FF_CHUNK]), 0.0)
        yield
        acc = acc + _dot(mid * mid, w2_ref[c:c + FF_CHUNK, :])
        yield
    gate = _sigmoid(_dot(_rms(acc, gn_ref[1:2, :]), wgate_ref[...]))
    yield
    store(acc + _dot(p, wup_ref[...]) * gate)


def _round_robin(gens):
    gens = list(gens)
    while gens:
        gens = [g for g in gens if next(g, StopIteration) is not StopIteration]


def _tail_attn_kernel(x_ref, o_ref, p_ref, gn_ref, wo_ref, w1_ref, w2_ref, wup_ref, wgate_ref, out_ref, *, nslab):
    rows_per = x_ref.shape[0] // nslab

    def slab(i):
        rows = slice(i * rows_per, (i + 1) * rows_per)

        def store(v):
            out_ref[rows, :] = v

        return _tail_stages(x_ref[rows, :], lambda: _dot_tn(o_ref[:, rows], wo_ref[...]), p_ref[rows, :],
                            gn_ref, w1_ref, w2_ref, wup_ref, wgate_ref, store)

    _round_robin(slab(i) for i in range(nslab))


def _tail_rwkv_kernel(x_ref, y_ref, bonus_ref, g_ref, p_ref, vec_ref, bc_ref, be_ref, gnq_ref, wq_ref, wk_ref,
                      wv_ref, gn_ref, wo_ref, w1_ref, w2_ref, wup_ref, wgate_ref, out_ref, q_out, k_out, v_out,
                      *, nslab):
    rows_per = x_ref.shape[0] // nslab
    assert v_out.shape == (nslab, x_ref.shape[1], rows_per)
    inv_n = 1.0 / RWKV_HEAD

    def slab(i):
        rows = slice(i * rows_per, (i + 1) * rows_per)
        y = y_ref[rows, :]
        mean = _head_reduce(y, bc_ref) * inv_n
        yield
        yc = y - _head_expand(mean, be_ref)
        yield
        var = _head_reduce(yc * yc, bc_ref) * inv_n
        yield
        yn = yc * _head_expand(lax.rsqrt(var + GN_EPS), be_ref) * vec_ref[6:7, :] + vec_ref[7:8, :]
        yield
        fresh = []

        def store(v):
            out_ref[rows, :] = v
            fresh.append(v)

        gated = (yn + bonus_ref[rows, :]) * g_ref[rows, :]
        yield from _tail_stages(x_ref[rows, :], lambda: _dot(gated, wo_ref[...]), p_ref[rows, :],
                                gn_ref, w1_ref, w2_ref, wup_ref, wgate_ref, store)
        yield
        yield from _qkv_stages(fresh[0], gnq_ref, wq_ref, wk_ref, wv_ref, bc_ref, be_ref, q_out, k_out, v_out,
                               rows, i)

    _round_robin(slab(i) for i in range(nslab))


def _tail(kernel_fn, rows, cols, p3d, layer, extra_consts, gn, wo, w1, w2, wup, wgate, tm, key_block=None):
    n, d = rows[0].shape
    row_specs = [_row_spec(tm, d)] * len(rows) + [pl.BlockSpec((d, tm), lambda i: (0, i))] * len(cols)
    consts = tuple(extra_consts) + (gn, wo, w1, w2, wup, wgate)
    out_specs = [_row_spec(tm, d)]
    out_shape = [jax.ShapeDtypeStruct((n, d), F32)]
    if key_block:
        out_specs += [_row_spec(tm, d)] * 2 + [pl.BlockSpec((tm // key_block, d, key_block), lambda i: (i, 0, 0))]
        out_shape += [jax.ShapeDtypeStruct((n, d), BF16)] * 2
        out_shape += [jax.ShapeDtypeStruct((n // key_block, d, key_block), BF16)]
    outs = pl.pallas_call(
        kernel_fn,
        grid=(n // tm,),
        in_specs=row_specs
                 + [pl.BlockSpec((None, tm, p3d.shape[2]), lambda i: (layer, i, 0))]
                 + [_const_spec(c.shape) for c in consts],
        out_specs=out_specs,
        out_shape=out_shape,
        compiler_params=pltpu.CompilerParams(dimension_semantics=("arbitrary",),
                                             vmem_limit_bytes=VMEM_LIMIT),
        name="tail",
    )(*rows, *cols, p3d, *consts)
    return outs if key_block else outs[0]


def _qkv_stages(x, gn_ref, wq_ref, wk_ref, wv_ref, bc_ref, be_ref, q_out, k_out, v_out, rows, slab):
    inv_n = 1.0 / DIFF_HEAD
    hk = _rms(x, gn_ref[1:2, :]).astype(BF16)
    tq = _dot(_rms(x, gn_ref[0:1, :]), wq_ref[...])
    yield
    ms = _head_reduce(tq * tq, bc_ref) * inv_n
    yield
    q = tq * _head_expand(lax.rsqrt(ms + NORM_EPS), be_ref) * gn_ref[2:3, :]
    q_out[rows, :] = q.astype(q_out.dtype)
    yield
    tk = _dot(hk, wk_ref[...])
    yield
    ms = _head_reduce(tk * tk, bc_ref) * inv_n
    yield
    k_out[rows, :] = (tk * _head_expand(lax.rsqrt(ms + NORM_EPS), be_ref) * gn_ref[3:4, :]).astype(k_out.dtype)
    yield
    v_out[slab] = _dot(hk, wv_ref[...]).T.astype(v_out.dtype)


def _bucket_tiles(tb):
    i = np.arange(tb, dtype=np.int64)[None, :]
    j = np.arange(tb, dtype=np.int64)[:, None]

    def bucket(rel):
        n = np.maximum(rel, 0)
        max_exact = N_BUCKETS // 2
        nf = np.maximum(n, 1).astype(np.float32)
        large = max_exact + (np.log(nf / np.float32(max_exact)) / np.float32(math.log(MAX_DIST / max_exact))
                             * np.float32(N_BUCKETS - max_exact)).astype(np.int32)
        large = np.minimum(large, N_BUCKETS - 1)
        return np.where(n < max_exact, n, large).astype(np.int32)

    diag = np.where(i - j >= 0, bucket(i - j), -1)
    near = bucket(tb + i - j)
    tiles = np.stack([diag, near]).astype(np.int32)
    return np.concatenate([tiles, tiles], axis=2)


def _bias_kernel(tab_ref, bucket_ref, out_ref):
    h = pl.program_id(0)
    b = bucket_ref[...]
    far = tab_ref[N_BUCKETS - 1, h]
    acc = jnp.where(b < 0, NEG_BIG, 0.0)
    for n in range(N_BUCKETS - 1):
        acc = jnp.where(b == n, (tab_ref[n, h] - far) * LOG2E, acc)
    out_ref[0] = acc


def _bias_tiles(rel_bias, tb):
    buckets = jnp.asarray(_bucket_tiles(tb))
    nh = rel_bias.shape[1]
    return pl.pallas_call(
        _bias_kernel,
        grid=(nh,),
        in_specs=[pl.BlockSpec(memory_space=pltpu.SMEM),
                  pl.BlockSpec((2, tb, 2 * tb), lambda h: (0, 0, 0))],
        out_specs=pl.BlockSpec((1, 2, tb, 2 * tb), lambda h: (h, 0, 0, 0)),
        out_shape=jax.ShapeDtypeStruct((nh, 2, tb, 2 * tb), F32),
        compiler_params=pltpu.CompilerParams(dimension_semantics=("arbitrary",)),
        name="bias",
    )(rel_bias, buckets)


def _attn_kernel(q_ref, k_ref, vt_ref, bias_ref, lam_ref, g_ref, o_ref, m_ref, l_ref, acc_ref,
                 *, tb, nhead, out_scale, lambda_init, bounded):
    qi = pl.program_id(2)
    heads = range(nhead)
    lane = lax.broadcasted_iota(jnp.int32, (1, LANES), 1)

    def head_lanes(h):
        return slice(h * LANES, (h + 1) * LANES)

    def stacked_q(h):
        q = q_ref[:, head_lanes(h)]
        zero = jnp.zeros_like(q)
        return jnp.concatenate([jnp.where(lane < DIFF_HEAD, q, zero),
                                jnp.where(lane < DIFF_HEAD, zero, q)], axis=0)

    qs = [stacked_q(h) for h in heads]
    m_ref[...] = jnp.full_like(m_ref, NEG_BIG)
    l_ref[...] = jnp.zeros_like(l_ref)
    acc_ref[...] = jnp.zeros_like(acc_ref)

    def advance(j, biases):
        nblk = len(biases)
        rows = pl.ds(pl.multiple_of(j * tb, tb), nblk * tb)
        logits = lambda h: _dot_nt(k_ref[rows, head_lanes(h)], qs[h])
        ahead = min(3, nhead)
        st = {h: logits(h) for h in range(ahead)}
        spans = []
        for i, b in enumerate(biases):
            if b is None and spans and spans[-1][2] is None:
                spans[-1] = (spans[-1][0], i + 1, None)
            else:
                spans.append((i, i + 1, b))
        for h in heads:
            if h + ahead < nhead:
                st[h + ahead] = logits(h + ahead)
            s = st.pop(h)
            parts = [s[i * tb:e * tb] if b is None else s[i * tb:e * tb] + bias_ref[h, b] for i, e, b in spans]
            if bounded:
                ps = [jnp.exp2(x) for x in parts]
                l_new = l_ref[h]
            else:
                m_old = m_ref[h]
                m_new = m_old
                for x in parts:
                    m_new = jnp.maximum(m_new, jnp.max(x, axis=0, keepdims=True))
                alpha = jnp.exp2(m_old - m_new)
                ps = [jnp.exp2(x - m_new) for x in parts]
                l_new = alpha * l_ref[h]
                m_ref[h] = m_new
            for p in ps:
                l_new = l_new + jnp.sum(p, axis=0, keepdims=True)
            l_ref[h] = l_new
            pv = None
            for (i, e, _), p in zip(spans, ps):
                p = p.astype(BF16)
                for b in range(i, e):
                    term = _dot(vt_ref[j + b, head_lanes(h), :], p[(b - i) * tb:(b - i + 1) * tb])
                    pv = term if pv is None else pv + term
            acc_ref[h] = acc_ref[h] + pv if bounded else alpha * acc_ref[h] + pv

    def far_blocks(t, carry):
        for i in range(0, FAR_BLOCKS, 2):
            advance(FAR_BLOCKS * t + i, (None, None))
        return carry

    nfar = jnp.maximum(qi - 1, 0)
    lax.fori_loop(0, nfar // FAR_BLOCKS, far_blocks, 0)

    @pl.when(qi == 0)
    def _():
        advance(0, (0,))

    for left in range(FAR_BLOCKS):
        @pl.when((qi >= 1) & (nfar % FAR_BLOCKS == left))
        def _(left=left):
            for i in range(0, left - 1, 2):
                advance(qi - 1 - left + i, (None, None))
            advance(qi - 1 - left % 2, (None,) * (left % 2) + (1, 0))

    lam = lam_ref[...]
    lam_full = (jnp.exp(jnp.sum(lam[0:1] * lam[1:2], axis=1, keepdims=True))
                - jnp.exp(jnp.sum(lam[2:3] * lam[3:4], axis=1, keepdims=True)) + lambda_init)
    for h in heads:
        o = acc_ref[h] * (1.0 / l_ref[h])
        o = o[:, :tb] - lam_full * o[:, tb:]
        o = o * lax.rsqrt(jnp.mean(o * o, axis=0, keepdims=True) + NORM_EPS)
        o_ref[head_lanes(h), :] = (o * (g_ref[...] * out_scale)).astype(o_ref.dtype)


def _attn(q, k, vt, bias, lam, subln_g, batch, seq, tb, nhead, layer_idx, bounded):
    n, d = q.shape
    ngroup = d // (LANES * nhead)
    width = LANES * nhead
    nq = seq // tb
    lambda_init = 0.8 - 0.6 * math.exp(-0.3 * layer_idx)
    q_spec = pl.BlockSpec((tb, width), lambda b, h, i: (b * nq + i, h))
    return pl.pallas_call(
        functools.partial(_attn_kernel, tb=tb, nhead=nhead, out_scale=1.0 - lambda_init,
                          lambda_init=lambda_init, bounded=bounded),
        grid=(batch, ngroup, nq),
        in_specs=[q_spec,
                  pl.BlockSpec((seq, width), lambda b, h, i: (b, h)),
                  pl.BlockSpec((nq, width, tb), lambda b, h, i: (b, h, 0)),
                  pl.BlockSpec((nhead, 2, tb, 2 * tb), lambda b, h, i: (h, 0, 0, 0)),
                  pl.BlockSpec(lam.shape, lambda b, h, i: (0, 0)),
                  pl.BlockSpec(subln_g.shape, lambda b, h, i: (0, 0))],
        out_specs=pl.BlockSpec((width, tb), lambda b, h, i: (h, b * nq + i)),
        out_shape=jax.ShapeDtypeStruct((d, n), BF16),
        scratch_shapes=[pltpu.VMEM((nhead, 1, 2 * tb), F32), pltpu.VMEM((nhead, 1, 2 * tb), F32),
                        pltpu.VMEM((nhead, LANES, 2 * tb), F32)],
        compiler_params=pltpu.CompilerParams(
            dimension_semantics=("arbitrary", "arbitrary", "arbitrary"),
            vmem_limit_bytes=VMEM_LIMIT),
        name="attn",
    )(q, k, vt, bias, lam, subln_g)


def kernel(x, p, norm_g, mlp_w1, mlp_w2, ple_w_up, ple_w_gate, rwkv_mix, rwkv_w_rkvo, rwkv_w0, rwkv_w1, rwkv_w2, rwkv_a0, rwkv_a1, rwkv_a2, rwkv_g1, rwkv_g2, rwkv_k_k, rwkv_k_a, rwkv_r_k, rwkv_ln_w, rwkv_ln_b, kv_norm_g, w_k_shared, w_v_shared, k_norm_g, diff_w_q, diff_q_norm_g, diff_lam, diff_subln_g, diff_w_o, rel_bias):
    batch, seq, d = x.shape
    assert d == D_MODEL and norm_g.shape[0] == 2
    n = batch * seq
    tm = min(256, seq)
    tb_rec = min(256, seq)
    tb_attn = min(256, seq)
    assert seq % tm == 0 and seq % tb_rec == 0 and seq % tb_attn == 0 and tb_attn >= MAX_DIST

    bf = lambda w: w.astype(BF16)
    x2d = x.reshape(n, d)
    p3d = p.reshape(p.shape[0], n, p.shape[-1])
    head_id = jnp.arange(d, dtype=jnp.int32) // RWKV_HEAD
    slot = jnp.arange(LANES, dtype=jnp.int32)
    bc = ((head_id[:, None] == slot[None, :] % N_HEADS) & (slot[None, :] < 2 * N_HEADS)).astype(BF16)
    be = bc.T

    vecs = jnp.stack([norm_g[0, 0], rwkv_w0[0], rwkv_a0[0], rwkv_k_k[0], rwkv_k_a[0],
                      rwkv_r_k[0].reshape(d), rwkv_ln_w[0], rwkv_ln_b[0]])
    y, g, bonus = _rwkv(
        x2d, vecs, rwkv_mix[0], bf(rwkv_w_rkvo[0, 0]), bf(rwkv_w_rkvo[0, 1]), bf(rwkv_w_rkvo[0, 2]),
        bf(rwkv_w1[0]), bf(rwkv_w2[0]), bf(rwkv_a1[0]), bf(rwkv_a2[0]), bf(rwkv_g1[0]), bf(rwkv_g2[0]),
        bc, be, batch, seq, tb_rec)
    reps = d // DIFF_HEAD
    gnq = jnp.stack([norm_g[1, 0], kv_norm_g,
                     jnp.tile(diff_q_norm_g[0], reps) * (DIFF_HEAD ** -0.5 * LOG2E), jnp.tile(k_norm_g, reps)])
    x2d, q, kq, vq = _tail(functools.partial(_tail_rwkv_kernel, nslab=2), (x2d, y, bonus, g), (), p3d, 0,
                           (vecs, bc, be, gnq, bf(diff_w_q[0]), bf(w_k_shared), bf(w_v_shared)), norm_g[0, 1:3],
                           bf(rwkv_w_rkvo[0, 3]), bf(mlp_w1[0]), bf(mlp_w2[0]), bf(ple_w_up[0]),
                           bf(ple_w_gate[0]), 2 * tb_attn, key_block=tb_attn)

    bias = _bias_tiles(rel_bias, tb_attn)
    logit_bound = 1.02 * LOG2E * (DIFF_HEAD ** 0.5 * jnp.max(jnp.abs(diff_q_norm_g[0] * k_norm_g))
                                  + jnp.max(jnp.abs(rel_bias - rel_bias[-1:])))
    attn = functools.partial(_attn, q, kq, vq, bias, diff_lam[0], diff_subln_g[0].reshape(LANES, 1),
                             batch, seq, tb_attn, 4, 1)
    o = lax.cond(logit_bound <= LOGIT_BOUND, lambda: attn(True), lambda: attn(False))
    x2d = _tail(functools.partial(_tail_attn_kernel, nslab=2), (x2d,), (o,), p3d, 1, (), norm_g[1, 1:3],
                bf(diff_w_o[0]), bf(mlp_w1[1]), bf(mlp_w2[1]), bf(ple_w_up[1]), bf(ple_w_gate[1]), 2 * tm)
    return x2d.reshape(batch, seq, d)
```

```python
import functools
import math

import numpy as np
import jax
import jax.numpy as jnp
from jax import lax
from jax.experimental import pallas as pl
from jax.experimental.pallas import tpu as pltpu

F32 = jnp.float32
BF16 = jnp.bfloat16

D_MODEL = 1024
RWKV_HEAD = 64
N_HEADS = D_MODEL // RWKV_HEAD
DIFF_HEAD = 64
DIFF_HEADS = 8
LANES = 128
CHUNK = 64
MIN_PACK = 16
SETUP_AHEAD = 4
PRE_STAGE_EVERY = 5
FF_CHUNK = 1024
NORM_EPS = 1e-6
GN_EPS = 64e-5
N_BUCKETS = 32
MAX_DIST = 128
NEG_BIG = -1e30
LOG2E = math.log2(math.e)
FAR_BLOCKS = 4
LOGIT_BOUND = 60.0
VMEM_LIMIT = 60 * 1024 * 1024


def _dot(a, b):
    return jnp.dot(a.astype(BF16), b.astype(BF16), preferred_element_type=F32)


def _dot_nt(a, b):
    return lax.dot_general(a.astype(BF16), b.astype(BF16), (((1,), (1,)), ((), ())),
                           preferred_element_type=F32)


def _dot_tn(a, b):
    return lax.dot_general(a.astype(BF16), b.astype(BF16), (((0,), (0,)), ((), ())),
                           preferred_element_type=F32)


def _rms(x, g):
    return x * lax.rsqrt(jnp.mean(x * x, axis=-1, keepdims=True) + NORM_EPS) * g


def _sigmoid(x):
    return 1.0 / (1.0 + jnp.exp(-x))


def _head_reduce(x, bc_ref):
    return _dot(x, bc_ref[...])


def _head_expand(c, be_ref):
    lane = lax.broadcasted_iota(jnp.int32, (1, LANES), 1)
    hi = c.astype(BF16).astype(F32)
    return _dot(jnp.where(lane < N_HEADS, hi, c - hi), be_ref[...])


def _const_spec(shape):
    nd = len(shape)
    return pl.BlockSpec(shape, lambda *_: (0,) * nd, pipeline_mode=pl.Buffered(1))


def _row_spec(tm, width):
    return pl.BlockSpec((tm, width), lambda i: (i, 0))


def _pre_stages(x_ref, xp_ref, vec_ref, mix_ref, wr_ref, wk_ref, wv_ref,
                w1_ref, w2_ref, a1_ref, a2_ref, g1_ref, g2_ref, bc_ref, be_ref,
                r_out, lw_out, k_out, v_out, kk_out, a_out, g_out, bonus_out, *, first_block):
    g0 = vec_ref[0:1, :]
    h = _rms(x_ref[...], g0)
    prev = _rms(xp_ref[...], g0)[7:8, :]
    prev = jnp.where(first_block, 0.0, prev)
    row = lax.broadcasted_iota(jnp.int32, h.shape, 0)
    hprev = jnp.where(row == 0, prev, pltpu.roll(h, 1, 0))
    dx = hprev - h

    def mixed(j):
        return (h + dx * mix_ref[j:j + 1, :]).astype(BF16)

    yield
    r = _dot(mixed(0), wr_ref[...])
    r_out[...] = r
    yield
    k = _dot(mixed(2), wk_ref[...])
    yield
    v = _dot(mixed(3), wv_ref[...])
    v_out[...] = v
    yield
    t_w = jnp.tanh(_dot(mixed(1), w1_ref[...]))
    yield
    t_a = _dot(mixed(4), a1_ref[...])
    t_g = _sigmoid(_dot(mixed(5), g1_ref[...]))
    yield
    lw_out[...] = -math.exp(-0.5) * _sigmoid(vec_ref[1:2, :] + _dot(t_w, w2_ref[...]))
    yield
    a = _sigmoid(vec_ref[2:3, :] + _dot(t_a, a2_ref[...]))
    a_out[...] = a
    yield
    g_out[...] = _dot(t_g, g2_ref[...]).astype(g_out.dtype)
    yield
    kk = k * vec_ref[3:4, :]
    ss = _head_reduce(kk * kk, bc_ref)
    k = k * (1.0 + (a - 1.0) * vec_ref[4:5, :])
    k_out[...] = k
    bsum = _head_reduce(r * k * vec_ref[5:6, :], bc_ref)
    yield
    kk_out[...] = kk * _head_expand(1.0 / jnp.maximum(jnp.sqrt(ss), 1e-12), be_ref)
    bonus_out[...] = (_head_expand(bsum, be_ref) * v).astype(bonus_out.dtype)


def _rec_stages(r_ref, lw_ref, k_ref, v_ref, kk_ref, a_ref, y_ref, s_ref, *, nchunk, npair, first_block):
    c2 = 2 * CHUNK
    width = npair * LANES

    @pl.when(first_block)
    def _():
        s_ref[...] = jnp.zeros_like(s_ref)

    ri = lax.broadcasted_iota(jnp.int32, (c2, c2), 0)
    ci = lax.broadcasted_iota(jnp.int32, (c2, c2), 1)
    same_block = {m: (ri // m) == (ci // m) for m in (2, 16, 32, 64)}
    sibling_block = {m: ((ri // m) ^ (ci // m)) == 1 for m in (2, 4, 8, 16, 32)}
    eye = (ri == ci).astype(F32)
    gi = lax.broadcasted_iota(jnp.int32, (2 * c2, 2 * c2), 0)
    gj = lax.broadcasted_iota(jnp.int32, (2 * c2, 2 * c2), 1)
    gram_mask = (gj % CHUNK) < (gi % CHUNK) + jnp.where(gi < c2, 0, 1)
    lo_lane = (lax.broadcasted_iota(jnp.int32, (1, width), 1) % LANES) < CHUNK
    tri = (lax.broadcasted_iota(jnp.int32, (CHUNK, CHUNK), 1)
           <= lax.broadcasted_iota(jnp.int32, (CHUNK, CHUNK), 0)).astype(BF16)
    pairs = range(npair)

    def pack(m, blk):
        return m.reshape(c2 // blk, blk, c2).sum(axis=0)

    def unpack(p, blk):
        return jnp.where(same_block[blk], jnp.tile(p, (c2 // blk, 1)), 0.0)

    def per_pair(x):
        return [x[:, p * LANES:(p + 1) * LANES] for p in pairs]

    def stacked(x):
        return [jnp.concatenate([a, b], axis=0)
                for a, b in zip(per_pair(jnp.where(lo_lane, x, 0.0)), per_pair(jnp.where(lo_lane, 0.0, x)))]

    def setup(c, out):
        sl = slice(c * CHUNK, (c + 1) * CHUNK)
        lw = lw_ref[sl, :]
        l_hi = lw.astype(BF16)
        l_mid = (lw - l_hi.astype(F32)).astype(BF16)
        l_lo = (lw - l_hi.astype(F32) - l_mid.astype(F32)).astype(BF16)
        cum = _dot(tri, l_hi) + _dot(tri, l_mid) + _dot(tri, l_lo)
        tot = cum[CHUNK - 1:CHUNK, :]
        e_neg = jnp.exp(-cum)
        e_rest = jnp.exp(tot - cum)
        k = k_ref[sl, :]
        kk = kk_ref[sl, :]
        kb = kk * a_ref[sl, :]
        at = stacked(-kk * jnp.exp(cum - lw))
        rt = stacked(r_ref[sl, :] * jnp.exp(cum))
        bt = stacked(kb * e_neg)
        kt = stacked(k * e_neg)
        out["bkw"] = [jnp.concatenate([b, kq], axis=0).astype(BF16)
                      for b, kq in zip(stacked(kb * e_rest), stacked(k * e_rest))]
        out["vm"] = [v.astype(BF16) for v in stacked(v_ref[sl, :])]
        out["decay"] = per_pair(jnp.exp(tot))
        lhs = out["lhs"] = [jnp.concatenate([a, r], axis=0).astype(BF16) for a, r in zip(at, rt)]
        yield
        gram = [jnp.where(gram_mask, _dot_nt(l, jnp.concatenate([b, kq], axis=0)), 0.0)
                for l, b, kq in zip(lhs, bt, kt)]
        out["a_k"] = [g[:c2, c2:].astype(BF16) for g in gram]
        out["r_bk"] = [g[c2:].astype(BF16) for g in gram]
        abd = [g[:c2, :c2] for g in gram]
        yield
        x = [eye + jnp.where(same_block[2], a, 0.0) for a in abd]
        blk = MIN_PACK
        xp = [pack(xi, blk) for xi in x]
        for m in (2, 4, 8):
            below = [jnp.where(sibling_block[m], a, 0.0) for a in abd]
            lx = [_dot(pack(l, blk), xi) for l, xi in zip(below, x)]
            yield
            xlx = [_dot(xpi, unpack(y, blk)) for xpi, y in zip(xp, lx)]
            yield
            x = [xi + unpack(y, blk) for xi, y in zip(x, xlx)]
            xp = [xpi + y for xpi, y in zip(xp, xlx)]
        for m in (16, 32):
            lx = [_dot(jnp.where(sibling_block[m], a, 0.0), xi) for a, xi in zip(abd, x)]
            yield
            xlx = [_dot(xi, y) for xi, y in zip(x, lx)]
            yield
            x = [xi + y for xi, y in zip(x, xlx)]
        out["tinv"] = [xi.astype(BF16) for xi in x]

    def carry(c, pre):
        sl = slice(c * CHUNK, (c + 1) * CHUNK)
        vm = pre["vm"]
        hs = [_dot_nt(l, s_ref[p]) for p, l in zip(pairs, pre["lhs"])]
        yield
        w = [h[:c2] + _dot(g, v) for h, g, v in zip(hs, pre["a_k"], vm)]
        yield
        u = [_dot(ti, wi) for ti, wi in zip(pre["tinv"], w)]
        uv = [jnp.concatenate([ui.astype(BF16), v], axis=0) for ui, v in zip(u, vm)]
        yield
        ym = [h[c2:] + _dot(g, x2) for h, g, x2 in zip(hs, pre["r_bk"], uv)]
        y_ref[sl, :] = jnp.concatenate([yi[:CHUNK] + yi[CHUNK:] for yi in ym], axis=1)
        yield
        for p in pairs:
            s_ref[p] = s_ref[p] * pre["decay"][p] + _dot_tn(uv[p], pre["bkw"][p])

    pre = [dict() for _ in range(nchunk)]

    def take_turns(active):
        while active:
            for gen in list(active):
                if next(gen, StopIteration) is StopIteration:
                    active.remove(gen)
                else:
                    yield

    yield from take_turns([setup(c, pre[c]) for c in range(min(SETUP_AHEAD, nchunk))])
    for c in range(nchunk):
        ahead = c + SETUP_AHEAD
        yield from take_turns([carry(c, pre[c])] + ([setup(ahead, pre[ahead])] if ahead < nchunk else []))


def _drain(gen):
    for _ in gen:
        pass


def _rwkv_kernel(*refs, nchunk, npair, nt):
    ins, (y_out, g_out, bonus_out), (r_s, lw_s, k_s, v_s, kk_s, a_s, s_ref) = refs[:15], refs[15:18], refs[18:]
    s = pl.program_id(1)
    bufs = (r_s, lw_s, k_s, v_s, kk_s, a_s)
    pre = lambda: _pre_stages(*ins, *[b.at[s % 2] for b in bufs], g_out, bonus_out, first_block=s == 0)
    rec = lambda: _rec_stages(*[b.at[(s + 1) % 2] for b in bufs], y_out, s_ref, nchunk=nchunk, npair=npair,
                              first_block=s == 1)

    @pl.when(s == 0)
    def _():
        _drain(pre())

    @pl.when((s > 0) & (s < nt))
    def _():
        side = pre()
        for i, _ in enumerate(rec()):
            if i % PRE_STAGE_EVERY == 0:
                next(side, None)
        _drain(side)

    @pl.when(s == nt)
    def _():
        _drain(rec())


def _rwkv(x2d, vecs, mix, wr, wk, wv, w1, w2, a1, a2, g1, g2, bc, be, batch, seq, tb):
    n, d = x2d.shape
    nt = seq // tb
    npair = d // LANES
    consts = (vecs, mix, wr, wk, wv, w1, w2, a1, a2, g1, g2, bc, be)
    pre_blk = lambda b, s: b * nt + jnp.minimum(s, nt - 1)
    rec_blk = lambda b, s: b * nt + jnp.maximum(s - 1, 0)
    in_spec = pl.BlockSpec((tb, d), lambda b, s: (pre_blk(b, s), 0))
    prev_spec = pl.BlockSpec((8, d), lambda b, s: (jnp.maximum(pre_blk(b, s) * (tb // 8) - 1, 0), 0))
    return pl.pallas_call(
        functools.partial(_rwkv_kernel, nchunk=tb // CHUNK, npair=npair, nt=nt),
        grid=(batch, nt + 1),
        in_specs=[in_spec, prev_spec] + [_const_spec(c.shape) for c in consts],
        out_specs=[pl.BlockSpec((tb, d), lambda b, s: (rec_blk(b, s), 0)), in_spec, in_spec],
        out_shape=[jax.ShapeDtypeStruct((n, d), F32)] + [jax.ShapeDtypeStruct((n, d), BF16)] * 2,
        scratch_shapes=[pltpu.VMEM((2, tb, d), F32)] * 6 + [pltpu.VMEM((npair, LANES, LANES), F32)],
        compiler_params=pltpu.CompilerParams(dimension_semantics=("arbitrary", "arbitrary"),
                                             vmem_limit_bytes=VMEM_LIMIT),
        name="rwkv",
    )(x2d, x2d, *consts)


def _tail_stages(x, mixer_out, p, gn_ref, w1_ref, w2_ref, wup_ref, wgate_ref, store):
    x = x + mixer_out()
    yield
    hn = _rms(x, gn_ref[0:1, :]).astype(BF16)
    d_ff = w1_ref.shape[1]
    acc = x
    for c in range(0, d_ff, FF_CHUNK):
        mid = jnp.maximum(_dot(hn, w1_ref[:, c:c + FF_CHUNK]), 0.0)
        yield
        acc = acc + _dot(mid * mid, w2_ref[c:c + FF_CHUNK, :])
        yield
    gate = _sigmoid(_dot(_rms(acc, gn_ref[1:2, :]), wgate_ref[...]))
    yield
    store(acc + _dot(p, wup_ref[...]) * gate)


def _round_robin(gens):
    gens = list(gens)
    while gens:
        gens = [g for g in gens if next(g, StopIteration) is not StopIteration]


def _tail_attn_kernel(x_ref, o_ref, p_ref, gn_ref, wo_ref, w1_ref, w2_ref, wup_ref, wgate_ref, out_ref, *, nslab):
    rows_per = x_ref.shape[0] // nslab

    def slab(i):
        rows = slice(i * rows_per, (i + 1) * rows_per)

        def store(v):
            out_ref[rows, :] = v

        return _tail_stages(x_ref[rows, :], lambda: _dot_tn(o_ref[:, rows], wo_ref[...]), p_ref[rows, :],
                            gn_ref, w1_ref, w2_ref, wup_ref, wgate_ref, store)

    _round_robin(slab(i) for i in range(nslab))


def _tail_rwkv_kernel(x_ref, y_ref, bonus_ref, g_ref, p_ref, vec_ref, bc_ref, be_ref, gnq_ref, wq_ref, wk_ref,
                      wv_ref, gn_ref, wo_ref, w1_ref, w2_ref, wup_ref, wgate_ref, out_ref, q_out, k_out, v_out,
                      *, nslab):
    rows_per = x_ref.shape[0] // nslab
    assert v_out.shape == (nslab, x_ref.shape[1], rows_per)
    inv_n = 1.0 / RWKV_HEAD

    def slab(i):
        rows = slice(i * rows_per, (i + 1) * rows_per)
        y = y_ref[rows, :]
        mean = _head_reduce(y, bc_ref) * inv_n
        yield
        yc = y - _head_expand(mean, be_ref)
        yield
        var = _head_reduce(yc * yc, bc_ref) * inv_n
        yield
        yn = yc * _head_expand(lax.rsqrt(var + GN_EPS), be_ref) * vec_ref[6:7, :] + vec_ref[7:8, :]
        yield
        fresh = []

        def store(v):
            out_ref[rows, :] = v
            fresh.append(v)

        gated = (yn + bonus_ref[rows, :]) * g_ref[rows, :]
        yield from _tail_stages(x_ref[rows, :], lambda: _dot(gated, wo_ref[...]), p_ref[rows, :],
                                gn_ref, w1_ref, w2_ref, wup_ref, wgate_ref, store)
        yield
        yield from _qkv_stages(fresh[0], gnq_ref, wq_ref, wk_ref, wv_ref, bc_ref, be_ref, q_out, k_out, v_out,
                               rows, i)

    _round_robin(slab(i) for i in range(nslab))


def _tail(kernel_fn, rows, cols, p3d, layer, extra_consts, gn, wo, w1, w2, wup, wgate, tm, key_block=None):
    n, d = rows[0].shape
    row_specs = [_row_spec(tm, d)] * len(rows) + [pl.BlockSpec((d, tm), lambda i: (0, i))] * len(cols)
    consts = tuple(extra_consts) + (gn, wo, w1, w2, wup, wgate)
    out_specs = [_row_spec(tm, d)]
    out_shape = [jax.ShapeDtypeStruct((n, d), F32)]
    if key_block:
        out_specs += [_row_spec(tm, d)] * 2 + [pl.BlockSpec((tm // key_block, d, key_block), lambda i: (i, 0, 0))]
        out_shape += [jax.ShapeDtypeStruct((n, d), BF16)] * 2
        out_shape += [jax.ShapeDtypeStruct((n // key_block, d, key_block), BF16)]
    outs = pl.pallas_call(
        kernel_fn,
        grid=(n // tm,),
        in_specs=row_specs
                 + [pl.BlockSpec((None, tm, p3d.shape[2]), lambda i: (layer, i, 0))]
                 + [_const_spec(c.shape) for c in consts],
        out_specs=out_specs,
        out_shape=out_shape,
        compiler_params=pltpu.CompilerParams(dimension_semantics=("arbitrary",),
                                             vmem_limit_bytes=VMEM_LIMIT),
        name="tail",
    )(*rows, *cols, p3d, *consts)
    return outs if key_block else outs[0]


def _qkv_stages(x, gn_ref, wq_ref, wk_ref, wv_ref, bc_ref, be_ref, q_out, k_out, v_out, rows, slab):
    inv_n = 1.0 / DIFF_HEAD
    hk = _rms(x, gn_ref[1:2, :]).astype(BF16)
    tq = _dot(_rms(x, gn_ref[0:1, :]), wq_ref[...])
    yield
    ms = _head_reduce(tq * tq, bc_ref) * inv_n
    yield
    q = tq * _head_expand(lax.rsqrt(ms + NORM_EPS), be_ref) * gn_ref[2:3, :]
    q_out[rows, :] = q.astype(q_out.dtype)
    yield
    tk = _dot(hk, wk_ref[...])
    yield
    ms = _head_reduce(tk * tk, bc_ref) * inv_n
    yield
    k_out[rows, :] = (tk * _head_expand(lax.rsqrt(ms + NORM_EPS), be_ref) * gn_ref[3:4, :]).astype(k_out.dtype)
    yield
    v_out[slab] = _dot(hk, wv_ref[...]).T.astype(v_out.dtype)


def _bucket_tiles(tb):
    i = np.arange(tb, dtype=np.int64)[None, :]
    j = np.arange(tb, dtype=np.int64)[:, None]

    def bucket(rel):
        n = np.maximum(rel, 0)
        max_exact = N_BUCKETS // 2
        nf = np.maximum(n, 1).astype(np.float32)
        large = max_exact + (np.log(nf / np.float32(max_exact)) / np.float32(math.log(MAX_DIST / max_exact))
                             * np.float32(N_BUCKETS - max_exact)).astype(np.int32)
        large = np.minimum(large, N_BUCKETS - 1)
        return np.where(n < max_exact, n, large).astype(np.int32)

    diag = np.where(i - j >= 0, bucket(i - j), -1)
    near = bucket(tb + i - j)
    tiles = np.stack([diag, near]).astype(np.int32)
    return np.concatenate([tiles, tiles], axis=2)


def _bias_kernel(tab_ref, bucket_ref, out_ref):
    h = pl.program_id(0)
    b = bucket_ref[...]
    far = tab_ref[N_BUCKETS - 1, h]
    acc = jnp.where(b < 0, NEG_BIG, 0.0)
    for n in range(N_BUCKETS - 1):
        acc = jnp.where(b == n, (tab_ref[n, h] - far) * LOG2E, acc)
    out_ref[0] = acc


def _bias_tiles(rel_bias, tb):
    buckets = jnp.asarray(_bucket_tiles(tb))
    nh = rel_bias.shape[1]
    return pl.pallas_call(
        _bias_kernel,
        grid=(nh,),
        in_specs=[pl.BlockSpec(memory_space=pltpu.SMEM),
                  pl.BlockSpec((2, tb, 2 * tb), lambda h: (0, 0, 0))],
        out_specs=pl.BlockSpec((1, 2, tb, 2 * tb), lambda h: (h, 0, 0, 0)),
        out_shape=jax.ShapeDtypeStruct((nh, 2, tb, 2 * tb), F32),
        compiler_params=pltpu.CompilerParams(dimension_semantics=("arbitrary",)),
        name="bias",
    )(rel_bias, buckets)


def _attn_kernel(q_ref, k_ref, vt_ref, bias_ref, lam_ref, g_ref, o_ref, m_ref, l_ref, acc_ref,
                 *, tb, nhead, out_scale, lambda_init, bounded):
    qi = pl.program_id(2)
    heads = range(nhead)
    lane = lax.broadcasted_iota(jnp.int32, (1, LANES), 1)

    def head_lanes(h):
        return slice(h * LANES, (h + 1) * LANES)

    def stacked_q(h):
        q = q_ref[:, head_lanes(h)]
        zero = jnp.zeros_like(q)
        return jnp.concatenate([jnp.where(lane < DIFF_HEAD, q, zero),
                                jnp.where(lane < DIFF_HEAD, zero, q)], axis=0)

    qs = [stacked_q(h) for h in heads]
    m_ref[...] = jnp.full_like(m_ref, NEG_BIG)
    l_ref[...] = jnp.zeros_like(l_ref)
    acc_ref[...] = jnp.zeros_like(acc_ref)

    def advance(j, biases):
        nblk = len(biases)
        rows = pl.ds(pl.multiple_of(j * tb, tb), nblk * tb)
        logits = lambda h: _dot_nt(k_ref[rows, head_lanes(h)], qs[h])
        ahead = min(3, nhead)
        st = {h: logits(h) for h in range(ahead)}
        spans = []
        for i, b in enumerate(biases):
            if b is None and spans and spans[-1][2] is None:
                spans[-1] = (spans[-1][0], i + 1, None)
            else:
                spans.append((i, i + 1, b))
        for h in heads:
            if h + ahead < nhead:
                st[h + ahead] = logits(h + ahead)
            s = st.pop(h)
            parts = [s[i * tb:e * tb] if b is None else s[i * tb:e * tb] + bias_ref[h, b] for i, e, b in spans]
            if bounded:
                ps = [jnp.exp2(x) for x in parts]
                l_new = l_ref[h]
            else:
                m_old = m_ref[h]
                m_new = m_old
                for x in parts:
                    m_new = jnp.maximum(m_new, jnp.max(x, axis=0, keepdims=True))
                alpha = jnp.exp2(m_old - m_new)
                ps = [jnp.exp2(x - m_new) for x in parts]
                l_new = alpha * l_ref[h]
                m_ref[h] = m_new
            for p in ps:
                l_new = l_new + jnp.sum(p, axis=0, keepdims=True)
            l_ref[h] = l_new
            pv = None
            for (i, e, _), p in zip(spans, ps):
                p = p.astype(BF16)
                for b in range(i, e):
                    term = _dot(vt_ref[j + b, head_lanes(h), :], p[(b - i) * tb:(b - i + 1) * tb])
                    pv = term if pv is None else pv + term
            acc_ref[h] = acc_ref[h] + pv if bounded else alpha * acc_ref[h] + pv

    def far_blocks(t, carry):
        for i in range(0, FAR_BLOCKS, 2):
            advance(FAR_BLOCKS * t + i, (None, None))
        return carry

    nfar = jnp.maximum(qi - 1, 0)
    lax.fori_loop(0, nfar // FAR_BLOCKS, far_blocks, 0)

    @pl.when(qi == 0)
    def _():
        advance(0, (0,))

    for left in range(FAR_BLOCKS):
        @pl.when((qi >= 1) & (nfar % FAR_BLOCKS == left))
        def _(left=left):
            for i in range(0, left - 1, 2):
                advance(qi - 1 - left + i, (None, None))
            advance(qi - 1 - left % 2, (None,) * (left % 2) + (1, 0))

    lam = lam_ref[...]
    lam_full = (jnp.exp(jnp.sum(lam[0:1] * lam[1:2], axis=1, keepdims=True))
                - jnp.exp(jnp.sum(lam[2:3] * lam[3:4], axis=1, keepdims=True)) + lambda_init)
    for h in heads:
        o = acc_ref[h] * (1.0 / l_ref[h])
        o = o[:, :tb] - lam_full * o[:, tb:]
        o = o * lax.rsqrt(jnp.mean(o * o, axis=0, keepdims=True) + NORM_EPS)
        o_ref[head_lanes(h), :] = (o * (g_ref[...] * out_scale)).astype(o_ref.dtype)


def _attn(q, k, vt, bias, lam, subln_g, batch, seq, tb, nhead, layer_idx, bounded):
    n, d = q.shape
    ngroup = d // (LANES * nhead)
    width = LANES * nhead
    nq = seq // tb
    lambda_init = 0.8 - 0.6 * math.exp(-0.3 * layer_idx)
    q_spec = pl.BlockSpec((tb, width), lambda b, h, i: (b * nq + i, h))
    return pl.pallas_call(
        functools.partial(_attn_kernel, tb=tb, nhead=nhead, out_scale=1.0 - lambda_init,
                          lambda_init=lambda_init, bounded=bounded),
        grid=(batch, ngroup, nq),
        in_specs=[q_spec,
                  pl.BlockSpec((seq, width), lambda b, h, i: (b, h)),
                  pl.BlockSpec((nq, width, tb), lambda b, h, i: (b, h, 0)),
                  pl.BlockSpec((nhead, 2, tb, 2 * tb), lambda b, h, i: (h, 0, 0, 0)),
                  pl.BlockSpec(lam.shape, lambda b, h, i: (0, 0)),
                  pl.BlockSpec(subln_g.shape, lambda b, h, i: (0, 0))],
        out_specs=pl.BlockSpec((width, tb), lambda b, h, i: (h, b * nq + i)),
        out_shape=jax.ShapeDtypeStruct((d, n), BF16),
        scratch_shapes=[pltpu.VMEM((nhead, 1, 2 * tb), F32), pltpu.VMEM((nhead, 1, 2 * tb), F32),
                        pltpu.VMEM((nhead, LANES, 2 * tb), F32)],
        compiler_params=pltpu.CompilerParams(
            dimension_semantics=("arbitrary", "arbitrary", "arbitrary"),
            vmem_limit_bytes=VMEM_LIMIT),
        name="attn",
    )(q, k, vt, bias, lam, subln_g)


def kernel(x, p, norm_g, mlp_w1, mlp_w2, ple_w_up, ple_w_gate, rwkv_mix, rwkv_w_rkvo, rwkv_w0, rwkv_w1, rwkv_w2, rwkv_a0, rwkv_a1, rwkv_a2, rwkv_g1, rwkv_g2, rwkv_k_k, rwkv_k_a, rwkv_r_k, rwkv_ln_w, rwkv_ln_b, kv_norm_g, w_k_shared, w_v_shared, k_norm_g, diff_w_q, diff_q_norm_g, diff_lam, diff_subln_g, diff_w_o, rel_bias):
    batch, seq, d = x.shape
    assert d == D_MODEL and norm_g.shape[0] == 2
    n = batch * seq
    tm = min(256, seq)
    tb_rec = min(256, seq)
    tb_attn = min(256, seq)
    assert seq % tm == 0 and seq % tb_rec == 0 and seq % tb_attn == 0 and tb_attn >= MAX_DIST

    bf = lambda w: w.astype(BF16)
    x2d = x.reshape(n, d)
    p3d = p.reshape(p.shape[0], n, p.shape[-1])
    head_id = jnp.arange(d, dtype=jnp.int32) // RWKV_HEAD
    slot = jnp.arange(LANES, dtype=jnp.int32)
    bc = ((head_id[:, None] == slot[None, :] % N_HEADS) & (slot[None, :] < 2 * N_HEADS)).astype(BF16)
    be = bc.T

    vecs = jnp.stack([norm_g[0, 0], rwkv_w0[0], rwkv_a0[0], rwkv_k_k[0], rwkv_k_a[0],
                      rwkv_r_k[0].reshape(d), rwkv_ln_w[0], rwkv_ln_b[0]])
    y, g, bonus = _rwkv(
        x2d, vecs, rwkv_mix[0], bf(rwkv_w_rkvo[0, 0]), bf(rwkv_w_rkvo[0, 1]), bf(rwkv_w_rkvo[0, 2]),
        bf(rwkv_w1[0]), bf(rwkv_w2[0]), bf(rwkv_a1[0]), bf(rwkv_a2[0]), bf(rwkv_g1[0]), bf(rwkv_g2[0]),
        bc, be, batch, seq, tb_rec)
    reps = d // DIFF_HEAD
    gnq = jnp.stack([norm_g[1, 0], kv_norm_g,
                     jnp.tile(diff_q_norm_g[0], reps) * (DIFF_HEAD ** -0.5 * LOG2E), jnp.tile(k_norm_g, reps)])
    x2d, q, kq, vq = _tail(functools.partial(_tail_rwkv_kernel, nslab=2), (x2d, y, bonus, g), (), p3d, 0,
                           (vecs, bc, be, gnq, bf(diff_w_q[0]), bf(w_k_shared), bf(w_v_shared)), norm_g[0, 1:3],
                           bf(rwkv_w_rkvo[0, 3]), bf(mlp_w1[0]), bf(mlp_w2[0]), bf(ple_w_up[0]),
                           bf(ple_w_gate[0]), 2 * tb_attn, key_block=tb_attn)

    bias = _bias_tiles(rel_bias, tb_attn)
    logit_bound = 1.02 * LOG2E * (DIFF_HEAD ** 0.5 * jnp.max(jnp.abs(diff_q_norm_g[0] * k_norm_g))
                                  + jnp.max(jnp.abs(rel_bias - rel_bias[-1:])))
    attn = functools.partial(_attn, q, kq, vq, bias, diff_lam[0], diff_subln_g[0].reshape(LANES, 1),
                             batch, seq, tb_attn, 4, 1)
    o = lax.cond(logit_bound <= LOGIT_BOUND, lambda: attn(True), lambda: attn(False))
    x2d = _tail(functools.partial(_tail_attn_kernel, nslab=2), (x2d,), (o,), p3d, 1, (), norm_g[1, 1:3],
                bf(diff_w_o[0]), bf(mlp_w1[1]), bf(mlp_w2[1]), bf(ple_w_up[1]), bf(ple_w_gate[1]), 2 * tm)
    return x2d.reshape(batch, seq, d)
```

```python
import functools
import math

import numpy as np
import jax
import jax.numpy as jnp
from jax import lax
from jax.experimental import pallas as pl
from jax.experimental.pallas import tpu as pltpu

F32 = jnp.float32
BF16 = jnp.bfloat16

D_MODEL = 1024
RWKV_HEAD = 64
N_HEADS = D_MODEL // RWKV_HEAD
DIFF_HEAD = 64
DIFF_HEADS = 8
LANES = 128
CHUNK = 64
MIN_PACK = 16
SETUP_AHEAD = 4
REC_GROUPS = 2
PRE_STAGE_EVERY = 10
FF_CHUNK = 1024
NORM_EPS = 1e-6
GN_EPS = 64e-5
N_BUCKETS = 32
MAX_DIST = 128
NEG_BIG = -1e30
LOG2E = math.log2(math.e)
FAR_BLOCKS = 4
LOGIT_BOUND = 60.0
VMEM_LIMIT = 60 * 1024 * 1024


def _dot(a, b):
    return jnp.dot(a.astype(BF16), b.astype(BF16), preferred_element_type=F32)


def _dot_nt(a, b):
    return lax.dot_general(a.astype(BF16), b.astype(BF16), (((1,), (1,)), ((), ())),
                           preferred_element_type=F32)


def _dot_tn(a, b):
    return lax.dot_general(a.astype(BF16), b.astype(BF16), (((0,), (0,)), ((), ())),
                           preferred_element_type=F32)


def _rms(x, g):
    return x * lax.rsqrt(jnp.mean(x * x, axis=-1, keepdims=True) + NORM_EPS) * g


def _sigmoid(x):
    return 1.0 / (1.0 + jnp.exp(-x))


def _head_reduce(x, bc_ref):
    return _dot(x, bc_ref[...])


def _head_expand(c, be_ref):
    lane = lax.broadcasted_iota(jnp.int32, (1, LANES), 1)
    hi = c.astype(BF16).astype(F32)
    return _dot(jnp.where(lane < N_HEADS, hi, c - hi), be_ref[...])


def _const_spec(shape):
    nd = len(shape)
    return pl.BlockSpec(shape, lambda *_: (0,) * nd, pipeline_mode=pl.Buffered(1))


def _row_spec(tm, width):
    return pl.BlockSpec((tm, width), lambda i: (i, 0))


def _pre_stages(x_ref, xp_ref, vec_ref, mix_ref, wr_ref, wk_ref, wv_ref,
                w1_ref, w2_ref, a1_ref, a2_ref, g1_ref, g2_ref, bc_ref, be_ref,
                r_out, lw_out, k_out, v_out, kk_out, a_out, g_out, bonus_out, *, first_block):
    g0 = vec_ref[0:1, :]
    h = _rms(x_ref[...], g0)
    prev = _rms(xp_ref[...], g0)[7:8, :]
    prev = jnp.where(first_block, 0.0, prev)
    row = lax.broadcasted_iota(jnp.int32, h.shape, 0)
    hprev = jnp.where(row == 0, prev, pltpu.roll(h, 1, 0))
    dx = hprev - h

    def mixed(j):
        return (h + dx * mix_ref[j:j + 1, :]).astype(BF16)

    yield
    r = _dot(mixed(0), wr_ref[...])
    r_out[...] = r
    yield
    k = _dot(mixed(2), wk_ref[...])
    yield
    v = _dot(mixed(3), wv_ref[...])
    v_out[...] = v
    yield
    t_w = jnp.tanh(_dot(mixed(1), w1_ref[...]))
    yield
    t_a = _dot(mixed(4), a1_ref[...])
    t_g = _sigmoid(_dot(mixed(5), g1_ref[...]))
    yield
    lw_out[...] = -math.exp(-0.5) * _sigmoid(vec_ref[1:2, :] + _dot(t_w, w2_ref[...]))
    yield
    a = _sigmoid(vec_ref[2:3, :] + _dot(t_a, a2_ref[...]))
    a_out[...] = a
    yield
    g_out[...] = _dot(t_g, g2_ref[...]).astype(g_out.dtype)
    yield
    kk = k * vec_ref[3:4, :]
    ss = _head_reduce(kk * kk, bc_ref)
    k = k * (1.0 + (a - 1.0) * vec_ref[4:5, :])
    k_out[...] = k
    bsum = _head_reduce(r * k * vec_ref[5:6, :], bc_ref)
    yield
    kk_out[...] = kk * _head_expand(1.0 / jnp.maximum(jnp.sqrt(ss), 1e-12), be_ref)
    bonus_out[...] = (_head_expand(bsum, be_ref) * v).astype(bonus_out.dtype)


def _rec_stages(r_ref, lw_ref, k_ref, v_ref, kk_ref, a_ref, y_ref, s_ref, *, nchunk, npair, first_block):
    c2 = 2 * CHUNK
    width = npair * LANES

    @pl.when(first_block)
    def _():
        s_ref[...] = jnp.zeros_like(s_ref)

    ri = lax.broadcasted_iota(jnp.int32, (c2, c2), 0)
    ci = lax.broadcasted_iota(jnp.int32, (c2, c2), 1)
    same_block = {m: (ri // m) == (ci // m) for m in (2, 16, 32, 64)}
    sibling_block = {m: ((ri // m) ^ (ci // m)) == 1 for m in (2, 4, 8, 16, 32)}
    eye = (ri == ci).astype(F32)
    gi = lax.broadcasted_iota(jnp.int32, (2 * c2, 2 * c2), 0)
    gj = lax.broadcasted_iota(jnp.int32, (2 * c2, 2 * c2), 1)
    gram_mask = (gj % CHUNK) < (gi % CHUNK) + jnp.where(gi < c2, 0, 1)
    lo_lane = (lax.broadcasted_iota(jnp.int32, (1, width), 1) % LANES) < CHUNK
    tri = (lax.broadcasted_iota(jnp.int32, (CHUNK, CHUNK), 1)
           <= lax.broadcasted_iota(jnp.int32, (CHUNK, CHUNK), 0)).astype(BF16)
    pairs = range(npair)

    def pack(m, blk):
        return m.reshape(c2 // blk, blk, c2).sum(axis=0)

    def unpack(p, blk):
        return jnp.where(same_block[blk], jnp.tile(p, (c2 // blk, 1)), 0.0)

    def per_pair(x):
        return [x[:, p * LANES:(p + 1) * LANES] for p in pairs]

    def stacked(x):
        return [jnp.concatenate([a, b], axis=0)
                for a, b in zip(per_pair(jnp.where(lo_lane, x, 0.0)), per_pair(jnp.where(lo_lane, 0.0, x)))]

    def setup(c, out):
        sl = slice(c * CHUNK, (c + 1) * CHUNK)
        lw = lw_ref[sl, :]
        l_hi = lw.astype(BF16)
        l_mid = (lw - l_hi.astype(F32)).astype(BF16)
        l_lo = (lw - l_hi.astype(F32) - l_mid.astype(F32)).astype(BF16)
        cum = _dot(tri, l_hi) + _dot(tri, l_mid) + _dot(tri, l_lo)
        tot = cum[CHUNK - 1:CHUNK, :]
        e_neg = jnp.exp(-cum)
        e_rest = jnp.exp(tot - cum)
        k = k_ref[sl, :]
        kk = kk_ref[sl, :]
        kb = kk * a_ref[sl, :]
        at = stacked(-kk * jnp.exp(cum - lw))
        rt = stacked(r_ref[sl, :] * jnp.exp(cum))
        bt = stacked(kb * e_neg)
        kt = stacked(k * e_neg)
        out["bkw"] = [jnp.concatenate([b, kq], axis=0).astype(BF16)
                      for b, kq in zip(stacked(kb * e_rest), stacked(k * e_rest))]
        out["vm"] = [v.astype(BF16) for v in stacked(v_ref[sl, :])]
        out["decay"] = per_pair(jnp.exp(tot))
        lhs = out["lhs"] = [jnp.concatenate([a, r], axis=0).astype(BF16) for a, r in zip(at, rt)]
        yield
        gram = [jnp.where(gram_mask, _dot_nt(l, jnp.concatenate([b, kq], axis=0)), 0.0)
                for l, b, kq in zip(lhs, bt, kt)]
        out["a_k"] = [g[:c2, c2:].astype(BF16) for g in gram]
        out["r_bk"] = [g[c2:].astype(BF16) for g in gram]
        abd = [g[:c2, :c2] for g in gram]
        yield
        x = [eye + jnp.where(same_block[2], a, 0.0) for a in abd]
        blk = MIN_PACK
        xp = [pack(xi, blk) for xi in x]
        for m in (2, 4, 8):
            below = [jnp.where(sibling_block[m], a, 0.0) for a in abd]
            lx = [_dot(pack(l, blk), xi) for l, xi in zip(below, x)]
            yield
            xlx = [_dot(xpi, unpack(y, blk)) for xpi, y in zip(xp, lx)]
            yield
            x = [xi + unpack(y, blk) for xi, y in zip(x, xlx)]
            xp = [xpi + y for xpi, y in zip(xp, xlx)]
        for m in (16, 32):
            lx = [_dot(jnp.where(sibling_block[m], a, 0.0), xi) for a, xi in zip(abd, x)]
            yield
            xlx = [_dot(xi, y) for xi, y in zip(x, lx)]
            yield
            x = [xi + y for xi, y in zip(x, xlx)]
        out["tinv"] = [xi.astype(BF16) for xi in x]

    def carry(c, pre):
        sl = slice(c * CHUNK, (c + 1) * CHUNK)
        vm = pre["vm"]
        hs = [_dot_nt(l, s_ref[p]) for p, l in zip(pairs, pre["lhs"])]
        yield
        w = [h[:c2] + _dot(g, v) for h, g, v in zip(hs, pre["a_k"], vm)]
        yield
        u = [_dot(ti, wi) for ti, wi in zip(pre["tinv"], w)]
        uv = [jnp.concatenate([ui.astype(BF16), v], axis=0) for ui, v in zip(u, vm)]
        yield
        ym = [h[c2:] + _dot(g, x2) for h, g, x2 in zip(hs, pre["r_bk"], uv)]
        y_ref[sl, :] = jnp.concatenate([yi[:CHUNK] + yi[CHUNK:] for yi in ym], axis=1)
        yield
        for p in pairs:
            s_ref[p] = s_ref[p] * pre["decay"][p] + _dot_tn(uv[p], pre["bkw"][p])

    pre = [dict() for _ in range(nchunk)]

    def take_turns(active):
        while active:
            for gen in list(active):
                if next(gen, StopIteration) is StopIteration:
                    active.remove(gen)
                else:
                    yield

    yield from take_turns([setup(c, pre[c]) for c in range(min(SETUP_AHEAD, nchunk))])
    for c in range(nchunk):
        ahead = c + SETUP_AHEAD
        yield from take_turns([carry(c, pre[c])] + ([setup(ahead, pre[ahead])] if ahead < nchunk else []))


def _drain(gen):
    for _ in gen:
        pass


def _rwkv_kernel(*refs, nchunk, npair, nt):
    ins, (y_out, g_out, bonus_out), (r_s, lw_s, k_s, v_s, kk_s, a_s, s_ref) = refs[:15], refs[15:18], refs[18:]
    s = pl.program_id(1)
    bufs = (r_s, lw_s, k_s, v_s, kk_s, a_s)
    pre = lambda: _pre_stages(*ins, *[b.at[s % 2] for b in bufs], g_out, bonus_out, first_block=s == 0)
    gp = npair // REC_GROUPS

    def rec():
        groups = []
        for g in range(REC_GROUPS):
            lanes = pl.ds(g * gp * LANES, gp * LANES)
            groups.append(_rec_stages(*[b.at[(s + 1) % 2, :, lanes] for b in bufs], y_out.at[:, lanes],
                                      s_ref.at[pl.ds(g * gp, gp)], nchunk=nchunk, npair=gp, first_block=s == 1))
        while groups:
            for gen in list(groups):
                if next(gen, StopIteration) is StopIteration:
                    groups.remove(gen)
                else:
                    yield

    @pl.when(s == 0)
    def _():
        _drain(pre())

    @pl.when((s > 0) & (s < nt))
    def _():
        side = pre()
        for i, _ in enumerate(rec()):
            if i % PRE_STAGE_EVERY == 0:
                next(side, None)
        _drain(side)

    @pl.when(s == nt)
    def _():
        _drain(rec())


def _rwkv(x2d, vecs, mix, wr, wk, wv, w1, w2, a1, a2, g1, g2, bc, be, batch, seq, tb):
    n, d = x2d.shape
    nt = seq // tb
    npair = d // LANES
    consts = (vecs, mix, wr, wk, wv, w1, w2, a1, a2, g1, g2, bc, be)
    pre_blk = lambda b, s: b * nt + jnp.minimum(s, nt - 1)
    rec_blk = lambda b, s: b * nt + jnp.maximum(s - 1, 0)
    in_spec = pl.BlockSpec((tb, d), lambda b, s: (pre_blk(b, s), 0))
    prev_spec = pl.BlockSpec((8, d), lambda b, s: (jnp.maximum(pre_blk(b, s) * (tb // 8) - 1, 0), 0))
    return pl.pallas_call(
        functools.partial(_rwkv_kernel, nchunk=tb // CHUNK, npair=npair, nt=nt),
        grid=(batch, nt + 1),
        in_specs=[in_spec, prev_spec] + [_const_spec(c.shape) for c in consts],
        out_specs=[pl.BlockSpec((tb, d), lambda b, s: (rec_blk(b, s), 0)), in_spec, in_spec],
        out_shape=[jax.ShapeDtypeStruct((n, d), F32)] + [jax.ShapeDtypeStruct((n, d), BF16)] * 2,
        scratch_shapes=[pltpu.VMEM((2, tb, d), F32)] * 6 + [pltpu.VMEM((npair, LANES, LANES), F32)],
        compiler_params=pltpu.CompilerParams(dimension_semantics=("arbitrary", "arbitrary"),
                                             vmem_limit_bytes=VMEM_LIMIT),
        name="rwkv",
    )(x2d, x2d, *consts)


def _tail_stages(x, mixer_out, p, gn_ref, w1_ref, w2_ref, wup_ref, wgate_ref, store):
    x = x + mixer_out()
    yield
    hn = _rms(x, gn_ref[0:1, :]).astype(BF16)
    d_ff = w1_ref.shape[1]
    acc = x
    for c in range(0, d_ff, FF_CHUNK):
        mid = jnp.maximum(_dot(hn, w1_ref[:, c:c + FF_CHUNK]), 0.0)
        yield
        acc = acc + _dot(mid * mid, w2_ref[c:c + FF_CHUNK, :])
        yield
    gate = _sigmoid(_dot(_rms(acc, gn_ref[1:2, :]), wgate_ref[...]))
    yield
    store(acc + _dot(p, wup_ref[...]) * gate)


def _round_robin(gens):
    gens = list(gens)
    while gens:
        gens = [g for g in gens if next(g, StopIteration) is not StopIteration]


def _tail_attn_kernel(x_ref, o_ref, p_ref, gn_ref, wo_ref, w1_ref, w2_ref, wup_ref, wgate_ref, out_ref, *, nslab):
    rows_per = x_ref.shape[0] // nslab

    def slab(i):
        rows = slice(i * rows_per, (i + 1) * rows_per)

        def store(v):
            out_ref[rows, :] = v

        return _tail_stages(x_ref[rows, :], lambda: _dot_tn(o_ref[:, rows], wo_ref[...]), p_ref[rows, :],
                            gn_ref, w1_ref, w2_ref, wup_ref, wgate_ref, store)

    _round_robin(slab(i) for i in range(nslab))


def _tail_rwkv_kernel(x_ref, y_ref, bonus_ref, g_ref, p_ref, vec_ref, bc_ref, be_ref, gnq_ref, wq_ref, wk_ref,
                      wv_ref, gn_ref, wo_ref, w1_ref, w2_ref, wup_ref, wgate_ref, out_ref, q_out, k_out, v_out,
                      *, nslab):
    rows_per = x_ref.shape[0] // nslab
    assert v_out.shape == (nslab, x_ref.shape[1], rows_per)
    inv_n = 1.0 / RWKV_HEAD

    def slab(i):
        rows = slice(i * rows_per, (i + 1) * rows_per)
        y = y_ref[rows, :]
        mean = _head_reduce(y, bc_ref) * inv_n
        yield
        yc = y - _head_expand(mean, be_ref)
        yield
        var = _head_reduce(yc * yc, bc_ref) * inv_n
        yield
        yn = yc * _head_expand(lax.rsqrt(var + GN_EPS), be_ref) * vec_ref[6:7, :] + vec_ref[7:8, :]
        yield
        fresh = []

        def store(v):
            out_ref[rows, :] = v
            fresh.append(v)

        gated = (yn + bonus_ref[rows, :]) * g_ref[rows, :]
        yield from _tail_stages(x_ref[rows, :], lambda: _dot(gated, wo_ref[...]), p_ref[rows, :],
                                gn_ref, w1_ref, w2_ref, wup_ref, wgate_ref, store)
        yield
        yield from _qkv_stages(fresh[0], gnq_ref, wq_ref, wk_ref, wv_ref, bc_ref, be_ref, q_out, k_out, v_out,
                               rows, i)

    _round_robin(slab(i) for i in range(nslab))


def _tail(kernel_fn, rows, cols, p3d, layer, extra_consts, gn, wo, w1, w2, wup, wgate, tm, key_block=None):
    n, d = rows[0].shape
    row_specs = [_row_spec(tm, d)] * len(rows) + [pl.BlockSpec((d, tm), lambda i: (0, i))] * len(cols)
    consts = tuple(extra_consts) + (gn, wo, w1, w2, wup, wgate)
    out_specs = [_row_spec(tm, d)]
    out_shape = [jax.ShapeDtypeStruct((n, d), F32)]
    if key_block:
        out_specs += [_row_spec(tm, d)] * 2 + [pl.BlockSpec((tm // key_block, d, key_block), lambda i: (i, 0, 0))]
        out_shape += [jax.ShapeDtypeStruct((n, d), BF16)] * 2
        out_shape += [jax.ShapeDtypeStruct((n // key_block, d, key_block), BF16)]
    outs = pl.pallas_call(
        kernel_fn,
        grid=(n // tm,),
        in_specs=row_specs
                 + [pl.BlockSpec((None, tm, p3d.shape[2]), lambda i: (layer, i, 0))]
                 + [_const_spec(c.shape) for c in consts],
        out_specs=out_specs,
        out_shape=out_shape,
        compiler_params=pltpu.CompilerParams(dimension_semantics=("arbitrary",),
                                             vmem_limit_bytes=VMEM_LIMIT),
        name="tail",
    )(*rows, *cols, p3d, *consts)
    return outs if key_block else outs[0]


def _qkv_stages(x, gn_ref, wq_ref, wk_ref, wv_ref, bc_ref, be_ref, q_out, k_out, v_out, rows, slab):
    inv_n = 1.0 / DIFF_HEAD
    hk = _rms(x, gn_ref[1:2, :]).astype(BF16)
    tq = _dot(_rms(x, gn_ref[0:1, :]), wq_ref[...])
    yield
    ms = _head_reduce(tq * tq, bc_ref) * inv_n
    yield
    q = tq * _head_expand(lax.rsqrt(ms + NORM_EPS), be_ref) * gn_ref[2:3, :]
    q_out[rows, :] = q.astype(q_out.dtype)
    yield
    tk = _dot(hk, wk_ref[...])
    yield
    ms = _head_reduce(tk * tk, bc_ref) * inv_n
    yield
    k_out[rows, :] = (tk * _head_expand(lax.rsqrt(ms + NORM_EPS), be_ref) * gn_ref[3:4, :]).astype(k_out.dtype)
    yield
    v_out[slab] = _dot(hk, wv_ref[...]).T.astype(v_out.dtype)


def _bucket_tiles(tb):
    i = np.arange(tb, dtype=np.int64)[None, :]
    j = np.arange(tb, dtype=np.int64)[:, None]

    def bucket(rel):
        n = np.maximum(rel, 0)
        max_exact = N_BUCKETS // 2
        nf = np.maximum(n, 1).astype(np.float32)
        large = max_exact + (np.log(nf / np.float32(max_exact)) / np.float32(math.log(MAX_DIST / max_exact))
                             * np.float32(N_BUCKETS - max_exact)).astype(np.int32)
        large = np.minimum(large, N_BUCKETS - 1)
        return np.where(n < max_exact, n, large).astype(np.int32)

    diag = np.where(i - j >= 0, bucket(i - j), -1)
    near = bucket(tb + i - j)
    tiles = np.stack([diag, near]).astype(np.int32)
    return np.concatenate([tiles, tiles], axis=2)


def _bias_kernel(tab_ref, bucket_ref, out_ref):
    h = pl.program_id(0)
    b = bucket_ref[...]
    far = tab_ref[N_BUCKETS - 1, h]
    acc = jnp.where(b < 0, NEG_BIG, 0.0)
    for n in range(N_BUCKETS - 1):
        acc = jnp.where(b == n, (tab_ref[n, h] - far) * LOG2E, acc)
    out_ref[0] = acc


def _bias_tiles(rel_bias, tb):
    buckets = jnp.asarray(_bucket_tiles(tb))
    nh = rel_bias.shape[1]
    return pl.pallas_call(
        _bias_kernel,
        grid=(nh,),
        in_specs=[pl.BlockSpec(memory_space=pltpu.SMEM),
                  pl.BlockSpec((2, tb, 2 * tb), lambda h: (0, 0, 0))],
        out_specs=pl.BlockSpec((1, 2, tb, 2 * tb), lambda h: (h, 0, 0, 0)),
        out_shape=jax.ShapeDtypeStruct((nh, 2, tb, 2 * tb), F32),
        compiler_params=pltpu.CompilerParams(dimension_semantics=("arbitrary",)),
        name="bias",
    )(rel_bias, buckets)


def _attn_kernel(q_ref, k_ref, vt_ref, bias_ref, lam_ref, g_ref, o_ref, m_ref, l_ref, acc_ref,
                 *, tb, nhead, out_scale, lambda_init, bounded):
    qi = pl.program_id(2)
    heads = range(nhead)
    lane = lax.broadcasted_iota(jnp.int32, (1, LANES), 1)

    def head_lanes(h):
        return slice(h * LANES, (h + 1) * LANES)

    def stacked_q(h):
        q = q_ref[:, head_lanes(h)]
        zero = jnp.zeros_like(q)
        return jnp.concatenate([jnp.where(lane < DIFF_HEAD, q, zero),
                                jnp.where(lane < DIFF_HEAD, zero, q)], axis=0)

    qs = [stacked_q(h) for h in heads]
    m_ref[...] = jnp.full_like(m_ref, NEG_BIG)
    l_ref[...] = jnp.zeros_like(l_ref)
    acc_ref[...] = jnp.zeros_like(acc_ref)

    def advance(j, biases):
        nblk = len(biases)
        rows = pl.ds(pl.multiple_of(j * tb, tb), nblk * tb)
        logits = lambda h: _dot_nt(k_ref[rows, head_lanes(h)], qs[h])
        ahead = min(3, nhead)
        st = {h: logits(h) for h in range(ahead)}
        spans = []
        for i, b in enumerate(biases):
            if b is None and spans and spans[-1][2] is None:
                spans[-1] = (spans[-1][0], i + 1, None)
            else:
                spans.append((i, i + 1, b))
        for h in heads:
            if h + ahead < nhead:
                st[h + ahead] = logits(h + ahead)
            s = st.pop(h)
            parts = [s[i * tb:e * tb] if b is None else s[i * tb:e * tb] + bias_ref[h, b] for i, e, b in spans]
            if bounded:
                ps = [jnp.exp2(x) for x in parts]
                l_new = l_ref[h]
            else:
                m_old = m_ref[h]
                m_new = m_old
                for x in parts:
                    m_new = jnp.maximum(m_new, jnp.max(x, axis=0, keepdims=True))
                alpha = jnp.exp2(m_old - m_new)
                ps = [jnp.exp2(x - m_new) for x in parts]
                l_new = alpha * l_ref[h]
                m_ref[h] = m_new
            for p in ps:
                l_new = l_new + jnp.sum(p, axis=0, keepdims=True)
            l_ref[h] = l_new
            pv = None
            for (i, e, _), p in zip(spans, ps):
                p = p.astype(BF16)
                for b in range(i, e):
                    term = _dot(vt_ref[j + b, head_lanes(h), :], p[(b - i) * tb:(b - i + 1) * tb])
                    pv = term if pv is None else pv + term
            acc_ref[h] = acc_ref[h] + pv if bounded else alpha * acc_ref[h] + pv

    def far_blocks(t, carry):
        for i in range(0, FAR_BLOCKS, 2):
            advance(FAR_BLOCKS * t + i, (None, None))
        return carry

    nfar = jnp.maximum(qi - 1, 0)
    lax.fori_loop(0, nfar // FAR_BLOCKS, far_blocks, 0)

    @pl.when(qi == 0)
    def _():
        advance(0, (0,))

    for left in range(FAR_BLOCKS):
        @pl.when((qi >= 1) & (nfar % FAR_BLOCKS == left))
        def _(left=left):
            for i in range(0, left - 1, 2):
                advance(qi - 1 - left + i, (None, None))
            advance(qi - 1 - left % 2, (None,) * (left % 2) + (1, 0))

    lam = lam_ref[...]
    lam_full = (jnp.exp(jnp.sum(lam[0:1] * lam[1:2], axis=1, keepdims=True))
                - jnp.exp(jnp.sum(lam[2:3] * lam[3:4], axis=1, keepdims=True)) + lambda_init)
    for h in heads:
        o = acc_ref[h] * (1.0 / l_ref[h])
        o = o[:, :tb] - lam_full * o[:, tb:]
        o = o * lax.rsqrt(jnp.mean(o * o, axis=0, keepdims=True) + NORM_EPS)
        o_ref[head_lanes(h), :] = (o * (g_ref[...] * out_scale)).astype(o_ref.dtype)


def _attn(q, k, vt, bias, lam, subln_g, batch, seq, tb, nhead, layer_idx, bounded):
    n, d = q.shape
    ngroup = d // (LANES * nhead)
    width = LANES * nhead
    nq = seq // tb
    lambda_init = 0.8 - 0.6 * math.exp(-0.3 * layer_idx)
    q_spec = pl.BlockSpec((tb, width), lambda b, h, i: (b * nq + i, h))
    return pl.pallas_call(
        functools.partial(_attn_kernel, tb=tb, nhead=nhead, out_scale=1.0 - lambda_init,
                          lambda_init=lambda_init, bounded=bounded),
        grid=(batch, ngroup, nq),
        in_specs=[q_spec,
                  pl.BlockSpec((seq, width), lambda b, h, i: (b, h)),
                  pl.BlockSpec((nq, width, tb), lambda b, h, i: (b, h, 0)),
                  pl.BlockSpec((nhead, 2, tb, 2 * tb), lambda b, h, i: (h, 0, 0, 0)),
                  pl.BlockSpec(lam.shape, lambda b, h, i: (0, 0)),
                  pl.BlockSpec(subln_g.shape, lambda b, h, i: (0, 0))],
        out_specs=pl.BlockSpec((width, tb), lambda b, h, i: (h, b * nq + i)),
        out_shape=jax.ShapeDtypeStruct((d, n), BF16),
        scratch_shapes=[pltpu.VMEM((nhead, 1, 2 * tb), F32), pltpu.VMEM((nhead, 1, 2 * tb), F32),
                        pltpu.VMEM((nhead, LANES, 2 * tb), F32)],
        compiler_params=pltpu.CompilerParams(
            dimension_semantics=("arbitrary", "arbitrary", "arbitrary"),
            vmem_limit_bytes=VMEM_LIMIT),
        name="attn",
    )(q, k, vt, bias, lam, subln_g)


def kernel(x, p, norm_g, mlp_w1, mlp_w2, ple_w_up, ple_w_gate, rwkv_mix, rwkv_w_rkvo, rwkv_w0, rwkv_w1, rwkv_w2, rwkv_a0, rwkv_a1, rwkv_a2, rwkv_g1, rwkv_g2, rwkv_k_k, rwkv_k_a, rwkv_r_k, rwkv_ln_w, rwkv_ln_b, kv_norm_g, w_k_shared, w_v_shared, k_norm_g, diff_w_q, diff_q_norm_g, diff_lam, diff_subln_g, diff_w_o, rel_bias):
    batch, seq, d = x.shape
    assert d == D_MODEL and norm_g.shape[0] == 2
    n = batch * seq
    tm = min(256, seq)
    tb_rec = min(256, seq)
    tb_attn = min(256, seq)
    assert seq % tm == 0 and seq % tb_rec == 0 and seq % tb_attn == 0 and tb_attn >= MAX_DIST

    bf = lambda w: w.astype(BF16)
    x2d = x.reshape(n, d)
    p3d = p.reshape(p.shape[0], n, p.shape[-1])
    head_id = jnp.arange(d, dtype=jnp.int32) // RWKV_HEAD
    slot = jnp.arange(LANES, dtype=jnp.int32)
    bc = ((head_id[:, None] == slot[None, :] % N_HEADS) & (slot[None, :] < 2 * N_HEADS)).astype(BF16)
    be = bc.T

    vecs = jnp.stack([norm_g[0, 0], rwkv_w0[0], rwkv_a0[0], rwkv_k_k[0], rwkv_k_a[0],
                      rwkv_r_k[0].reshape(d), rwkv_ln_w[0], rwkv_ln_b[0]])
    y, g, bonus = _rwkv(
        x2d, vecs, rwkv_mix[0], bf(rwkv_w_rkvo[0, 0]), bf(rwkv_w_rkvo[0, 1]), bf(rwkv_w_rkvo[0, 2]),
        bf(rwkv_w1[0]), bf(rwkv_w2[0]), bf(rwkv_a1[0]), bf(rwkv_a2[0]), bf(rwkv_g1[0]), bf(rwkv_g2[0]),
        bc, be, batch, seq, tb_rec)
    reps = d // DIFF_HEAD
    gnq = jnp.stack([norm_g[1, 0], kv_norm_g,
                     jnp.tile(diff_q_norm_g[0], reps) * (DIFF_HEAD ** -0.5 * LOG2E), jnp.tile(k_norm_g, reps)])
    x2d, q, kq, vq = _tail(functools.partial(_tail_rwkv_kernel, nslab=2), (x2d, y, bonus, g), (), p3d, 0,
                           (vecs, bc, be, gnq, bf(diff_w_q[0]), bf(w_k_shared), bf(w_v_shared)), norm_g[0, 1:3],
                           bf(rwkv_w_rkvo[0, 3]), bf(mlp_w1[0]), bf(mlp_w2[0]), bf(ple_w_up[0]),
                           bf(ple_w_gate[0]), 2 * tb_attn, key_block=tb_attn)

    bias = _bias_tiles(rel_bias, tb_attn)
    logit_bound = 1.02 * LOG2E * (DIFF_HEAD ** 0.5 * jnp.max(jnp.abs(diff_q_norm_g[0] * k_norm_g))
                                  + jnp.max(jnp.abs(rel_bias - rel_bias[-1:])))
    attn = functools.partial(_attn, q, kq, vq, bias, diff_lam[0], diff_subln_g[0].reshape(LANES, 1),
                             batch, seq, tb_attn, 4, 1)
    o = lax.cond(logit_bound <= LOGIT_BOUND, lambda: attn(True), lambda: attn(False))
    x2d = _tail(functools.partial(_tail_attn_kernel, nslab=2), (x2d,), (o,), p3d, 1, (), norm_g[1, 1:3],
                bf(diff_w_o[0]), bf(mlp_w1[1]), bf(mlp_w2[1]), bf(ple_w_up[1]), bf(ple_w_gate[1]), 2 * tm)
    return x2d.reshape(batch, seq, d)
```

```python
import functools
import math

import numpy as np
import jax
import jax.numpy as jnp
from jax import lax
from jax.experimental import pallas as pl
from jax.experimental.pallas import tpu as pltpu

F32 = jnp.float32
BF16 = jnp.bfloat16

D_MODEL = 1024
RWKV_HEAD = 64
N_HEADS = D_MODEL // RWKV_HEAD
DIFF_HEAD = 64
DIFF_HEADS = 8
LANES = 128
CHUNK = 64
MIN_PACK = 16
SETUP_AHEAD = 4
PRE_STAGE_EVERY = 5
FF_CHUNK = 1024
NORM_EPS = 1e-6
GN_EPS = 64e-5
N_BUCKETS = 32
MAX_DIST = 128
NEG_BIG = -1e30
LOG2E = math.log2(math.e)
FAR_BLOCKS = 4
LOGIT_BOUND = 60.0
VMEM_LIMIT = 60 * 1024 * 1024


def _dot(a, b):
    return jnp.dot(a.astype(BF16), b.astype(BF16), preferred_element_type=F32)


def _dot_nt(a, b):
    return lax.dot_general(a.astype(BF16), b.astype(BF16), (((1,), (1,)), ((), ())),
                           preferred_element_type=F32)


def _dot_tn(a, b):
    return lax.dot_general(a.astype(BF16), b.astype(BF16), (((0,), (0,)), ((), ())),
                           preferred_element_type=F32)


def _rms(x, g):
    return x * lax.rsqrt(jnp.mean(x * x, axis=-1, keepdims=True) + NORM_EPS) * g


def _sigmoid(x):
    return 1.0 / (1.0 + jnp.exp(-x))


def _head_reduce(x, bc_ref):
    return _dot(x, bc_ref[...])


def _head_expand(c, be_ref):
    lane = lax.broadcasted_iota(jnp.int32, (1, LANES), 1)
    hi = c.astype(BF16).astype(F32)
    return _dot(jnp.where(lane < N_HEADS, hi, c - hi), be_ref[...])


def _const_spec(shape):
    nd = len(shape)
    return pl.BlockSpec(shape, lambda *_: (0,) * nd, pipeline_mode=pl.Buffered(1))


def _row_spec(tm, width):
    return pl.BlockSpec((tm, width), lambda i: (i, 0))


def _pre_stages(x_ref, xp_ref, vec_ref, mix_ref, wr_ref, wk_ref, wv_ref,
                w1_ref, w2_ref, a1_ref, a2_ref, g1_ref, g2_ref, bc_ref, be_ref,
                r_out, lw_out, k_out, v_out, kk_out, a_out, g_out, bonus_out, *, first_block):
    g0 = vec_ref[0:1, :]
    h = _rms(x_ref[...], g0)
    prev = _rms(xp_ref[...], g0)[7:8, :]
    prev = jnp.where(first_block, 0.0, prev)
    row = lax.broadcasted_iota(jnp.int32, h.shape, 0)
    hprev = jnp.where(row == 0, prev, pltpu.roll(h, 1, 0))
    dx = hprev - h

    def mixed(j):
        return (h + dx * mix_ref[j:j + 1, :]).astype(BF16)

    yield
    r = _dot(mixed(0), wr_ref[...])
    r_out[...] = r
    yield
    k = _dot(mixed(2), wk_ref[...])
    yield
    v = _dot(mixed(3), wv_ref[...])
    v_out[...] = v
    yield
    t_w = jnp.tanh(_dot(mixed(1), w1_ref[...]))
    yield
    t_a = _dot(mixed(4), a1_ref[...])
    t_g = _sigmoid(_dot(mixed(5), g1_ref[...]))
    yield
    lw_out[...] = -math.exp(-0.5) * _sigmoid(vec_ref[1:2, :] + _dot(t_w, w2_ref[...]))
    yield
    a = _sigmoid(vec_ref[2:3, :] + _dot(t_a, a2_ref[...]))
    a_out[...] = a
    yield
    g_out[...] = _dot(t_g, g2_ref[...]).astype(g_out.dtype)
    yield
    kk = k * vec_ref[3:4, :]
    ss = _head_reduce(kk * kk, bc_ref)
    k = k * (1.0 + (a - 1.0) * vec_ref[4:5, :])
    k_out[...] = k
    bsum = _head_reduce(r * k * vec_ref[5:6, :], bc_ref)
    yield
    kk_out[...] = kk * _head_expand(1.0 / jnp.maximum(jnp.sqrt(ss), 1e-12), be_ref)
    bonus_out[...] = (_head_expand(bsum, be_ref) * v).astype(bonus_out.dtype)


def _rec_stages(r_ref, lw_ref, k_ref, v_ref, kk_ref, a_ref, y_ref, s_ref, *, nchunk, npair, first_block):
    c2 = 2 * CHUNK
    width = npair * LANES

    @pl.when(first_block)
    def _():
        s_ref[...] = jnp.zeros_like(s_ref)

    ri = lax.broadcasted_iota(jnp.int32, (c2, c2), 0)
    ci = lax.broadcasted_iota(jnp.int32, (c2, c2), 1)
    same_block = {m: (ri // m) == (ci // m) for m in (2, 16, 32, 64)}
    sibling_block = {m: ((ri // m) ^ (ci // m)) == 1 for m in (2, 4, 8, 16, 32)}
    eye = (ri == ci).astype(F32)
    gi = lax.broadcasted_iota(jnp.int32, (2 * c2, 2 * c2), 0)
    gj = lax.broadcasted_iota(jnp.int32, (2 * c2, 2 * c2), 1)
    gram_mask = (gj % CHUNK) < (gi % CHUNK) + jnp.where(gi < c2, 0, 1)
    lo_lane = (lax.broadcasted_iota(jnp.int32, (1, width), 1) % LANES) < CHUNK
    tri = (lax.broadcasted_iota(jnp.int32, (CHUNK, CHUNK), 1)
           <= lax.broadcasted_iota(jnp.int32, (CHUNK, CHUNK), 0)).astype(BF16)
    pairs = range(npair)

    def pack(m, blk):
        return m.reshape(c2 // blk, blk, c2).sum(axis=0)

    def unpack(p, blk):
        return jnp.where(same_block[blk], jnp.tile(p, (c2 // blk, 1)), 0.0)

    def per_pair(x):
        return [x[:, p * LANES:(p + 1) * LANES] for p in pairs]

    def stacked(x):
        return [jnp.concatenate([a, b], axis=0)
                for a, b in zip(per_pair(jnp.where(lo_lane, x, 0.0)), per_pair(jnp.where(lo_lane, 0.0, x)))]

    def setup(c, out):
        sl = slice(c * CHUNK, (c + 1) * CHUNK)
        lw = lw_ref[sl, :]
        l_hi = lw.astype(BF16)
        l_mid = (lw - l_hi.astype(F32)).astype(BF16)
        l_lo = (lw - l_hi.astype(F32) - l_mid.astype(F32)).astype(BF16)
        cum = _dot(tri, l_hi) + _dot(tri, l_mid) + _dot(tri, l_lo)
        tot = cum[CHUNK - 1:CHUNK, :]
        e_neg = jnp.exp(-cum)
        e_rest = jnp.exp(tot - cum)
        k = k_ref[sl, :]
        kk = kk_ref[sl, :]
        kb = kk * a_ref[sl, :]
        at = stacked(-kk * jnp.exp(cum - lw))
        rt = stacked(r_ref[sl, :] * jnp.exp(cum))
        bt = stacked(kb * e_neg)
        kt = stacked(k * e_neg)
        out["bkw"] = [jnp.concatenate([b, kq], axis=0).astype(BF16)
                      for b, kq in zip(stacked(kb * e_rest), stacked(k * e_rest))]
        out["vm"] = [v.astype(BF16) for v in stacked(v_ref[sl, :])]
        out["decay"] = per_pair(jnp.exp(tot))
        lhs = out["lhs"] = [jnp.concatenate([a, r], axis=0).astype(BF16) for a, r in zip(at, rt)]
        yield
        gram = [jnp.where(gram_mask, _dot_nt(l, jnp.concatenate([b, kq], axis=0)), 0.0)
                for l, b, kq in zip(lhs, bt, kt)]
        out["a_k"] = [g[:c2, c2:].astype(BF16) for g in gram]
        out["r_bk"] = [g[c2:].astype(BF16) for g in gram]
        abd = [g[:c2, :c2] for g in gram]
        yield
        x = [eye + jnp.where(same_block[2], a, 0.0) for a in abd]
        blk = MIN_PACK
        xp = [pack(xi, blk) for xi in x]
        for m in (2, 4, 8):
            below = [jnp.where(sibling_block[m], a, 0.0) for a in abd]
            lx = [_dot(pack(l, blk), xi) for l, xi in zip(below, x)]
            yield
            xlx = [_dot(xpi, unpack(y, blk)) for xpi, y in zip(xp, lx)]
            yield
            x = [xi + unpack(y, blk) for xi, y in zip(x, xlx)]
            xp = [xpi + y for xpi, y in zip(xp, xlx)]
        for m in (16, 32):
            lx = [_dot(jnp.where(sibling_block[m], a, 0.0), xi) for a, xi in zip(abd, x)]
            yield
            xlx = [_dot(xi, y) for xi, y in zip(x, lx)]
            yield
            x = [xi + y for xi, y in zip(x, xlx)]
        out["tinv"] = [xi.astype(BF16) for xi in x]

    def carry(c, pre):
        sl = slice(c * CHUNK, (c + 1) * CHUNK)
        vm = pre["vm"]
        hs = [_dot_nt(l, s_ref[p]) for p, l in zip(pairs, pre["lhs"])]
        yield
        w = [h[:c2] + _dot(g, v) for h, g, v in zip(hs, pre["a_k"], vm)]
        yield
        u = [_dot(ti, wi) for ti, wi in zip(pre["tinv"], w)]
        uv = [jnp.concatenate([ui.astype(BF16), v], axis=0) for ui, v in zip(u, vm)]
        yield
        ym = [h[c2:] + _dot(g, x2) for h, g, x2 in zip(hs, pre["r_bk"], uv)]
        y_ref[sl, :] = jnp.concatenate([yi[:CHUNK] + yi[CHUNK:] for yi in ym], axis=1)
        yield
        for p in pairs:
            s_ref[p] = s_ref[p] * pre["decay"][p] + _dot_tn(uv[p], pre["bkw"][p])

    pre = [dict() for _ in range(nchunk)]

    def take_turns(active):
        while active:
            for gen in list(active):
                if next(gen, StopIteration) is StopIteration:
                    active.remove(gen)
                else:
                    yield

    yield from take_turns([setup(c, pre[c]) for c in range(min(SETUP_AHEAD, nchunk))])
    for c in range(nchunk):
        ahead = c + SETUP_AHEAD
        yield from take_turns([carry(c, pre[c])] + ([setup(ahead, pre[ahead])] if ahead < nchunk else []))


def _drain(gen):
    for _ in gen:
        pass


def _rwkv_kernel(*refs, nchunk, npair, nt):
    ins, (y_out, g_out, bonus_out), (r_s, lw_s, k_s, v_s, kk_s, a_s, s_ref) = refs[:15], refs[15:18], refs[18:]
    s = pl.program_id(1)
    bufs = (r_s, lw_s, k_s, v_s, kk_s, a_s)
    pre = lambda: _pre_stages(*ins, *[b.at[s % 2] for b in bufs], g_out, bonus_out, first_block=s == 0)
    rec = lambda: _rec_stages(*[b.at[(s + 1) % 2] for b in bufs], y_out, s_ref, nchunk=nchunk, npair=npair,
                              first_block=s == 1)

    @pl.when(s == 0)
    def _():
        _drain(pre())

    @pl.when((s > 0) & (s < nt))
    def _():
        side = pre()
        for i, _ in enumerate(rec()):
            if i % PRE_STAGE_EVERY == 0:
                next(side, None)
        _drain(side)

    @pl.when(s == nt)
    def _():
        _drain(rec())


def _rwkv(x2d, vecs, mix, wr, wk, wv, w1, w2, a1, a2, g1, g2, bc, be, batch, seq, tb):
    n, d = x2d.shape
    nt = seq // tb
    npair = d // LANES
    consts = (vecs, mix, wr, wk, wv, w1, w2, a1, a2, g1, g2, bc, be)
    pre_blk = lambda b, s: b * nt + jnp.minimum(s, nt - 1)
    rec_blk = lambda b, s: b * nt + jnp.maximum(s - 1, 0)
    in_spec = pl.BlockSpec((tb, d), lambda b, s: (pre_blk(b, s), 0))
    prev_spec = pl.BlockSpec((8, d), lambda b, s: (jnp.maximum(pre_blk(b, s) * (tb // 8) - 1, 0), 0))
    return pl.pallas_call(
        functools.partial(_rwkv_kernel, nchunk=tb // CHUNK, npair=npair, nt=nt),
        grid=(batch, nt + 1),
        in_specs=[in_spec, prev_spec] + [_const_spec(c.shape) for c in consts],
        out_specs=[pl.BlockSpec((tb, d), lambda b, s: (rec_blk(b, s), 0)), in_spec, in_spec],
        out_shape=[jax.ShapeDtypeStruct((n, d), F32)] + [jax.ShapeDtypeStruct((n, d), BF16)] * 2,
        scratch_shapes=[pltpu.VMEM((2, tb, d), F32)] * 6 + [pltpu.VMEM((npair, LANES, LANES), F32)],
        compiler_params=pltpu.CompilerParams(dimension_semantics=("arbitrary", "arbitrary"),
                                             vmem_limit_bytes=VMEM_LIMIT),
        name="rwkv",
    )(x2d, x2d, *consts)


def _tail_stages(x, mixer_out, p, gn_ref, w1_ref, w2_ref, wup_ref, wgate_ref, store):
    x = x + mixer_out()
    yield
    hn = _rms(x, gn_ref[0:1, :]).astype(BF16)
    d_ff = w1_ref.shape[1]
    acc = x
    for c in range(0, d_ff, FF_CHUNK):
        mid = jnp.maximum(_dot(hn, w1_ref[:, c:c + FF_CHUNK]), 0.0)
        yield
        acc = acc + _dot(mid * mid, w2_ref[c:c + FF_CHUNK, :])
        yield
    gate = _sigmoid(_dot(_rms(acc, gn_ref[1:2, :]), wgate_ref[...]))
    yield
    store(acc + _dot(p, wup_ref[...]) * gate)


def _round_robin(gens):
    gens = list(gens)
    while gens:
        gens = [g for g in gens if next(g, StopIteration) is not StopIteration]


def _tail_attn_kernel(x_ref, o_ref, p_ref, gn_ref, wo_ref, w1_ref, w2_ref, wup_ref, wgate_ref, out_ref, *, nslab):
    rows_per = x_ref.shape[0] // nslab

    def slab(i):
        rows = slice(i * rows_per, (i + 1) * rows_per)

        def store(v):
            out_ref[rows, :] = v

        return _tail_stages(x_ref[rows, :], lambda: _dot_tn(o_ref[:, rows], wo_ref[...]), p_ref[rows, :],
                            gn_ref, w1_ref, w2_ref, wup_ref, wgate_ref, store)

    _round_robin(slab(i) for i in range(nslab))


def _tail_rwkv_kernel(x_ref, y_ref, bonus_ref, g_ref, p_ref, vec_ref, bc_ref, be_ref, gnq_ref, wq_ref, wk_ref,
                      wv_ref, gn_ref, wo_ref, w1_ref, w2_ref, wup_ref, wgate_ref, out_ref, q_out, k_out, v_out,
                      *, nslab):
    rows_per = x_ref.shape[0] // nslab
    assert v_out.shape == (nslab, x_ref.shape[1], rows_per)
    inv_n = 1.0 / RWKV_HEAD

    def slab(i):
        rows = slice(i * rows_per, (i + 1) * rows_per)
        y = y_ref[rows, :]
        mean = _head_reduce(y, bc_ref) * inv_n
        yield
        yc = y - _head_expand(mean, be_ref)
        yield
        var = _head_reduce(yc * yc, bc_ref) * inv_n
        yield
        yn = yc * _head_expand(lax.rsqrt(var + GN_EPS), be_ref) * vec_ref[6:7, :] + vec_ref[7:8, :]
        yield
        fresh = []

        def store(v):
            out_ref[rows, :] = v
            fresh.append(v)

        gated = (yn + bonus_ref[rows, :]) * g_ref[rows, :]
        yield from _tail_stages(x_ref[rows, :], lambda: _dot(gated, wo_ref[...]), p_ref[rows, :],
                                gn_ref, w1_ref, w2_ref, wup_ref, wgate_ref, store)
        yield
        yield from _qkv_stages(fresh[0], gnq_ref, wq_ref, wk_ref, wv_ref, bc_ref, be_ref, q_out, k_out, v_out,
                               rows, i)

    _round_robin(slab(i) for i in range(nslab))


def _tail(kernel_fn, rows, cols, p3d, layer, extra_consts, gn, wo, w1, w2, wup, wgate, tm, key_block=None):
    n, d = rows[0].shape
    row_specs = [_row_spec(tm, d)] * len(rows) + [pl.BlockSpec((d, tm), lambda i: (0, i))] * len(cols)
    consts = tuple(extra_consts) + (gn, wo, w1, w2, wup, wgate)
    out_specs = [_row_spec(tm, d)]
    out_shape = [jax.ShapeDtypeStruct((n, d), F32)]
    if key_block:
        out_specs += [_row_spec(tm, d)] * 2 + [pl.BlockSpec((tm // key_block, d, key_block), lambda i: (i, 0, 0))]
        out_shape += [jax.ShapeDtypeStruct((n, d), BF16)] * 2
        out_shape += [jax.ShapeDtypeStruct((n // key_block, d, key_block), BF16)]
    outs = pl.pallas_call(
        kernel_fn,
        grid=(n // tm,),
        in_specs=row_specs
                 + [pl.BlockSpec((None, tm, p3d.shape[2]), lambda i: (layer, i, 0))]
                 + [_const_spec(c.shape) for c in consts],
        out_specs=out_specs,
        out_shape=out_shape,
        compiler_params=pltpu.CompilerParams(dimension_semantics=("arbitrary",),
                                             vmem_limit_bytes=VMEM_LIMIT),
        name="tail",
    )(*rows, *cols, p3d, *consts)
    return outs if key_block else outs[0]


def _qkv_stages(x, gn_ref, wq_ref, wk_ref, wv_ref, bc_ref, be_ref, q_out, k_out, v_out, rows, slab):
    inv_n = 1.0 / DIFF_HEAD
    hk = _rms(x, gn_ref[1:2, :]).astype(BF16)
    tq = _dot(_rms(x, gn_ref[0:1, :]), wq_ref[...])
    yield
    ms = _head_reduce(tq * tq, bc_ref) * inv_n
    yield
    q = tq * _head_expand(lax.rsqrt(ms + NORM_EPS), be_ref) * gn_ref[2:3, :]
    q_out[rows, :] = q.astype(q_out.dtype)
    yield
    tk = _dot(hk, wk_ref[...])
    yield
    ms = _head_reduce(tk * tk, bc_ref) * inv_n
    yield
    k_out[rows, :] = (tk * _head_expand(lax.rsqrt(ms + NORM_EPS), be_ref) * gn_ref[3:4, :]).astype(k_out.dtype)
    yield
    v_out[slab] = _dot(hk, wv_ref[...]).T.astype(v_out.dtype)


def _bucket_tiles(tb):
    i = np.arange(tb, dtype=np.int64)[None, :]
    j = np.arange(tb, dtype=np.int64)[:, None]

    def bucket(rel):
        n = np.maximum(rel, 0)
        max_exact = N_BUCKETS // 2
        nf = np.maximum(n, 1).astype(np.float32)
        large = max_exact + (np.log(nf / np.float32(max_exact)) / np.float32(math.log(MAX_DIST / max_exact))
                             * np.float32(N_BUCKETS - max_exact)).astype(np.int32)
        large = np.minimum(large, N_BUCKETS - 1)
        return np.where(n < max_exact, n, large).astype(np.int32)

    diag = np.where(i - j >= 0, bucket(i - j), -1)
    near = bucket(tb + i - j)
    tiles = np.stack([diag, near]).astype(np.int32)
    return np.concatenate([tiles, tiles], axis=2)


def _bias_kernel(tab_ref, bucket_ref, out_ref):
    h = pl.program_id(0)
    b = bucket_ref[...]
    far = tab_ref[N_BUCKETS - 1, h]
    acc = jnp.where(b < 0, NEG_BIG, 0.0)
    for n in range(N_BUCKETS - 1):
        acc = jnp.where(b == n, (tab_ref[n, h] - far) * LOG2E, acc)
    out_ref[0] = acc


def _bias_tiles(rel_bias, tb):
    buckets = jnp.asarray(_bucket_tiles(tb))
    nh = rel_bias.shape[1]
    return pl.pallas_call(
        _bias_kernel,
        grid=(nh,),
        in_specs=[pl.BlockSpec(memory_space=pltpu.SMEM),
                  pl.BlockSpec((2, tb, 2 * tb), lambda h: (0, 0, 0))],
        out_specs=pl.BlockSpec((1, 2, tb, 2 * tb), lambda h: (h, 0, 0, 0)),
        out_shape=jax.ShapeDtypeStruct((nh, 2, tb, 2 * tb), F32),
        compiler_params=pltpu.CompilerParams(dimension_semantics=("arbitrary",)),
        name="bias",
    )(rel_bias, buckets)


def _attn_kernel(q_ref, k_ref, vt_ref, bias_ref, lam_ref, g_ref, o_ref, m_ref, l_ref, acc_ref,
                 *, tb, nhead, out_scale, lambda_init, bounded):
    qi = pl.program_id(2)
    heads = range(nhead)
    lane = lax.broadcasted_iota(jnp.int32, (1, LANES), 1)

    def head_lanes(h):
        return slice(h * LANES, (h + 1) * LANES)

    def stacked_q(h):
        q = q_ref[:, head_lanes(h)]
        zero = jnp.zeros_like(q)
        return jnp.concatenate([jnp.where(lane < DIFF_HEAD, q, zero),
                                jnp.where(lane < DIFF_HEAD, zero, q)], axis=0)

    qs = [stacked_q(h) for h in heads]
    m_ref[...] = jnp.full_like(m_ref, NEG_BIG)
    l_ref[...] = jnp.zeros_like(l_ref)
    acc_ref[...] = jnp.zeros_like(acc_ref)

    def advance(j, biases):
        nblk = len(biases)
        rows = pl.ds(pl.multiple_of(j * tb, tb), nblk * tb)
        logits = lambda h: _dot_nt(k_ref[rows, head_lanes(h)], qs[h])
        ahead = min(3, nhead)
        st = {h: logits(h) for h in range(ahead)}
        spans = []
        for i, b in enumerate(biases):
            if b is None and spans and spans[-1][2] is None:
                spans[-1] = (spans[-1][0], i + 1, None)
            else:
                spans.append((i, i + 1, b))
        for h in heads:
            if h + ahead < nhead:
                st[h + ahead] = logits(h + ahead)
            s = st.pop(h)
            parts = [s[i * tb:e * tb] if b is None else s[i * tb:e * tb] + bias_ref[h, b] for i, e, b in spans]
            if bounded:
                ps = [jnp.exp2(x) for x in parts]
                l_new = l_ref[h]
            else:
                m_old = m_ref[h]
                m_new = m_old
                for x in parts:
                    m_new = jnp.maximum(m_new, jnp.max(x, axis=0, keepdims=True))
                alpha = jnp.exp2(m_old - m_new)
                ps = [jnp.exp2(x - m_new) for x in parts]
                l_new = alpha * l_ref[h]
                m_ref[h] = m_new
            for p in ps:
                l_new = l_new + jnp.sum(p, axis=0, keepdims=True)
            l_ref[h] = l_new
            pv = None
            for (i, e, _), p in zip(spans, ps):
                p = p.astype(BF16)
                for b in range(i, e):
                    term = _dot(vt_ref[j + b, head_lanes(h), :], p[(b - i) * tb:(b - i + 1) * tb])
                    pv = term if pv is None else pv + term
            acc_ref[h] = acc_ref[h] + pv if bounded else alpha * acc_ref[h] + pv

    def far_blocks(t, carry):
        for i in range(0, FAR_BLOCKS, 2):
            advance(FAR_BLOCKS * t + i, (None, None))
        return carry

    nfar = jnp.maximum(qi - 1, 0)
    lax.fori_loop(0, nfar // FAR_BLOCKS, far_blocks, 0)

    @pl.when(qi == 0)
    def _():
        advance(0, (0,))

    for left in range(FAR_BLOCKS):
        @pl.when((qi >= 1) & (nfar % FAR_BLOCKS == left))
        def _(left=left):
            for i in range(0, left - 1, 2):
                advance(qi - 1 - left + i, (None, None))
            advance(qi - 1 - left % 2, (None,) * (left % 2) + (1, 0))

    lam = lam_ref[...]
    lam_full = (jnp.exp(jnp.sum(lam[0:1] * lam[1:2], axis=1, keepdims=True))
                - jnp.exp(jnp.sum(lam[2:3] * lam[3:4], axis=1, keepdims=True)) + lambda_init)
    for h in heads:
        o = acc_ref[h] * (1.0 / l_ref[h])
        o = o[:, :tb] - lam_full * o[:, tb:]
        o = o * lax.rsqrt(jnp.mean(o * o, axis=0, keepdims=True) + NORM_EPS)
        o_ref[head_lanes(h), :] = (o * (g_ref[...] * out_scale)).astype(o_ref.dtype)


def _attn(q, k, vt, bias, lam, subln_g, batch, seq, tb, nhead, layer_idx, bounded):
    n, d = q.shape
    ngroup = d // (LANES * nhead)
    width = LANES * nhead
    nq = seq // tb
    lambda_init = 0.8 - 0.6 * math.exp(-0.3 * layer_idx)
    q_spec = pl.BlockSpec((tb, width), lambda b, h, i: (b * nq + i, h))
    return pl.pallas_call(
        functools.partial(_attn_kernel, tb=tb, nhead=nhead, out_scale=1.0 - lambda_init,
                          lambda_init=lambda_init, bounded=bounded),
        grid=(batch, ngroup, nq),
        in_specs=[q_spec,
                  pl.BlockSpec((seq, width), lambda b, h, i: (b, h), pipeline_mode=pl.Buffered(1)),
                  pl.BlockSpec((nq, width, tb), lambda b, h, i: (b, h, 0), pipeline_mode=pl.Buffered(1)),
                  pl.BlockSpec((nhead, 2, tb, 2 * tb), lambda b, h, i: (h, 0, 0, 0),
                               pipeline_mode=pl.Buffered(1)),
                  pl.BlockSpec(lam.shape, lambda b, h, i: (0, 0)),
                  pl.BlockSpec(subln_g.shape, lambda b, h, i: (0, 0))],
        out_specs=pl.BlockSpec((width, tb), lambda b, h, i: (h, b * nq + i)),
        out_shape=jax.ShapeDtypeStruct((d, n), BF16),
        scratch_shapes=[pltpu.VMEM((nhead, 1, 2 * tb), F32), pltpu.VMEM((nhead, 1, 2 * tb), F32),
                        pltpu.VMEM((nhead, LANES, 2 * tb), F32)],
        compiler_params=pltpu.CompilerParams(
            dimension_semantics=("arbitrary", "arbitrary", "arbitrary"),
            vmem_limit_bytes=VMEM_LIMIT),
        name="attn",
    )(q, k, vt, bias, lam, subln_g)


def kernel(x, p, norm_g, mlp_w1, mlp_w2, ple_w_up, ple_w_gate, rwkv_mix, rwkv_w_rkvo, rwkv_w0, rwkv_w1, rwkv_w2, rwkv_a0, rwkv_a1, rwkv_a2, rwkv_g1, rwkv_g2, rwkv_k_k, rwkv_k_a, rwkv_r_k, rwkv_ln_w, rwkv_ln_b, kv_norm_g, w_k_shared, w_v_shared, k_norm_g, diff_w_q, diff_q_norm_g, diff_lam, diff_subln_g, diff_w_o, rel_bias):
    batch, seq, d = x.shape
    assert d == D_MODEL and norm_g.shape[0] == 2
    n = batch * seq
    tm = min(256, seq)
    tb_rec = min(256, seq)
    tb_attn = min(256, seq)
    assert seq % tm == 0 and seq % tb_rec == 0 and seq % tb_attn == 0 and tb_attn >= MAX_DIST

    bf = lambda w: w.astype(BF16)
    x2d = x.reshape(n, d)
    p3d = p.reshape(p.shape[0], n, p.shape[-1])
    head_id = jnp.arange(d, dtype=jnp.int32) // RWKV_HEAD
    slot = jnp.arange(LANES, dtype=jnp.int32)
    bc = ((head_id[:, None] == slot[None, :] % N_HEADS) & (slot[None, :] < 2 * N_HEADS)).astype(BF16)
    be = bc.T

    vecs = jnp.stack([norm_g[0, 0], rwkv_w0[0], rwkv_a0[0], rwkv_k_k[0], rwkv_k_a[0],
                      rwkv_r_k[0].reshape(d), rwkv_ln_w[0], rwkv_ln_b[0]])
    y, g, bonus = _rwkv(
        x2d, vecs, rwkv_mix[0], bf(rwkv_w_rkvo[0, 0]), bf(rwkv_w_rkvo[0, 1]), bf(rwkv_w_rkvo[0, 2]),
        bf(rwkv_w1[0]), bf(rwkv_w2[0]), bf(rwkv_a1[0]), bf(rwkv_a2[0]), bf(rwkv_g1[0]), bf(rwkv_g2[0]),
        bc, be, batch, seq, tb_rec)
    reps = d // DIFF_HEAD
    gnq = jnp.stack([norm_g[1, 0], kv_norm_g,
                     jnp.tile(diff_q_norm_g[0], reps) * (DIFF_HEAD ** -0.5 * LOG2E), jnp.tile(k_norm_g, reps)])
    x2d, q, kq, vq = _tail(functools.partial(_tail_rwkv_kernel, nslab=2), (x2d, y, bonus, g), (), p3d, 0,
                           (vecs, bc, be, gnq, bf(diff_w_q[0]), bf(w_k_shared), bf(w_v_shared)), norm_g[0, 1:3],
                           bf(rwkv_w_rkvo[0, 3]), bf(mlp_w1[0]), bf(mlp_w2[0]), bf(ple_w_up[0]),
                           bf(ple_w_gate[0]), 2 * tb_attn, key_block=tb_attn)

    bias = _bias_tiles(rel_bias, tb_attn)
    logit_bound = 1.02 * LOG2E * (DIFF_HEAD ** 0.5 * jnp.max(jnp.abs(diff_q_norm_g[0] * k_norm_g))
                                  + jnp.max(jnp.abs(rel_bias - rel_bias[-1:])))
    attn = functools.partial(_attn, q, kq, vq, bias, diff_lam[0], diff_subln_g[0].reshape(LANES, 1),
                             batch, seq, tb_attn, 4, 1)
    o = lax.cond(logit_bound <= LOGIT_BOUND, lambda: attn(True), lambda: attn(False))
    x2d = _tail(functools.partial(_tail_attn_kernel, nslab=2), (x2d,), (o,), p3d, 1, (), norm_g[1, 1:3],
                bf(diff_w_o[0]), bf(mlp_w1[1]), bf(mlp_w2[1]), bf(ple_w_up[1]), bf(ple_w_gate[1]), 2 * tm)
    return x2d.reshape(batch, seq, d)
```
